```python
import jax, jax.numpy as jnp
from jax import lax
import numpy as np

D_MODEL = 2048
BATCH = 2
SEQ = 4096
DEPTH = 1
DEC_BATCH = 32
DEC_SEQ = 4
PAST_LEN = 8192
PAGE_SIZE = 128

N_HEADS = 8
N_KV_HEADS = 2
GQA = N_HEADS // N_KV_HEADS
HEAD_DIM = 128
ATTN_W = N_HEADS * HEAD_DIM
KV_W = N_KV_HEADS * HEAD_DIM
GATE_COLS = 3 * N_HEADS
BLOCK_CMP = 32
STRIDE_CMP = 16
CMP_HID = 256
SEL_BLOCK = 64
TOP_N = 16
WINDOW = 512
Q_BLOCK = 128
CONV_CH = D_MODEL - ATTN_W
CONV_W = 31
IN_COLS = ATTN_W + 6 * KV_W + GATE_COLS + 2 * CONV_CH
D_FF = 5632
FFN_CONV_W = 3
N_MEM = 256
MEM_HEADS = 4
MEM_HEAD_DIM = 128
MEM_W = MEM_HEADS * MEM_HEAD_DIM
ATTN_SCALE = HEAD_DIM ** -0.5
MEM_SCALE = MEM_HEAD_DIM ** -0.5
EPS = 1e-6

kernel_name = 'hymba_nsa_conformer_step'


def rmsnorm(x, g):
    xf = x.astype(jnp.float32)
    y = xf * lax.rsqrt(jnp.mean(xf * xf, axis=-1, keepdims=True) + EPS)
    return y.astype(x.dtype) * g


def layernorm(x, g, b):
    xf = x.astype(jnp.float32)
    mu = jnp.mean(xf, axis=-1, keepdims=True)
    var = jnp.mean(jnp.square(xf - mu), axis=-1, keepdims=True)
    return ((xf - mu) * lax.rsqrt(var + EPS)).astype(x.dtype) * g + b


def masked_softmax(s, mask):
    s = jnp.where(mask, s.astype(jnp.float32), -jnp.inf)
    m = jnp.max(s, axis=-1, keepdims=True)
    m = jnp.where(jnp.isfinite(m), m, 0.0)
    e = jnp.where(mask, jnp.exp(s - m), 0.0)
    return e / jnp.maximum(jnp.sum(e, axis=-1, keepdims=True), 1e-30)


def causal_dwconv(ext, w, b):
    y = lax.conv_general_dilated(ext, w[:, None, :].astype(ext.dtype), window_strides=(1,), padding='VALID',
                                 dimension_numbers=('NWC', 'WIO', 'NWC'), feature_group_count=ext.shape[-1])
    return y + b


def project(h, w_in, b_gate):
    B, T, _ = h.shape
    sizes = [ATTN_W] + [KV_W] * 6 + [GATE_COLS, 2 * CONV_CH]
    q, kc, vc, ks, vs, kw, vw, g, glu = jnp.split(h @ w_in, np.cumsum(sizes)[:-1].tolist(), axis=-1)
    kv = lambda a: a.reshape(B, T, N_KV_HEADS, HEAD_DIM)
    q = q.reshape(B, T, N_KV_HEADS, GQA, HEAD_DIM).transpose(0, 2, 3, 1, 4)
    gates = jax.nn.sigmoid(g + b_gate).reshape(B, T, N_KV_HEADS, GQA, 3).transpose(0, 2, 3, 1, 4)
    return q, kv(kc), kv(vc), kv(ks), kv(vs), kv(kw), kv(vw), gates, glu


def compress(rows, pe, w1, b1, w2):
    B, L, KV, DH = rows.shape
    n_chunk = L // STRIDE_CMP
    c = rows[:, :n_chunk * STRIDE_CMP].reshape(B, n_chunk, STRIDE_CMP, KV, DH)
    w1r = w1.reshape(BLOCK_CMP, DH, CMP_HID)
    lead = jnp.einsum('bnlkd,ldf->bnkf', c + pe[:STRIDE_CMP, None, :], w1r[:STRIDE_CMP])
    trail = jnp.einsum('bnlkd,ldf->bnkf', c + pe[STRIDE_CMP:, None, :], w1r[STRIDE_CMP:])
    hid = jax.nn.gelu(lead[:, :-1] + trail[:, 1:] + b1)
    return jnp.einsum('bnkf,fd->bnkd', hid, w2)


def to_blocks(rows, n_sel):
    B, L, KV, DH = rows.shape
    rows = jnp.pad(rows, ((0, 0), (0, n_sel * SEL_BLOCK - L), (0, 0), (0, 0)))
    return rows.reshape(B, n_sel, SEL_BLOCK, KV, DH).transpose(0, 3, 1, 2, 4)


def compressed_attention(q, q_pos, kcc, vcc):
    end = jnp.arange(kcc.shape[1]) * STRIDE_CMP + (BLOCK_CMP - 1)
    mask = end[None, :] <= q_pos[:, None]
    s = jnp.einsum('bkgtd,bnkd->bkgtn', q, kcc) * ATTN_SCALE
    p = masked_softmax(s, mask)
    return jnp.einsum('bkgtn,bnkd->bkgtd', p.astype(vcc.dtype), vcc), p


def select_blocks(p_cmp, q_pos, n_sel):
    i = jnp.arange(p_cmp.shape[-1])[:, None] * STRIDE_CMP
    j = jnp.arange(n_sel)[None, :] * SEL_BLOCK
    overlap = jnp.clip(jnp.minimum(i + BLOCK_CMP, j + SEL_BLOCK) - jnp.maximum(i, j), 0, None).astype(jnp.float32) / BLOCK_CMP
    imp = jnp.einsum('bkgtn,ns->bkts', p_cmp, overlap)
    blk = jnp.arange(n_sel)[None, :]
    cur = (q_pos // SEL_BLOCK)[:, None]
    valid = blk * SEL_BLOCK <= q_pos[:, None]
    forced = (blk == 0) | (blk == cur) | (blk == cur - 1)
    score = jnp.where(valid, jnp.where(forced, jnp.inf, imp), -jnp.inf)
    _, idx = lax.top_k(score, min(TOP_N, n_sel))
    return idx


def selected_attention(q, q_pos, idx, kb, vb):
    gather = jax.vmap(jax.vmap(lambda blocks, sel: blocks[sel]))
    kg, vg = gather(kb, idx), gather(vb, idx)
    B, KV, Tq, K, SB, DH = kg.shape
    kg, vg = kg.reshape(B, KV, Tq, K * SB, DH), vg.reshape(B, KV, Tq, K * SB, DH)
    k_pos = (idx[..., None] * SEL_BLOCK + jnp.arange(SEL_BLOCK)).reshape(B, KV, Tq, K * SB)
    mask = (k_pos <= q_pos[:, None])[:, :, None]
    s = jnp.einsum('bkgtd,bktsd->bkgts', q, kg) * ATTN_SCALE
    p = masked_softmax(s, mask)
    return jnp.einsum('bkgts,bktsd->bkgtd', p.astype(vg.dtype), vg)


def cmp_and_select(q, q_pos, kcc, vcc, kb, vb):
    o_c, p_c = compressed_attention(q, q_pos, kcc, vcc)
    idx = select_blocks(p_c, q_pos, kb.shape[2])
    return o_c, selected_attention(q, q_pos, idx, kb, vb)


def window_attention(q, q_pos, k, v, k_pos):
    kp, qp = k_pos[:, None, :], q_pos[:, :, None]
    mask = (kp <= qp) & (qp - kp < WINDOW) & (kp >= 0)
    s = jnp.einsum('bkgntd,bnskd->bkgnts', q, k) * ATTN_SCALE
    p = masked_softmax(s, mask)
    return jnp.einsum('bkgnts,bnskd->bkgntd', p.astype(v.dtype), v)


def banded_window_attention(q, kw, vw):
    B, KV, G, T, DH = q.shape
    nb, nprev = T // Q_BLOCK, WINDOW // Q_BLOCK

    def band(rows):
        padded = jnp.concatenate([jnp.zeros((B, WINDOW, KV, DH), rows.dtype), rows], axis=1)
        blocks = padded.reshape(B, nb + nprev, Q_BLOCK, KV, DH)
        return jnp.concatenate([blocks[:, i:i + nb] for i in range(nprev + 1)], axis=2)

    k_pos = jnp.arange(nb)[:, None] * Q_BLOCK - WINDOW + jnp.arange((nprev + 1) * Q_BLOCK)[None, :]
    q_pos = jnp.arange(T).reshape(nb, Q_BLOCK)
    o = window_attention(q.reshape(B, KV, G, nb, Q_BLOCK, DH), q_pos, band(kw), band(vw), k_pos)
    return o.reshape(B, KV, G, T, DH)


def conformer_conv(glu_in, prev, conv_w, conv_b, ln_g, ln_b):
    a, gate = jnp.split(glu_in, 2, axis=-1)
    u = a * jax.nn.sigmoid(gate)
    ext = jnp.concatenate([prev, u], axis=1)
    y = jax.nn.silu(layernorm(causal_dwconv(ext, conv_w, conv_b), ln_g, ln_b))
    return y, ext[:, ext.shape[1] - (CONV_W - 1):]


def memory_kv(mem, mem_norm_g, w_mk, w_mv):
    B = mem.shape[0]
    m = rmsnorm(mem, mem_norm_g)
    return (m @ w_mk).reshape(B, N_MEM, MEM_HEADS, MEM_HEAD_DIM), (m @ w_mv).reshape(B, N_MEM, MEM_HEADS, MEM_HEAD_DIM)


def memory_attention(h, mk, mv, w_mq, w_mo):
    B, T, _ = h.shape
    q = (h @ w_mq).reshape(B, T, MEM_HEADS, MEM_HEAD_DIM)
    s = jnp.einsum('bthd,bmhd->bhtm', q, mk).astype(jnp.float32) * MEM_SCALE
    p = jax.nn.softmax(s, axis=-1).astype(mv.dtype)
    return jnp.einsum('bhtm,bmhd->bthd', p, mv).reshape(B, T, MEM_W) @ w_mo


def conv_ffn(h, prev, w_gate, w_up, cw, cb, w_down):
    g = h @ w_gate
    ext = jnp.concatenate([prev, g], axis=1)
    a = jax.nn.silu(causal_dwconv(ext, cw, cb)) * (h @ w_up)
    return a @ w_down, ext[:, ext.shape[1] - (FFN_CONV_W - 1):]


def decoder_layer(x, q_pos, lw, mem=None, past=None, page_table=None):
    (norm_mix_g, w_in, b_gate, cmp_k_pe, cmp_k_w1, cmp_k_b1, cmp_k_w2,
     cmp_v_pe, cmp_v_w1, cmp_v_b1, cmp_v_w2, conv_w, conv_b, conv_ln_g, conv_ln_b,
     grp_norm_attn_g, grp_norm_conv_g, w_out, norm_mem_g, mem_norm_g, w_mq, w_mk, w_mv, w_mo,
     norm_ffn_g, w_ffn_gate, w_ffn_up, ffn_conv_w, ffn_conv_b, w_ffn_down) = lw
    B, T, _ = x.shape
    h = rmsnorm(x, norm_mix_g)
    q, kc, vc, ks, vs, kw, vw, gates, glu = project(h, w_in, b_gate)

    if past is None:
        kc_all, vc_all, ks_all, vs_all = kc, vc, ks, vs
        conv_prev = jnp.zeros((B, CONV_W - 1, CONV_CH), x.dtype)
        ffn_prev = jnp.zeros((B, FFN_CONV_W - 1, D_FF), x.dtype)
        mk, mv = memory_kv(mem, mem_norm_g, w_mk, w_mv)
    else:
        (pool_kc, pool_vc, pool_ks, pool_vs, buf_kw, buf_vw, conv_prev, ffn_prev, mk, mv) = past

        def with_past(pool, new):
            old = pool[page_table].reshape(B, -1, N_KV_HEADS, HEAD_DIM)
            return jnp.concatenate([old, new], axis=1)

        kc_all, vc_all = with_past(pool_kc, kc), with_past(pool_vc, vc)
        ks_all, vs_all = with_past(pool_ks, ks), with_past(pool_vs, vs)

    L = kc_all.shape[1]
    kcc = compress(kc_all, cmp_k_pe, cmp_k_w1, cmp_k_b1, cmp_k_w2)
    vcc = compress(vc_all, cmp_v_pe, cmp_v_w1, cmp_v_b1, cmp_v_w2)
    n_sel = -(-L // SEL_BLOCK)
    kb, vb = to_blocks(ks_all, n_sel), to_blocks(vs_all, n_sel)

    if past is None:
        nb = T // Q_BLOCK
        qb = jnp.moveaxis(q.reshape(B, N_KV_HEADS, GQA, nb, Q_BLOCK, HEAD_DIM), 3, 0)
        o_c, o_s = lax.map(lambda blk: cmp_and_select(blk[0], blk[1], kcc, vcc, kb, vb),
                           (qb, q_pos.reshape(nb, Q_BLOCK)))
        unblock = lambda o: jnp.moveaxis(o, 0, 3).reshape(B, N_KV_HEADS, GQA, T, HEAD_DIM)
        o_c, o_s = unblock(o_c), unblock(o_s)
        o_w = banded_window_attention(q, kw, vw)
        w_p = min(WINDOW, T)
        new_kw, new_vw = kw[:, T - w_p:], vw[:, T - w_p:]
    else:
        o_c, o_s = cmp_and_select(q, q_pos, kcc, vcc, kb, vb)
        n_buf = buf_kw.shape[1]
        k_all = jnp.concatenate([buf_kw, kw], axis=1)
        v_all = jnp.concatenate([buf_vw, vw], axis=1)
        k_pos = q_pos[0] - n_buf + jnp.arange(n_buf + T)
        o_w = window_attention(q[:, :, :, None], q_pos[None], k_all[:, None], v_all[:, None], k_pos[None])[:, :, :, 0]
        new_kw, new_vw = k_all[:, T:], v_all[:, T:]

    o_attn = gates[..., 0:1] * o_c + gates[..., 1:2] * o_s + gates[..., 2:3] * o_w
    o_attn = o_attn.transpose(0, 3, 1, 2, 4).reshape(B, T, ATTN_W)
    o_conv, new_conv = conformer_conv(glu, conv_prev, conv_w, conv_b, conv_ln_g, conv_ln_b)
    mixed = jnp.concatenate([rmsnorm(o_attn, grp_norm_attn_g), rmsnorm(o_conv, grp_norm_conv_g)], axis=-1)
    x = x + mixed @ w_out
    x = x + memory_attention(rmsnorm(x, norm_mem_g), mk, mv, w_mq, w_mo)
    f, new_ffn = conv_ffn(rmsnorm(x, norm_ffn_g), ffn_prev, w_ffn_gate, w_ffn_up, ffn_conv_w, ffn_conv_b, w_ffn_down)
    x = x + f
    if past is None:
        return x, (kc, vc, ks, vs, new_kw, new_vw, new_conv, new_ffn, mk, mv)
    return x, (kc, vc, ks, vs, new_kw, new_vw, new_conv, new_ffn)


def setup_inputs(seed: int = 0) -> dict:
    key = jax.random.key(seed)
    keys = jax.random.split(key, 80)
    kit = iter(range(80))

    def nrm(shape, scale=1.0):
        return jax.random.normal(keys[next(kit)], shape, jnp.float32) * scale

    def gain(shape):
        return 1.0 + nrm(shape, 0.02)

    n_pages = PAST_LEN // PAGE_SIZE
    n_phys = (DEC_BATCH * n_pages * 5) // 4
    win_buf = min(WINDOW, PAST_LEN)
    page_table = jax.random.permutation(keys[next(kit)], n_phys)[:DEC_BATCH * n_pages]
    page_table = page_table.reshape(DEC_BATCH, n_pages).astype(jnp.int32)
    kv_pool = (DEPTH, n_phys, PAGE_SIZE, N_KV_HEADS, HEAD_DIM)
    win_shape = (DEPTH, DEC_BATCH, win_buf, N_KV_HEADS, HEAD_DIM)
    mem_shape = (DEPTH, DEC_BATCH, N_MEM, MEM_HEADS, MEM_HEAD_DIM)
    return {
        'x_prompt': nrm((BATCH, SEQ, D_MODEL)),
        'x_sample': nrm((DEC_BATCH, DEC_SEQ, D_MODEL)),
        'cache_k_cmp': nrm(kv_pool),
        'cache_v_cmp': nrm(kv_pool),
        'cache_k_sel': nrm(kv_pool),
        'cache_v_sel': nrm(kv_pool),
        'cache_k_win': nrm(win_shape),
        'cache_v_win': nrm(win_shape),
        'state_conv': nrm((DEPTH, DEC_BATCH, CONV_W - 1, CONV_CH), 0.5),
        'state_ffn_conv': nrm((DEPTH, DEC_BATCH, FFN_CONV_W - 1, D_FF)),
        'cache_mem_k': nrm(mem_shape),
        'cache_mem_v': nrm(mem_shape),
        'page_table': page_table,
        'mem_prompt': nrm((BATCH, N_MEM, D_MODEL)),
        'norm_mix_g': gain((DEPTH, D_MODEL)),
        'w_in': nrm((DEPTH, D_MODEL, IN_COLS), D_MODEL ** -0.5),
        'b_gate': nrm((DEPTH, GATE_COLS), 0.1),
        'cmp_k_pe': nrm((DEPTH, BLOCK_CMP, HEAD_DIM), 0.5),
        'cmp_k_w1': nrm((DEPTH, BLOCK_CMP * HEAD_DIM, CMP_HID), (BLOCK_CMP * HEAD_DIM) ** -0.5),
        'cmp_k_b1': nrm((DEPTH, CMP_HID), 0.02),
        'cmp_k_w2': nrm((DEPTH, CMP_HID, HEAD_DIM), CMP_HID ** -0.5),
        'cmp_v_pe': nrm((DEPTH, BLOCK_CMP, HEAD_DIM), 0.5),
        'cmp_v_w1': nrm((DEPTH, BLOCK_CMP * HEAD_DIM, CMP_HID), (BLOCK_CMP * HEAD_DIM) ** -0.5),
        'cmp_v_b1': nrm((DEPTH, CMP_HID), 0.02),
        'cmp_v_w2': nrm((DEPTH, CMP_HID, HEAD_DIM), CMP_HID ** -0.5),
        'conv_w': nrm((DEPTH, CONV_W, CONV_CH), CONV_W ** -0.5),
        'conv_b': nrm((DEPTH, CONV_CH), 0.02),
        'conv_ln_g': gain((DEPTH, CONV_CH)),
        'conv_ln_b': nrm((DEPTH, CONV_CH), 0.02),
        'grp_norm_attn_g': gain((DEPTH, ATTN_W)),
        'grp_norm_conv_g': gain((DEPTH, CONV_CH)),
        'w_out': nrm((DEPTH, D_MODEL, D_MODEL), D_MODEL ** -0.5),
        'norm_mem_g': gain((DEPTH, D_MODEL)),
        'mem_norm_g': gain((DEPTH, D_MODEL)),
        'w_mq': nrm((DEPTH, D_MODEL, MEM_W), D_MODEL ** -0.5),
        'w_mk': nrm((DEPTH, D_MODEL, MEM_W), D_MODEL ** -0.5),
        'w_mv': nrm((DEPTH, D_MODEL, MEM_W), D_MODEL ** -0.5),
        'w_mo': nrm((DEPTH, MEM_W, D_MODEL), MEM_W ** -0.5),
        'norm_ffn_g': gain((DEPTH, D_MODEL)),
        'w_ffn_gate': nrm((DEPTH, D_MODEL, D_FF), D_MODEL ** -0.5),
        'w_ffn_up': nrm((DEPTH, D_MODEL, D_FF), D_MODEL ** -0.5),
        'ffn_conv_w': nrm((DEPTH, FFN_CONV_W, D_FF), FFN_CONV_W ** -0.5),
        'ffn_conv_b': nrm((DEPTH, D_FF), 0.02),
        'w_ffn_down': nrm((DEPTH, D_FF, D_MODEL), D_FF ** -0.5),
        'norm_final_g': gain((D_MODEL,)),
    }


def reference(x_prompt, x_sample, cache_k_cmp, cache_v_cmp, cache_k_sel, cache_v_sel, cache_k_win, cache_v_win,
              state_conv, state_ffn_conv, cache_mem_k, cache_mem_v, page_table, mem_prompt,
              norm_mix_g, w_in, b_gate, cmp_k_pe, cmp_k_w1, cmp_k_b1, cmp_k_w2, cmp_v_pe, cmp_v_w1, cmp_v_b1, cmp_v_w2,
              conv_w, conv_b, conv_ln_g, conv_ln_b, grp_norm_attn_g, grp_norm_conv_g, w_out,
              norm_mem_g, mem_norm_g, w_mq, w_mk, w_mv, w_mo,
              norm_ffn_g, w_ffn_gate, w_ffn_up, ffn_conv_w, ffn_conv_b, w_ffn_down, norm_final_g):
    weights = (norm_mix_g, w_in, b_gate, cmp_k_pe, cmp_k_w1, cmp_k_b1, cmp_k_w2,
               cmp_v_pe, cmp_v_w1, cmp_v_b1, cmp_v_w2, conv_w, conv_b, conv_ln_g, conv_ln_b,
               grp_norm_attn_g, grp_norm_conv_g, w_out, norm_mem_g, mem_norm_g, w_mq, w_mk, w_mv, w_mo,
               norm_ffn_g, w_ffn_gate, w_ffn_up, ffn_conv_w, ffn_conv_b, w_ffn_down)
    past_len = page_table.shape[1] * PAGE_SIZE
    pos_p = jnp.arange(x_prompt.shape[1])
    pos_s = past_len + jnp.arange(x_sample.shape[1])
    xp, xs = x_prompt, x_sample
    st_p, st_s = [], []
    for l in range(DEPTH):
        lw = tuple(w[l] for w in weights)
        xp, sp = decoder_layer(xp, pos_p, lw, mem=mem_prompt)
        past = (cache_k_cmp[l], cache_v_cmp[l], cache_k_sel[l], cache_v_sel[l], cache_k_win[l], cache_v_win[l],
                state_conv[l], state_ffn_conv[l], cache_mem_k[l], cache_mem_v[l])
        xs, ss = decoder_layer(xs, pos_s, lw, past=past, page_table=page_table)
        st_p.append(sp)
        st_s.append(ss)
    stk = lambda sts, i: jnp.stack([s[i] for s in sts])
    y_prompt = rmsnorm(xp, norm_final_g)
    y_sample = rmsnorm(xs, norm_final_g)
    return (y_prompt, y_sample,
            stk(st_p, 0), stk(st_p, 1), stk(st_p, 2), stk(st_p, 3), stk(st_p, 4), stk(st_p, 5),
            stk(st_p, 6), stk(st_p, 7), stk(st_p, 8), stk(st_p, 9),
            stk(st_s, 0), stk(st_s, 1), stk(st_s, 2), stk(st_s, 3), stk(st_s, 4), stk(st_s, 5),
            stk(st_s, 6), stk(st_s, 7))
```

```python
import functools

import jax
import jax.numpy as jnp
from jax import lax
from jax.experimental import pallas as pl
from jax.experimental.pallas import tpu as pltpu

F32 = jnp.float32
BF16 = jnp.bfloat16
I32 = jnp.int32

D_MODEL = 2048
N_KV = 2
GQA = 4
DH = 128
ATTN_W = N_KV * GQA * DH
KV_W = N_KV * DH
N_GATE = 3
BLOCK_CMP = 32
STRIDE_CMP = 16
CMP_HID = 256
SEL_BLOCK = 64
TOP_N = 16
WINDOW = 512
CONV_CH = D_MODEL - ATTN_W
CONV_W = 31
D_FF = 5632
FFN_CONV_W = 3
MEM_HEADS = 4
MEM_DH = 128
MEM_W = MEM_HEADS * MEM_DH
ATTN_SCALE = DH ** -0.5
MEM_SCALE = MEM_DH ** -0.5
EPS = 1e-6
NEG = -1e30
BIG = 1e30
LANES = 128
VMEM_LIMIT = 56 * 1024 * 1024


def _dot(a, b):
    return jnp.dot(a, b, preferred_element_type=F32)


def _dot_nt(a, b):
    return lax.dot_general(a, b, (((1,), (1,)), ((), ())), preferred_element_type=F32)


def _rms(x):
    return x * lax.rsqrt(jnp.mean(x * x, axis=-1, keepdims=True) + EPS)


def _const_spec(shape):
    return pl.BlockSpec(shape, lambda *_: (0,) * len(shape), pipeline_mode=pl.Buffered(1))


def _params(*sem):
    return pltpu.CompilerParams(dimension_semantics=sem, vmem_limit_bytes=VMEM_LIMIT)


def _in_proj_body(x_ref, g_ref, wq_ref, wkv_ref, wg_ref, bg_ref, wglu_ref,
                  q_ref, kc_ref, vc_ref, ks_ref, vs_ref, kw_ref, vw_ref,
                  ksb_ref, vsb_ref, kwb_ref, vwb_ref, gates_ref, u_ref):
    hb = (_rms(x_ref[...]) * g_ref[...]).astype(BF16)
    half = ATTN_W // 2
    for c in range(2):
        q_ref[:, c * half:(c + 1) * half] = _dot(hb, wq_ref[:, c * half:(c + 1) * half]).astype(BF16)
    f32_outs = (kc_ref, vc_ref, ks_ref, vs_ref, kw_ref, vw_ref)
    bf_outs = (None, None, ksb_ref, vsb_ref, kwb_ref, vwb_ref)
    for c in range(6):
        r = _dot(hb, wkv_ref[:, c * KV_W:(c + 1) * KV_W])
        f32_outs[c][...] = r
        if bf_outs[c] is not None:
            bf_outs[c][...] = r.astype(BF16)
    gates_ref[...] = jax.nn.sigmoid(_dot(hb, wg_ref[...]) + bg_ref[...])
    cw = 256
    for c in range(CONV_CH // cw):
        a = _dot(hb, wglu_ref[:, c * cw:(c + 1) * cw])
        gt = _dot(hb, wglu_ref[:, CONV_CH + c * cw:CONV_CH + (c + 1) * cw])
        u_ref[:, c * cw:(c + 1) * cw] = a * jax.nn.sigmoid(gt)


def _in_proj(x, g, wq, wkv, wg, bg, wglu, tm):
    m = x.shape[0]
    row = lambda w: pl.BlockSpec((tm, w), lambda i: (i, 0))
    out_shape = ([jax.ShapeDtypeStruct((m, ATTN_W), BF16)]
                 + [jax.ShapeDtypeStruct((m, KV_W), F32)] * 6
                 + [jax.ShapeDtypeStruct((m, KV_W), BF16)] * 4
                 + [jax.ShapeDtypeStruct((m, N_KV * LANES), F32),
                    jax.ShapeDtypeStruct((m, CONV_CH), F32)])
    out_specs = ([row(ATTN_W)] + [row(KV_W)] * 10 + [row(N_KV * LANES), row(CONV_CH)])
    return pl.pallas_call(
        _in_proj_body,
        grid=(m // tm,),
        in_specs=[row(D_MODEL), _const_spec(g.shape), _const_spec(wq.shape), _const_spec(wkv.shape),
                  _const_spec(wg.shape), _const_spec(bg.shape), _const_spec(wglu.shape)],
        out_specs=out_specs,
        out_shape=out_shape,
        compiler_params=_params("arbitrary"),
        name="in_proj",
    )(x, g, wq, wkv, wg, bg, wglu)


def _compress_rows(get_lanes, n, pe_ref, w1l_ref, w1t_ref, b1_ref, w2_ref):
    xk = jnp.concatenate([get_lanes(l) for l in range(STRIDE_CMP)], axis=1)
    lead = _dot((xk + pe_ref[0:1, :]).astype(BF16), w1l_ref[...])
    trail = _dot((xk + pe_ref[1:2, :]).astype(BF16), w1t_ref[...])
    trail_next = pltpu.roll(trail, n - 1, axis=0)
    hid = jax.nn.gelu(lead + trail_next + b1_ref[...])
    out = _dot(hid.astype(BF16), w2_ref[...])
    rows = lax.broadcasted_iota(I32, (n, 1), 0)
    return jnp.where(rows < n - 1, out, 0.0)


def _compress_prompt_body(x_ref, pe_ref, w1l_ref, w1t_ref, b1_ref, w2_ref, o_ref):
    n = x_ref.shape[1]
    for k in range(N_KV):
        get = lambda l, k=k: x_ref[0, :, l * KV_W + k * DH:l * KV_W + (k + 1) * DH]
        o_ref[0, k] = _compress_rows(get, n, pe_ref, w1l_ref, w1t_ref, b1_ref, w2_ref)


def _compress_weights(pe, w1, b1, w2):
    half = STRIDE_CMP * DH
    pe2 = pe.reshape(2, half)
    return pe2, w1[:half].astype(BF16), w1[half:].astype(BF16), b1.reshape(1, CMP_HID), w2.astype(BF16)


def _compress_prompt(rows, cw, batch):
    n = rows.shape[0] // batch // STRIDE_CMP
    x = rows.reshape(batch, n, STRIDE_CMP * KV_W)
    return pl.pallas_call(
        _compress_prompt_body,
        grid=(batch,),
        in_specs=[pl.BlockSpec((1, n, STRIDE_CMP * KV_W), lambda b: (b, 0, 0))] + [_const_spec(w.shape) for w in cw],
        out_specs=pl.BlockSpec((1, N_KV, n, DH), lambda b: (b, 0, 0, 0)),
        out_shape=jax.ShapeDtypeStruct((batch, N_KV, n, DH), F32),
        compiler_params=_params("arbitrary"),
        name="compress_prompt",
    )(x, *cw)


def _overlap_matrix(n_cmp, n_blk=LANES):
    i = lax.broadcasted_iota(I32, (n_cmp, n_blk), 0) * STRIDE_CMP
    j = lax.broadcasted_iota(I32, (n_cmp, n_blk), 1) * SEL_BLOCK
    ov = jnp.maximum(jnp.minimum(i + BLOCK_CMP, j + SEL_BLOCK) - jnp.maximum(i, j), 0)
    return (ov.astype(F32) * (1.0 / BLOCK_CMP)).astype(BF16)


def _importance(p_sum, ov):
    hi = p_sum.astype(BF16)
    r1 = p_sum - hi.astype(F32)
    mid = r1.astype(BF16)
    lo = (r1 - mid.astype(F32)).astype(BF16)
    return _dot(hi, ov) + _dot(mid, ov) + _dot(lo, ov)


def _cmp_attention(q_heads, kcc, vcc, cmask):
    p_sum = jnp.zeros(cmask.shape, F32)
    outs = []
    for qg in q_heads:
        s = jnp.where(cmask, _dot_nt(qg, kcc) * ATTN_SCALE, NEG)
        m = jnp.max(s, axis=-1, keepdims=True)
        e = jnp.where(cmask, jnp.exp(s - m), 0.0)
        p = e / jnp.maximum(jnp.sum(e, axis=-1, keepdims=True), 1e-30)
        p_sum = p_sum + p
        outs.append(_dot(p.astype(BF16), vcc))
    return outs, p_sum


def _softmax_av(s, v):
    p = jnp.exp(s - jnp.max(s, axis=-1, keepdims=True))
    return _dot(p.astype(BF16), v) / jnp.sum(p, axis=-1, keepdims=True)


def _nsa_prompt_body(q_ref, kcc_ref, vcc_ref, ks_ref, vs_ref, kw_ref, vw_ref, gates_ref, o_ref,
                     m_scr, l_scr, acc_scr, *, tq, tk, n_sel):
    i = pl.program_id(2)
    t0 = i * tq
    q_pos = t0 + lax.broadcasted_iota(I32, (tq, 1), 0)
    q_heads = [q_ref[:, g * DH:(g + 1) * DH] for g in range(GQA)]

    n_cmp = kcc_ref.shape[2]
    kcc = kcc_ref[0, 0].astype(BF16)
    vcc = vcc_ref[0, 0].astype(BF16)
    cmp_end = lax.broadcasted_iota(I32, (1, n_cmp), 1) * STRIDE_CMP + (BLOCK_CMP - 1)
    o_cmp, p_sum = _cmp_attention(q_heads, kcc, vcc, cmp_end <= q_pos)

    imp = _importance(p_sum, _overlap_matrix(n_cmp))
    blk = lax.broadcasted_iota(I32, (1, LANES), 1)
    cur = q_pos // SEL_BLOCK
    valid = blk * SEL_BLOCK <= q_pos
    forced = (blk == 0) | (blk == cur) | (blk == cur - 1)
    score = jnp.where(valid, jnp.where(forced, BIG, imp), -BIG)
    s_t = score.T[:n_sel]
    jrow = lax.broadcasted_iota(I32, (n_sel, tq), 0)
    rank = jnp.zeros((n_sel, tq), F32)
    for ib in range(n_sel):
        row = s_t[ib:ib + 1, :]
        beats = (row > s_t) | ((row == s_t) & (jrow > ib))
        rank = rank + jnp.where(beats, 1.0, 0.0)
    sel_t = (rank < TOP_N).astype(F32)
    if n_sel < LANES:
        sel_t = jnp.concatenate([sel_t, jnp.zeros((LANES - n_sel, tq), F32)], axis=0)
    sel = sel_t.T.astype(BF16)

    m_scr[...] = jnp.full(m_scr.shape, NEG, F32)
    l_scr[...] = jnp.zeros(l_scr.shape, F32)
    acc_scr[...] = jnp.zeros(acc_scr.shape, F32)
    col = lax.broadcasted_iota(I32, (1, tk), 1)
    blk_rows = lax.broadcasted_iota(I32, (LANES, 1), 0)

    def key_tile(j, carry):
        k0 = pl.multiple_of(j * tk, tk)
        kt = ks_ref[pl.ds(k0, tk), :]
        vt = vs_ref[pl.ds(k0, tk), :]
        k_pos = k0 + col
        expand = (blk_rows == k_pos // SEL_BLOCK).astype(BF16)
        vis = (_dot(sel, expand) > 0.5) & (k_pos <= q_pos)
        bias = jnp.where(vis, 0.0, NEG)
        for g in range(GQA):
            s = _dot_nt(q_heads[g], kt) * ATTN_SCALE + bias
            m_old = m_scr[g]
            m_new = jnp.maximum(m_old, jnp.max(s, axis=-1, keepdims=True))
            alpha = jnp.exp(m_old - m_new)
            p = jnp.exp(s - m_new)
            l_scr[g] = alpha * l_scr[g] + jnp.sum(p, axis=-1, keepdims=True)
            acc_scr[g] = alpha * acc_scr[g] + _dot(p.astype(BF16), vt)
            m_scr[g] = m_new
        return carry

    lax.fori_loop(0, (t0 + tq + tk - 1) // tk, key_tile, 0)

    span = WINDOW + tq
    w0 = pl.multiple_of(jnp.maximum(t0 - WINDOW, 0), tq)
    kwt = kw_ref[pl.ds(w0, span), :]
    vwt = vw_ref[pl.ds(w0, span), :]
    kw_pos = w0 + lax.broadcasted_iota(I32, (1, span), 1)
    wbias = jnp.where((kw_pos <= q_pos) & (q_pos - kw_pos < WINDOW), 0.0, NEG)

    gates = gates_ref[...]
    for g in range(GQA):
        o_sel = acc_scr[g] / l_scr[g]
        o_win = _softmax_av(_dot_nt(q_heads[g], kwt) * ATTN_SCALE + wbias, vwt)
        c = g * N_GATE
        o_ref[:, g * DH:(g + 1) * DH] = (gates[:, c:c + 1] * o_cmp[g] + gates[:, c + 1:c + 2] * o_sel
                                         + gates[:, c + 2:c + 3] * o_win)


def _nsa_prompt(q, kcc, vcc, ksb, vsb, kwb, vwb, gates, batch, tq=128, tk=512):
    m = q.shape[0]
    t = m // batch
    tk = min(tk, t)
    nt = t // tq
    n_cmp = kcc.shape[2]
    n_sel = max(t // SEL_BLOCK, 8)
    assert n_sel <= LANES and t % tk == 0 and tk % tq == 0 and t >= WINDOW + tq
    seq = lambda: pl.BlockSpec((t, DH), lambda b, k, i: (b, k))
    cmp_spec = lambda: pl.BlockSpec((1, 1, n_cmp, DH), lambda b, k, i: (b, k, 0, 0))
    return pl.pallas_call(
        functools.partial(_nsa_prompt_body, tq=tq, tk=tk, n_sel=n_sel),
        grid=(batch, N_KV, nt),
        in_specs=[pl.BlockSpec((tq, GQA * DH), lambda b, k, i: (b * nt + i, k)),
                  cmp_spec(), cmp_spec(), seq(), seq(), seq(), seq(),
                  pl.BlockSpec((tq, LANES), lambda b, k, i: (b * nt + i, k))],
        out_specs=pl.BlockSpec((tq, GQA * DH), lambda b, k, i: (b * nt + i, k)),
        out_shape=jax.ShapeDtypeStruct((m, ATTN_W), F32),
        scratch_shapes=[pltpu.VMEM((GQA, tq, 1), F32), pltpu.VMEM((GQA, tq, 1), F32),
                        pltpu.VMEM((GQA, tq, DH), F32)],
        compiler_params=_params("arbitrary", "arbitrary", "arbitrary"),
        name="nsa_prompt",
    )(q, kcc, vcc, ksb, vsb, kwb, vwb, gates)


def _ln_silu_rms(y, lng_ref, lnb_ref, gn_ref):
    mu = jnp.mean(y, axis=-1, keepdims=True)
    var = jnp.mean(jnp.square(y - mu), axis=-1, keepdims=True)
    y = (y - mu) * lax.rsqrt(var + EPS) * lng_ref[...] + lnb_ref[...]
    y = y * jax.nn.sigmoid(y)
    return (_rms(y) * gn_ref[...]).astype(BF16)


def _mix_out_prompt_body(oa_ref, u_ref, halo_ref, x_ref, cw_ref, cb_ref, lng_ref, lnb_ref, ga_ref, gc_ref,
                         woa_ref, woc_ref, o_ref, ext_scr, conv_scr, *, tm, tiles_per_seq, rc):
    i = pl.program_id(0)
    pad = halo_ref.shape[0]
    first = (i % tiles_per_seq) == 0
    ext_scr[0:pad, :] = jnp.where(first, 0.0, halo_ref[...])
    ext_scr[pad:, :] = u_ref[...]
    off = pad - (CONV_W - 1)
    span = rc + pad

    def chunk(r, carry):
        base = pl.multiple_of(r * rc, rc)
        window = ext_scr[pl.ds(base, span), :]
        acc = jnp.zeros((rc, CONV_CH), F32) + cb_ref[...]
        for res in range(8):
            shifted = window if res == 0 else pltpu.roll(window, span - res, axis=0)
            for k in range(CONV_W):
                if (k + off) % 8 == res:
                    a8 = k + off - res
                    acc = acc + cw_ref[k:k + 1, :] * shifted[a8:a8 + rc]
        conv_scr[pl.ds(base, rc), :] = acc
        return carry

    lax.fori_loop(0, tm // rc, chunk, 0)
    conv_n = _ln_silu_rms(conv_scr[...], lng_ref, lnb_ref, gc_ref)
    attn_n = (_rms(oa_ref[...]) * ga_ref[...]).astype(BF16)
    half = D_MODEL // 2
    for c in range(2):
        cs = slice(c * half, (c + 1) * half)
        o_ref[:, cs] = x_ref[:, cs] + _dot(attn_n, woa_ref[:, cs]) + _dot(conv_n, woc_ref[:, cs])


def _mix_out_prompt(o_attn, u, x, mw, batch, tm=256, rc=32):
    m = x.shape[0]
    t = m // batch
    tm = min(tm, t)
    pad = 32
    row = lambda w: pl.BlockSpec((tm, w), lambda i: (i, 0))
    halo = pl.BlockSpec((pad, CONV_CH), lambda i: (jnp.maximum(i * (tm // pad) - 1, 0), 0))
    return pl.pallas_call(
        functools.partial(_mix_out_prompt_body, tm=tm, tiles_per_seq=t // tm, rc=rc),
        grid=(m // tm,),
        in_specs=[row(ATTN_W), row(CONV_CH), halo, row(D_MODEL)] + [_const_spec(w.shape) for w in mw],
        out_specs=row(D_MODEL),
        out_shape=jax.ShapeDtypeStruct((m, D_MODEL), F32),
        scratch_shapes=[pltpu.VMEM((tm + pad, CONV_CH), F32), pltpu.VMEM((tm, CONV_CH), F32)],
        compiler_params=_params("arbitrary"),
        name="mix_out_prompt",
    )(o_attn, u, u, x, *mw)


def _mem_kv_body(mem_ref, g_ref, wk_ref, wv_ref, k_ref, v_ref):
    mb = (_rms(mem_ref[...]) * g_ref[...]).astype(BF16)
    k_ref[...] = _dot(mb, wk_ref[...])
    v_ref[...] = _dot(mb, wv_ref[...])


def _mem_kv(mem, g, wk, wv, tm=256):
    m = mem.shape[0]
    row = lambda w: pl.BlockSpec((tm, w), lambda i: (i, 0))
    return pl.pallas_call(
        _mem_kv_body,
        grid=(m // tm,),
        in_specs=[row(D_MODEL), _const_spec(g.shape), _const_spec(wk.shape), _const_spec(wv.shape)],
        out_specs=[row(MEM_W), row(MEM_W)],
        out_shape=[jax.ShapeDtypeStruct((m, MEM_W), F32)] * 2,
        compiler_params=_params("arbitrary"),
        name="mem_kv",
    )(mem, g, wk, wv)


def _mem_attn_core(q, mk, mv):
    outs = []
    for h in range(MEM_HEADS):
        hs = slice(h * MEM_DH, (h + 1) * MEM_DH)
        s = _dot_nt(q[:, hs].astype(BF16), mk[:, hs]) * MEM_SCALE
        outs.append(_softmax_av(s, mv[:, hs]))
    return jnp.concatenate(outs, axis=1).astype(BF16)


def _mem_attn_prompt_body(x_ref, g_ref, wq_ref, mk_ref, mv_ref, wo_ref, o_ref):
    x = x_ref[...]
    hb = (_rms(x) * g_ref[...]).astype(BF16)
    a = _mem_attn_core(_dot(hb, wq_ref[...]), mk_ref[0].astype(BF16), mv_ref[0].astype(BF16))
    o_ref[...] = x + _dot(a, wo_ref[...])


def _mem_attn_prompt(x, g, wq, mk, mv, wo, batch, tm=256):
    m = x.shape[0]
    tiles_per_seq = m // batch // tm
    n_mem = mk.shape[1]
    row = pl.BlockSpec((tm, D_MODEL), lambda i: (i, 0))
    mem = pl.BlockSpec((1, n_mem, MEM_W), lambda i: (i // tiles_per_seq, 0, 0))
    return pl.pallas_call(
        _mem_attn_prompt_body,
        grid=(m // tm,),
        in_specs=[row, _const_spec(g.shape), _const_spec(wq.shape), mem, mem, _const_spec(wo.shape)],
        out_specs=row,
        out_shape=jax.ShapeDtypeStruct((m, D_MODEL), F32),
        compiler_params=_params("arbitrary"),
        name="mem_attn_prompt",
    )(x, g, wq, mk, mv, wo)


def _ffn_tail(f, nf, x_ref, acc_scr, gf_ref, o_ref):
    @pl.when(f == nf - 1)
    def _():
        o_ref[...] = _rms(x_ref[...] + acc_scr[...]) * gf_ref[...]


def _ffn_prompt_body(x_ref, g_ref, wg_ref, wu_ref, cw_ref, cb_ref, wd_ref, gf_ref, o_ref, st_ref,
                     h_scr, acc_scr, carry_scr, gext_scr, *, tm, tiles_per_seq):
    i = pl.program_id(0)
    f = pl.program_id(1)
    nf = pl.num_programs(1)
    hist = carry_scr.shape[1]

    @pl.when(f == 0)
    def _():
        h_scr[...] = (_rms(x_ref[...]) * g_ref[...]).astype(BF16)
        acc_scr[...] = jnp.zeros(acc_scr.shape, F32)

    hb = h_scr[...]
    gate = _dot(hb, wg_ref[...])
    first = (i % tiles_per_seq) == 0
    gext_scr[0:hist, :] = jnp.where(first, 0.0, carry_scr[f])
    gext_scr[hist:, :] = gate
    tail = gate[tm - hist:, :]
    carry_scr[f] = tail
    st_ref[0] = tail
    conv = (cw_ref[0:1, :] * gext_scr[hist - 2:hist - 2 + tm, :] + cw_ref[1:2, :] * gext_scr[hist - 1:hist - 1 + tm, :]
            + cw_ref[2:3, :] * gate + cb_ref[...])
    a = conv * jax.nn.sigmoid(conv) * _dot(hb, wu_ref[...])
    acc_scr[...] += _dot(a.astype(BF16), wd_ref[...])
    _ffn_tail(f, nf, x_ref, acc_scr, gf_ref, o_ref)


def _ffn_prompt(x, g, wg, wu, cw, cb, wd, gf, batch, tm=512, tf=512):
    m = x.shape[0]
    t = m // batch
    tm = min(tm, t)
    nf = D_FF // tf
    hist = 8
    row = pl.BlockSpec((tm, D_MODEL), lambda i, f: (i, 0))
    col = lambda r: pl.BlockSpec((r, tf), lambda i, f: (0, f))
    return pl.pallas_call(
        functools.partial(_ffn_prompt_body, tm=tm, tiles_per_seq=t // tm),
        grid=(m // tm, nf),
        in_specs=[row, _const_spec(g.shape), col(D_MODEL), col(D_MODEL), col(cw.shape[0]), col(1),
                  pl.BlockSpec((tf, D_MODEL), lambda i, f: (f, 0)), _const_spec(gf.shape)],
        out_specs=[row, pl.BlockSpec((1, hist, tf), lambda i, f: (i, 0, f))],
        out_shape=[jax.ShapeDtypeStruct((m, D_MODEL), F32), jax.ShapeDtypeStruct((m // tm, hist, D_FF), F32)],
        scratch_shapes=[pltpu.VMEM((tm, D_MODEL), BF16), pltpu.VMEM((tm, D_MODEL), F32),
                        pltpu.VMEM((nf, hist, tf), F32), pltpu.VMEM((tm + hist, tf), F32)],
        compiler_params=_params("arbitrary", "arbitrary"),
        name="ffn_prompt",
    )(x, g, wg, wu, cw, cb, wd, gf)


PAGE = 128
CHUNKS_PER_PAGE = PAGE // STRIDE_CMP


def _compress_sample_body(pt_ref, pool_ref, pe_ref, w1l_ref, w1t_ref, b1_ref, w2_ref, o_ref, buf, sem, *, n_pages):
    b = pl.program_id(0)
    slot = b % 2

    def pages(bb, sl, start):
        def one(j, carry):
            cp = pltpu.make_async_copy(
                pool_ref.at[pt_ref[bb, j]],
                buf.at[sl, pl.ds(pl.multiple_of(j * CHUNKS_PER_PAGE, CHUNKS_PER_PAGE), CHUNKS_PER_PAGE)],
                sem.at[sl])
            cp.start() if start else cp.wait()
            return carry
        lax.fori_loop(0, n_pages, one, 0)

    @pl.when(b == 0)
    def _():
        pages(0, 0, True)

    @pl.when(b + 1 < pl.num_programs(0))
    def _():
        pages(b + 1, 1 - slot, True)

    pages(b, slot, False)
    n = n_pages * CHUNKS_PER_PAGE
    for k in range(N_KV):
        get = lambda l, k=k: buf[slot, :, l * KV_W + k * DH:l * KV_W + (k + 1) * DH]
        o_ref[0, k] = _compress_rows(get, n, pe_ref, w1l_ref, w1t_ref, b1_ref, w2_ref)


def _compress_sample(pool, page_table, cw):
    nb, n_pages = page_table.shape
    n = n_pages * CHUNKS_PER_PAGE
    width = STRIDE_CMP * KV_W
    pool2 = pool.reshape(pool.shape[0], CHUNKS_PER_PAGE, width)
    grid_spec = pltpu.PrefetchScalarGridSpec(
        num_scalar_prefetch=1,
        grid=(nb,),
        in_specs=[pl.BlockSpec(memory_space=pl.ANY)] + [_const_spec(w.shape) for w in cw],
        out_specs=pl.BlockSpec((1, N_KV, n, DH), lambda b, pt: (b, 0, 0, 0)),
        scratch_shapes=[pltpu.VMEM((2, n, width), F32), pltpu.SemaphoreType.DMA((2,))],
    )
    return pl.pallas_call(
        functools.partial(_compress_sample_body, n_pages=n_pages),
        grid_spec=grid_spec,
        out_shape=jax.ShapeDtypeStruct((nb, N_KV, n, DH), F32),
        compiler_params=_params("arbitrary"),
        name="compress_sample",
    )(page_table, pool2, *cw)


HEAD_ROWS = 8
TOKEN_ROWS = 16


def _cmp_select_sample_body(q_ref, kcc_ref, vcc_ref, oc_ref, idx_ref, *, past_len, n_blk):
    n_cmp = kcc_ref.shape[2]
    rows = GQA * HEAD_ROWS
    q_pos = past_len + lax.broadcasted_iota(I32, (rows, 1), 0) % HEAD_ROWS
    cmp_end = lax.broadcasted_iota(I32, (1, n_cmp), 1) * STRIDE_CMP + (BLOCK_CMP - 1)
    cmask = cmp_end <= q_pos
    q_pos8 = past_len + lax.broadcasted_iota(I32, (HEAD_ROWS, 1), 0)
    blk = lax.broadcasted_iota(I32, (1, n_blk), 1)
    cur = q_pos8 // SEL_BLOCK
    valid = blk * SEL_BLOCK <= q_pos8
    forced = (blk == 0) | (blk == cur) | (blk == cur - 1)
    lane = lax.broadcasted_iota(I32, (HEAD_ROWS, n_blk), 1).astype(F32)
    out_lane = lax.broadcasted_iota(I32, (HEAD_ROWS, LANES), 1)
    ov = _overlap_matrix(n_cmp, n_blk)
    for k in range(N_KV):
        kcc = kcc_ref[0, k].astype(BF16)
        vcc = vcc_ref[0, k].astype(BF16)
        s = jnp.where(cmask, _dot_nt(q_ref[0, k], kcc) * ATTN_SCALE, NEG)
        e = jnp.where(cmask, jnp.exp(s - jnp.max(s, axis=-1, keepdims=True)), 0.0)
        p = e / jnp.maximum(jnp.sum(e, axis=-1, keepdims=True), 1e-30)
        oc_ref[0, k] = _dot(p.astype(BF16), vcc)
        p_sum = p[0:HEAD_ROWS]
        for g in range(1, GQA):
            p_sum = p_sum + p[g * HEAD_ROWS:(g + 1) * HEAD_ROWS]
        score = jnp.where(valid, jnp.where(forced, BIG, _importance(p_sum, ov)), -BIG)
        picks = jnp.zeros((HEAD_ROWS, LANES), F32)
        for n in range(TOP_N):
            best = jnp.max(score, axis=-1, keepdims=True)
            pick = jnp.min(jnp.where(score == best, lane, float(n_blk)), axis=-1, keepdims=True)
            picks = jnp.where(out_lane == n, pick, picks)
            score = jnp.where(lane == pick, -3e38, score)
        idx_ref[0, k] = picks.astype(I32)


def _cmp_select_sample(q_hm, kcc, vcc, past_len):
    nb = q_hm.shape[0]
    n_cmp = kcc.shape[2]
    n_blk = -(-(past_len // SEL_BLOCK + 1) // LANES) * LANES
    rows = GQA * HEAD_ROWS
    spec = lambda r, w: pl.BlockSpec((1, N_KV, r, w), lambda b: (b, 0, 0, 0))
    return pl.pallas_call(
        functools.partial(_cmp_select_sample_body, past_len=past_len, n_blk=n_blk),
        grid=(nb,),
        in_specs=[spec(rows, DH), spec(n_cmp, DH), spec(n_cmp, DH)],
        out_specs=[spec(rows, DH), spec(HEAD_ROWS, LANES)],
        out_shape=[jax.ShapeDtypeStruct((nb, N_KV, rows, DH), F32),
                   jax.ShapeDtypeStruct((nb, N_KV, HEAD_ROWS, LANES), I32)],
        compiler_params=_params("arbitrary"),
        name="cmp_select_sample",
    )(q_hm, kcc, vcc)


def _attn_sample_body(pt_ref, idx_s_ref, q_ref, idx_v_ref, ksn_ref, vsn_ref, kwn_ref, vwn_ref, kwin_ref, vwin_ref,
                      oc_ref, gates_ref, kpool_ref, vpool_ref, o_ref, kbuf, vbuf, sem, *, dec_seq, n_pool_blk):
    b = pl.program_id(0)
    slot = b % 2
    n_copy = N_KV * dec_seq * TOP_N
    blocks_per_page = PAGE // SEL_BLOCK

    def gather(bb, sl, start):
        def one(it, carry):
            blk = idx_s_ref[bb, it]
            src = jnp.where(blk < n_pool_blk, blk, 0)
            page = pt_ref[bb, src // blocks_per_page]
            r0 = pl.multiple_of((src % blocks_per_page) * SEL_BLOCK, SEL_BLOCK)
            kt = it // TOP_N
            dst = pl.ds(pl.multiple_of((it % TOP_N) * SEL_BLOCK, SEL_BLOCK), SEL_BLOCK)
            k = kt // dec_seq
            ck = pltpu.make_async_copy(kpool_ref.at[page, pl.ds(r0, SEL_BLOCK), k], kbuf.at[sl, kt, dst], sem.at[sl, 0])
            cv = pltpu.make_async_copy(vpool_ref.at[page, pl.ds(r0, SEL_BLOCK), k], vbuf.at[sl, kt, dst], sem.at[sl, 1])
            if start:
                ck.start()
                cv.start()
            else:
                ck.wait()
                cv.wait()
            return carry
        lax.fori_loop(0, n_copy, one, 0)

    @pl.when(b == 0)
    def _():
        gather(0, 0, True)

    @pl.when(b + 1 < pl.num_programs(0))
    def _():
        gather(b + 1, 1 - slot, True)

    gather(b, slot, False)

    n_keys = TOP_N * SEL_BLOCK
    rows = dec_seq * TOKEN_ROWS
    key_slot = lax.broadcasted_iota(I32, (LANES, n_keys), 1) // SEL_BLOCK
    expand = (lax.broadcasted_iota(I32, (LANES, n_keys), 0) == key_slot).astype(BF16)
    new_col = lax.broadcasted_iota(I32, (1, TOKEN_ROWS), 1)
    t_row = lax.broadcasted_iota(I32, (rows, 1), 0) // TOKEN_ROWS
    n_win = kwin_ref.shape[1]
    win_old_vis = lax.broadcasted_iota(I32, (1, n_win), 1) > t_row
    win_new_vis = new_col <= t_row
    for k in range(N_KV):
        qk = q_ref[0, k]
        pool_ok = _dot((idx_v_ref[0, k] < n_pool_blk).astype(BF16), expand)
        ksn, vsn = ksn_ref[0, k], vsn_ref[0, k]
        hs = slice(k * DH, (k + 1) * DH)
        s_old = jnp.where(win_old_vis, _dot_nt(qk, kwin_ref[0, :, hs].astype(BF16)) * ATTN_SCALE, NEG)
        s_new = jnp.where(win_new_vis, _dot_nt(qk, kwn_ref[0, k]) * ATTN_SCALE, NEG)
        m = jnp.maximum(jnp.max(s_old, axis=-1, keepdims=True), jnp.max(s_new, axis=-1, keepdims=True))
        p_old, p_new = jnp.exp(s_old - m), jnp.exp(s_new - m)
        l = jnp.sum(p_old, axis=-1, keepdims=True) + jnp.sum(p_new, axis=-1, keepdims=True)
        o_win = (_dot(p_old.astype(BF16), vwin_ref[0, :, hs].astype(BF16)) + _dot(p_new.astype(BF16), vwn_ref[0, k])) / l
        gates = gates_ref[0, k]
        for t in range(dec_seq):
            ts = slice(t * TOKEN_ROWS, (t + 1) * TOKEN_ROWS)
            kt = k * dec_seq + t
            qt = qk[ts]
            s_pool = jnp.where(pool_ok[t:t + 1, :] > 0.5,
                               _dot_nt(qt, kbuf[slot, kt].astype(BF16)) * ATTN_SCALE, NEG)
            s_cur = jnp.where(new_col <= t, _dot_nt(qt, ksn) * ATTN_SCALE, NEG)
            m = jnp.maximum(jnp.max(s_pool, axis=-1, keepdims=True), jnp.max(s_cur, axis=-1, keepdims=True))
            p_pool, p_cur = jnp.exp(s_pool - m), jnp.exp(s_cur - m)
            l = jnp.sum(p_pool, axis=-1, keepdims=True) + jnp.sum(p_cur, axis=-1, keepdims=True)
            o_sel = (_dot(p_pool.astype(BF16), vbuf[slot, kt].astype(BF16)) + _dot(p_cur.astype(BF16), vsn)) / l
            gt = gates[ts]
            o_ref[0, k, ts, :] = (gt[:, 0:1] * oc_ref[0, k, ts, :] + gt[:, 1:2] * o_sel + gt[:, 2:3] * o_win[ts])


def _attn_sample(page_table, idx_flat, q_tm, idx_pad, ksn, vsn, kwn, vwn, k_win, v_win, oc_tm, gates_tm,
                 k_pool, v_pool, dec_seq, past_len):
    nb = q_tm.shape[0]
    rows = dec_seq * TOKEN_ROWS
    n_win = k_win.shape[1]
    spec = lambda r, w: pl.BlockSpec((1, N_KV, r, w), lambda b, *_: (b, 0, 0, 0))
    win = pl.BlockSpec((1, n_win, KV_W), lambda b, *_: (b, 0, 0))
    any_spec = pl.BlockSpec(memory_space=pl.ANY)
    grid_spec = pltpu.PrefetchScalarGridSpec(
        num_scalar_prefetch=2,
        grid=(nb,),
        in_specs=[spec(rows, DH), spec(TOKEN_ROWS, LANES), spec(TOKEN_ROWS, DH), spec(TOKEN_ROWS, DH),
                  spec(TOKEN_ROWS, DH), spec(TOKEN_ROWS, DH), win, win, spec(rows, DH), spec(rows, LANES),
                  any_spec, any_spec],
        out_specs=spec(rows, DH),
        scratch_shapes=[pltpu.VMEM((2, N_KV * dec_seq, TOP_N * SEL_BLOCK, DH), F32),
                        pltpu.VMEM((2, N_KV * dec_seq, TOP_N * SEL_BLOCK, DH), F32),
                        pltpu.SemaphoreType.DMA((2, 2))],
    )
    return pl.pallas_call(
        functools.partial(_attn_sample_body, dec_seq=dec_seq, n_pool_blk=past_len // SEL_BLOCK),
        grid_spec=grid_spec,
        out_shape=jax.ShapeDtypeStruct((nb, N_KV, rows, DH), F32),
        compiler_params=_params("arbitrary"),
        name="attn_sample",
    )(page_table, idx_flat, q_tm, idx_pad, ksn, vsn, kwn, vwn, k_win, v_win, oc_tm, gates_tm, k_pool, v_pool)


def _mix_out_sample_body(oa_ref, u_ref, st_ref, x_ref, cw_ref, cb_ref, lng_ref, lnb_ref, ga_ref, gc_ref,
                         woa_ref, woc_ref, gm_ref, wmq_ref, o_ref, qm_ref, conv_scr, *, nb, dec_seq):
    hist = CONV_W - 1

    def ext(j):
        if j < hist:
            return st_ref[j * nb:(j + 1) * nb, :]
        return u_ref[(j - hist) * nb:(j - hist + 1) * nb, :]

    for t in range(dec_seq):
        acc = jnp.zeros((nb, CONV_CH), F32) + cb_ref[...]
        for k in range(CONV_W):
            acc = acc + cw_ref[k:k + 1, :] * ext(t + k)
        conv_scr[t * nb:(t + 1) * nb, :] = acc
    conv_n = _ln_silu_rms(conv_scr[...], lng_ref, lnb_ref, gc_ref)
    attn_n = (_rms(oa_ref[...]) * ga_ref[...]).astype(BF16)
    x1 = x_ref[...] + _dot(attn_n, woa_ref[...]) + _dot(conv_n, woc_ref[...])
    o_ref[...] = x1
    qm_ref[...] = _dot((_rms(x1) * gm_ref[...]).astype(BF16), wmq_ref[...])


def _mix_out_sample(o_attn, u, conv_state, x, mw, g_mem, w_mq, nb, dec_seq):
    m = x.shape[0]
    args = (o_attn, u, conv_state, x) + tuple(mw) + (g_mem, w_mq)
    return pl.pallas_call(
        functools.partial(_mix_out_sample_body, nb=nb, dec_seq=dec_seq),
        grid=(1,),
        in_specs=[_const_spec(a.shape) for a in args],
        out_specs=[_const_spec((m, D_MODEL)), _const_spec((m, MEM_W))],
        out_shape=[jax.ShapeDtypeStruct((m, D_MODEL), F32), jax.ShapeDtypeStruct((m, MEM_W), F32)],
        scratch_shapes=[pltpu.VMEM((m, CONV_CH), F32)],
        compiler_params=_params("arbitrary"),
        name="mix_out_sample",
    )(*args)


def _mem_attn_sample_body(q_ref, mk_ref, mv_ref, o_ref):
    o_ref[0] = _mem_attn_core(q_ref[0], mk_ref[0].astype(BF16), mv_ref[0].astype(BF16))


def _mem_attn_sample(q_pad, mk, mv):
    nb, rows, _ = q_pad.shape
    n_mem = mk.shape[1]
    q_spec = pl.BlockSpec((1, rows, MEM_W), lambda b: (b, 0, 0))
    mem = pl.BlockSpec((1, n_mem, MEM_W), lambda b: (b, 0, 0))
    return pl.pallas_call(
        _mem_attn_sample_body,
        grid=(nb,),
        in_specs=[q_spec, mem, mem],
        out_specs=q_spec,
        out_shape=jax.ShapeDtypeStruct((nb, rows, MEM_W), BF16),
        compiler_params=_params("arbitrary"),
        name="mem_attn_sample",
    )(q_pad, mk, mv)


def _ffn_sample_body(x_ref, a_ref, wo_ref, g_ref, wg_ref, wu_ref, cw_ref, cb_ref, wd_ref, gf_ref, st_ref,
                     o_ref, sto_ref, x2_scr, h_scr, acc_scr, gate_scr, conv_scr, *, nb, dec_seq):
    f = pl.program_id(0)
    hist = FFN_CONV_W - 1

    @pl.when(f == 0)
    def _():
        x2 = x_ref[...] + _dot(a_ref[...], wo_ref[...])
        x2_scr[...] = x2
        h_scr[...] = (_rms(x2) * g_ref[...]).astype(BF16)
        acc_scr[...] = jnp.zeros(acc_scr.shape, F32)

    hb = h_scr[...]
    gate_scr[...] = _dot(hb, wg_ref[...])

    def ext(j):
        if j < hist:
            return st_ref[j * nb:(j + 1) * nb, :]
        return gate_scr[(j - hist) * nb:(j - hist + 1) * nb, :]

    for t in range(dec_seq):
        acc = cb_ref[...] + cw_ref[0:1, :] * ext(t)
        for k in range(1, FFN_CONV_W):
            acc = acc + cw_ref[k:k + 1, :] * ext(t + k)
        conv_scr[t * nb:(t + 1) * nb, :] = acc
    for j in range(hist):
        sto_ref[j * nb:(j + 1) * nb, :] = ext(dec_seq + j)
    conv = conv_scr[...]
    a = conv * jax.nn.sigmoid(conv) * _dot(hb, wu_ref[...])
    acc_scr[...] += _dot(a.astype(BF16), wd_ref[...])
    _ffn_tail(f, pl.num_programs(0), x2_scr, acc_scr, gf_ref, o_ref)


def _ffn_sample(x1, a, w_mo, fw, ffn_state, nb, dec_seq, tf=512):
    g, wg, wu, cw, cb, wd, gf = fw
    m = x1.shape[0]
    nf = D_FF // tf
    hist = FFN_CONV_W - 1
    col = lambda r: pl.BlockSpec((r, tf), lambda f: (0, f))
    full = lambda shape: pl.BlockSpec(shape, lambda f: (0,) * len(shape))
    return pl.pallas_call(
        functools.partial(_ffn_sample_body, nb=nb, dec_seq=dec_seq),
        grid=(nf,),
        in_specs=[full(x1.shape), full(a.shape), full(w_mo.shape), full(g.shape), col(D_MODEL), col(D_MODEL),
                  col(cw.shape[0]), col(1), pl.BlockSpec((tf, D_MODEL), lambda f: (f, 0)), full(gf.shape),
                  col(nb * hist)],
        out_specs=[full((m, D_MODEL)), col(nb * hist)],
        out_shape=[jax.ShapeDtypeStruct((m, D_MODEL), F32), jax.ShapeDtypeStruct((nb * hist, D_FF), F32)],
        scratch_shapes=[pltpu.VMEM((m, D_MODEL), F32), pltpu.VMEM((m, D_MODEL), BF16), pltpu.VMEM((m, D_MODEL), F32),
                        pltpu.VMEM((m, tf), F32), pltpu.VMEM((m, tf), F32)],
        compiler_params=_params("arbitrary"),
        name="ffn_sample",
    )(x1, a, w_mo, g, wg, wu, cw, cb, wd, gf, ffn_state)


def _prepare_weights(norm_mix_g, w_in, b_gate, cmp_k, cmp_v, conv_w, conv_b, conv_ln_g, conv_ln_b,
                     grp_norm_attn_g, grp_norm_conv_g, w_out, norm_mem_g, mem_norm_g, w_mq, w_mk, w_mv, w_mo,
                     norm_ffn_g, w_ffn_gate, w_ffn_up, ffn_conv_w, ffn_conv_b, w_ffn_down, norm_final_g):
    vec = lambda v: v.reshape(1, -1)
    kv_end = ATTN_W + 6 * KV_W
    n_gate_cols = N_KV * GQA * N_GATE
    per_kv = GQA * N_GATE
    wg = w_in[:, kv_end:kv_end + n_gate_cols].reshape(D_MODEL, N_KV, per_kv)
    wg = jnp.pad(wg, ((0, 0), (0, 0), (0, LANES - per_kv))).reshape(D_MODEL, N_KV * LANES)
    bg = jnp.pad(b_gate.reshape(N_KV, per_kv), ((0, 0), (0, LANES - per_kv))).reshape(1, N_KV * LANES)
    return dict(
        in_proj=(vec(norm_mix_g), w_in[:, :ATTN_W].astype(BF16), w_in[:, ATTN_W:kv_end].astype(BF16),
                 wg.astype(BF16), bg, w_in[:, kv_end + n_gate_cols:].astype(BF16)),
        cmp_k=_compress_weights(*cmp_k),
        cmp_v=_compress_weights(*cmp_v),
        mix=(jnp.pad(conv_w, ((0, 32 - CONV_W), (0, 0))), vec(conv_b), vec(conv_ln_g), vec(conv_ln_b),
             vec(grp_norm_attn_g), vec(grp_norm_conv_g), w_out[:ATTN_W].astype(BF16), w_out[ATTN_W:].astype(BF16)),
        mem_kv=(vec(mem_norm_g), w_mk.astype(BF16), w_mv.astype(BF16)),
        mem=(vec(norm_mem_g), w_mq.astype(BF16), w_mo.astype(BF16)),
        ffn=(vec(norm_ffn_g), w_ffn_gate.astype(BF16), w_ffn_up.astype(BF16),
             jnp.pad(ffn_conv_w, ((0, 8 - FFN_CONV_W), (0, 0))), vec(ffn_conv_b), w_ffn_down.astype(BF16),
             vec(norm_final_g)),
    )


def _prompt_forward(x_prompt, mem_prompt, w):
    batch, t, _ = x_prompt.shape
    x = x_prompt.reshape(batch * t, D_MODEL)
    (q, kc, vc, ks, vs, kw, vw, ksb, vsb, kwb, vwb, gates, u) = _in_proj(x, *w["in_proj"], tm=256)
    kcc = _compress_prompt(kc, w["cmp_k"], batch)
    vcc = _compress_prompt(vc, w["cmp_v"], batch)
    o_attn = _nsa_prompt(q, kcc, vcc, ksb, vsb, kwb, vwb, gates, batch)
    x1 = _mix_out_prompt(o_attn, u, x, w["mix"], batch)
    n_mem = mem_prompt.shape[1]
    mk, mv = _mem_kv(mem_prompt.reshape(batch * n_mem, D_MODEL), *w["mem_kv"])
    g_mem, w_mq, w_mo = w["mem"]
    x2 = _mem_attn_prompt(x1, g_mem, w_mq, mk.reshape(batch, n_mem, MEM_W), mv.reshape(batch, n_mem, MEM_W),
                          w_mo, batch)
    y, ffn_tail = _ffn_prompt(x2, *w["ffn"], batch=batch)
    kv5 = lambda a: a.reshape(1, batch, t, N_KV, DH)
    win = lambda a: a.reshape(batch, t, N_KV, DH)[None, :, t - min(WINDOW, t):]
    tiles = ffn_tail.shape[0] // batch
    new_ffn = ffn_tail.reshape(batch, tiles, ffn_tail.shape[1], D_FF)[:, -1, -(FFN_CONV_W - 1):]
    new_conv = u.reshape(batch, t, CONV_CH)[:, t - (CONV_W - 1):]
    mem5 = lambda a: a.reshape(1, batch, n_mem, MEM_HEADS, MEM_DH)
    return (y.reshape(batch, t, D_MODEL), kv5(kc), kv5(vc), kv5(ks), kv5(vs), win(kw), win(vw),
            new_conv[None], new_ffn[None], mem5(mk), mem5(mv))


def _pad_axis(a, axis, size):
    pads = [(0, 0)] * a.ndim
    pads[axis] = (0, size - a.shape[axis])
    return jnp.pad(a, pads)


def _sample_forward(x_sample, pools, k_win, v_win, conv_state, ffn_state, mem_k, mem_v, page_table, w):
    nb, dec_seq, _ = x_sample.shape
    m = nb * dec_seq
    past_len = page_table.shape[1] * PAGE
    assert dec_seq <= HEAD_ROWS and k_win.shape[1] == WINDOW
    x = x_sample.reshape(m, D_MODEL)
    (q, kc, vc, ks, vs, kw, vw, ksb, vsb, kwb, vwb, gates, u) = _in_proj(x, *w["in_proj"], tm=m)
    pool_kc, pool_vc, pool_ks, pool_vs = pools
    kcc = _compress_sample(pool_kc, page_table, w["cmp_k"])
    vcc = _compress_sample(pool_vc, page_table, w["cmp_v"])

    q5 = q.reshape(nb, dec_seq, N_KV, GQA, DH)
    q_hm = _pad_axis(q5.transpose(0, 2, 3, 1, 4), 3, HEAD_ROWS).reshape(nb, N_KV, GQA * HEAD_ROWS, DH)
    q_tm = _pad_axis(q5.transpose(0, 2, 1, 3, 4), 3, TOKEN_ROWS).reshape(nb, N_KV, dec_seq * TOKEN_ROWS, DH)
    oc_hm, idx = _cmp_select_sample(q_hm, kcc, vcc, past_len)
    oc_tm = oc_hm.reshape(nb, N_KV, GQA, HEAD_ROWS, DH)[:, :, :, :dec_seq].transpose(0, 1, 3, 2, 4)
    oc_tm = _pad_axis(oc_tm, 3, TOKEN_ROWS).reshape(nb, N_KV, dec_seq * TOKEN_ROWS, DH)
    idx_flat = idx[:, :, :dec_seq, :TOP_N].reshape(nb, N_KV * dec_seq * TOP_N)
    idx_pad = _pad_axis(idx, 2, TOKEN_ROWS)
    new_rows = lambda a: _pad_axis(a.reshape(nb, dec_seq, N_KV, DH).transpose(0, 2, 1, 3), 2, TOKEN_ROWS)
    gates_tm = gates.reshape(nb, dec_seq, N_KV, LANES)[..., :GQA * N_GATE].reshape(nb, dec_seq, N_KV, GQA, N_GATE)
    gates_tm = _pad_axis(_pad_axis(gates_tm.transpose(0, 2, 1, 3, 4), 3, TOKEN_ROWS), 4, LANES)
    gates_tm = gates_tm.reshape(nb, N_KV, dec_seq * TOKEN_ROWS, LANES)
    o_tm = _attn_sample(page_table, idx_flat, q_tm, idx_pad, new_rows(ksb), new_rows(vsb), new_rows(kwb),
                        new_rows(vwb), k_win.reshape(nb, WINDOW, KV_W), v_win.reshape(nb, WINDOW, KV_W),
                        oc_tm, gates_tm, pool_ks, pool_vs, dec_seq, past_len)
    o_attn = o_tm.reshape(nb, N_KV, dec_seq, TOKEN_ROWS, DH)[:, :, :, :GQA].transpose(2, 0, 1, 3, 4).reshape(m, ATTN_W)
    step_major = lambda a: a.reshape(nb, -1, a.shape[-1]).transpose(1, 0, 2).reshape(-1, a.shape[-1])
    batch_major = lambda a: a.reshape(-1, nb, a.shape[-1]).transpose(1, 0, 2)

    g_mem, w_mq, w_mo = w["mem"]
    x1, qm = _mix_out_sample(o_attn, step_major(u), step_major(conv_state), step_major(x), w["mix"], g_mem, w_mq,
                             nb, dec_seq)
    n_mem = mem_k.shape[1]
    q_pad = _pad_axis(batch_major(qm), 1, TOKEN_ROWS).astype(BF16)
    a = _mem_attn_sample(q_pad, mem_k.reshape(nb, n_mem, MEM_W), mem_v.reshape(nb, n_mem, MEM_W))
    y, new_ffn = _ffn_sample(x1, step_major(a[:, :dec_seq]), w_mo, w["ffn"], step_major(ffn_state), nb, dec_seq)

    kv5 = lambda a: a.reshape(1, nb, dec_seq, N_KV, DH)
    shift = lambda buf, new: jnp.concatenate([buf[:, dec_seq:], new.reshape((nb, dec_seq) + buf.shape[2:])], axis=1)[None]
    return (batch_major(y), kv5(kc), kv5(vc), kv5(ks), kv5(vs), shift(k_win, kw), shift(v_win, vw),
            shift(conv_state, u), batch_major(new_ffn)[None])


def kernel(x_prompt, x_sample, cache_k_cmp, cache_v_cmp, cache_k_sel, cache_v_sel, cache_k_win, cache_v_win,
           state_conv, state_ffn_conv, cache_mem_k, cache_mem_v, page_table, mem_prompt,
           norm_mix_g, w_in, b_gate, cmp_k_pe, cmp_k_w1, cmp_k_b1, cmp_k_w2, cmp_v_pe, cmp_v_w1, cmp_v_b1, cmp_v_w2,
           conv_w, conv_b, conv_ln_g, conv_ln_b, grp_norm_attn_g, grp_norm_conv_g, w_out,
           norm_mem_g, mem_norm_g, w_mq, w_mk, w_mv, w_mo,
           norm_ffn_g, w_ffn_gate, w_ffn_up, ffn_conv_w, ffn_conv_b, w_ffn_down, norm_final_g):
    assert w_in.shape[0] == 1, "single-layer step"
    w = _prepare_weights(norm_mix_g[0], w_in[0], b_gate[0],
                         (cmp_k_pe[0], cmp_k_w1[0], cmp_k_b1[0], cmp_k_w2[0]),
                         (cmp_v_pe[0], cmp_v_w1[0], cmp_v_b1[0], cmp_v_w2[0]),
                         conv_w[0], conv_b[0], conv_ln_g[0], conv_ln_b[0], grp_norm_attn_g[0], grp_norm_conv_g[0],
                         w_out[0], norm_mem_g[0], mem_norm_g[0], w_mq[0], w_mk[0], w_mv[0], w_mo[0],
                         norm_ffn_g[0], w_ffn_gate[0], w_ffn_up[0], ffn_conv_w[0], ffn_conv_b[0], w_ffn_down[0],
                         norm_final_g)
    p = _prompt_forward(x_prompt, mem_prompt, w)
    s = _sample_forward(x_sample, (cache_k_cmp[0], cache_v_cmp[0], cache_k_sel[0], cache_v_sel[0]),
                        cache_k_win[0], cache_v_win[0], state_conv[0], state_ffn_conv[0],
                        cache_mem_k[0], cache_mem_v[0], page_table, w)
    return (p[0], s[0]) + p[1:] + s[1:]
```

```python
import functools

import jax
import jax.numpy as jnp
from jax import lax
from jax.experimental import pallas as pl
from jax.experimental.pallas import tpu as pltpu

F32 = jnp.float32
BF16 = jnp.bfloat16
I32 = jnp.int32

D_MODEL = 2048
N_KV = 2
GQA = 4
DH = 128
ATTN_W = N_KV * GQA * DH
KV_W = N_KV * DH
N_GATE = 3
BLOCK_CMP = 32
STRIDE_CMP = 16
CMP_HID = 256
SEL_BLOCK = 64
TOP_N = 16
WINDOW = 512
CONV_CH = D_MODEL - ATTN_W
CONV_W = 31
D_FF = 5632
FFN_CONV_W = 3
MEM_HEADS = 4
MEM_DH = 128
MEM_W = MEM_HEADS * MEM_DH
ATTN_SCALE = DH ** -0.5
MEM_SCALE = MEM_DH ** -0.5
EPS = 1e-6
NEG = -1e30
BIG = 1e30
LANES = 128
VMEM_LIMIT = 56 * 1024 * 1024


def _dot(a, b):
    return jnp.dot(a, b, preferred_element_type=F32)


def _dot_nt(a, b):
    return lax.dot_general(a, b, (((1,), (1,)), ((), ())), preferred_element_type=F32)


def _rms(x):
    return x * lax.rsqrt(jnp.mean(x * x, axis=-1, keepdims=True) + EPS)


def _const_spec(shape):
    return pl.BlockSpec(shape, lambda *_: (0,) * len(shape), pipeline_mode=pl.Buffered(1))


def _params(*sem):
    return pltpu.CompilerParams(dimension_semantics=sem, vmem_limit_bytes=VMEM_LIMIT)


def _in_proj_body(x_ref, g_ref, wq_ref, wkv_ref, wg_ref, bg_ref, wglu_ref,
                  q_ref, kc_ref, vc_ref, ks_ref, vs_ref, kw_ref, vw_ref,
                  ksb_ref, vsb_ref, kwb_ref, vwb_ref, gates_ref, u_ref):
    hb = (_rms(x_ref[...]) * g_ref[...]).astype(BF16)
    half = ATTN_W // 2
    for c in range(2):
        q_ref[:, c * half:(c + 1) * half] = _dot(hb, wq_ref[:, c * half:(c + 1) * half]).astype(BF16)
    f32_outs = (kc_ref, vc_ref, ks_ref, vs_ref, kw_ref, vw_ref)
    bf_outs = (None, None, ksb_ref, vsb_ref, kwb_ref, vwb_ref)
    for c in range(6):
        r = _dot(hb, wkv_ref[:, c * KV_W:(c + 1) * KV_W])
        f32_outs[c][...] = r
        if bf_outs[c] is not None:
            bf_outs[c][...] = r.astype(BF16)
    gates_ref[...] = jax.nn.sigmoid(_dot(hb, wg_ref[...]) + bg_ref[...])
    cw = 256
    for c in range(CONV_CH // cw):
        a = _dot(hb, wglu_ref[:, c * cw:(c + 1) * cw])
        gt = _dot(hb, wglu_ref[:, CONV_CH + c * cw:CONV_CH + (c + 1) * cw])
        u_ref[:, c * cw:(c + 1) * cw] = a * jax.nn.sigmoid(gt)


def _in_proj(x, g, wq, wkv, wg, bg, wglu, tm):
    m = x.shape[0]
    row = lambda w: pl.BlockSpec((tm, w), lambda i: (i, 0))
    out_shape = ([jax.ShapeDtypeStruct((m, ATTN_W), BF16)]
                 + [jax.ShapeDtypeStruct((m, KV_W), F32)] * 6
                 + [jax.ShapeDtypeStruct((m, KV_W), BF16)] * 4
                 + [jax.ShapeDtypeStruct((m, N_KV * LANES), F32),
                    jax.ShapeDtypeStruct((m, CONV_CH), F32)])
    out_specs = ([row(ATTN_W)] + [row(KV_W)] * 10 + [row(N_KV * LANES), row(CONV_CH)])
    return pl.pallas_call(
        _in_proj_body,
        grid=(m // tm,),
        in_specs=[row(D_MODEL), _const_spec(g.shape), _const_spec(wq.shape), _const_spec(wkv.shape),
                  _const_spec(wg.shape), _const_spec(bg.shape), _const_spec(wglu.shape)],
        out_specs=out_specs,
        out_shape=out_shape,
        compiler_params=_params("arbitrary"),
        name="in_proj",
    )(x, g, wq, wkv, wg, bg, wglu)


def _compress_rows(get_lanes, n, pe_ref, w1l_ref, w1t_ref, b1_ref, w2_ref):
    xk = jnp.concatenate([get_lanes(l) for l in range(STRIDE_CMP)], axis=1)
    lead = _dot((xk + pe_ref[0:1, :]).astype(BF16), w1l_ref[...])
    trail = _dot((xk + pe_ref[1:2, :]).astype(BF16), w1t_ref[...])
    trail_next = pltpu.roll(trail, n - 1, axis=0)
    hid = jax.nn.gelu(lead + trail_next + b1_ref[...])
    out = _dot(hid.astype(BF16), w2_ref[...])
    rows = lax.broadcasted_iota(I32, (n, 1), 0)
    return jnp.where(rows < n - 1, out, 0.0)


def _compress_prompt_body(x_ref, pe_ref, w1l_ref, w1t_ref, b1_ref, w2_ref, o_ref):
    n = x_ref.shape[1]
    for k in range(N_KV):
        get = lambda l, k=k: x_ref[0, :, l * KV_W + k * DH:l * KV_W + (k + 1) * DH]
        o_ref[0, k] = _compress_rows(get, n, pe_ref, w1l_ref, w1t_ref, b1_ref, w2_ref)


def _compress_weights(pe, w1, b1, w2):
    half = STRIDE_CMP * DH
    pe2 = pe.reshape(2, half)
    return pe2, w1[:half].astype(BF16), w1[half:].astype(BF16), b1.reshape(1, CMP_HID), w2.astype(BF16)


def _compress_prompt(rows, cw, batch):
    n = rows.shape[0] // batch // STRIDE_CMP
    x = rows.reshape(batch, n, STRIDE_CMP * KV_W)
    return pl.pallas_call(
        _compress_prompt_body,
        grid=(batch,),
        in_specs=[pl.BlockSpec((1, n, STRIDE_CMP * KV_W), lambda b: (b, 0, 0))] + [_const_spec(w.shape) for w in cw],
        out_specs=pl.BlockSpec((1, N_KV, n, DH), lambda b: (b, 0, 0, 0)),
        out_shape=jax.ShapeDtypeStruct((batch, N_KV, n, DH), F32),
        compiler_params=_params("arbitrary"),
        name="compress_prompt",
    )(x, *cw)


def _overlap_matrix(n_cmp, n_blk=LANES):
    i = lax.broadcasted_iota(I32, (n_cmp, n_blk), 0) * STRIDE_CMP
    j = lax.broadcasted_iota(I32, (n_cmp, n_blk), 1) * SEL_BLOCK
    ov = jnp.maximum(jnp.minimum(i + BLOCK_CMP, j + SEL_BLOCK) - jnp.maximum(i, j), 0)
    return (ov.astype(F32) * (1.0 / BLOCK_CMP)).astype(BF16)


def _importance(p_sum, ov):
    hi = p_sum.astype(BF16)
    r1 = p_sum - hi.astype(F32)
    mid = r1.astype(BF16)
    lo = (r1 - mid.astype(F32)).astype(BF16)
    return _dot(hi, ov) + _dot(mid, ov) + _dot(lo, ov)


def _cmp_attention(q_heads, kcc, vcc, cmask):
    p_sum = jnp.zeros(cmask.shape, F32)
    outs = []
    for qg in q_heads:
        s = jnp.where(cmask, _dot_nt(qg, kcc) * ATTN_SCALE, NEG)
        m = jnp.max(s, axis=-1, keepdims=True)
        e = jnp.where(cmask, jnp.exp(s - m), 0.0)
        p = e / jnp.maximum(jnp.sum(e, axis=-1, keepdims=True), 1e-30)
        p_sum = p_sum + p
        outs.append(_dot(p.astype(BF16), vcc))
    return outs, p_sum


def _softmax_av(s, v):
    p = jnp.exp(s - jnp.max(s, axis=-1, keepdims=True))
    return _dot(p.astype(BF16), v) / jnp.sum(p, axis=-1, keepdims=True)


GATE_ROWS = 16


def _nsa_prompt_body(qt_ref, kcc_ref, vcct_ref, ks_ref, vst_ref, kw_ref, vwt_ref, gt_ref, o_ref,
                     acc_scr, bias_scr, *, tq, tk, n_sel):
    i = pl.program_id(2)
    t0 = i * tq
    nq = GQA * tq
    heads = lambda a: jnp.concatenate([a] * GQA, axis=1)
    q_pos = t0 + lax.broadcasted_iota(I32, (1, tq), 1)
    q_pos4 = heads(q_pos)
    qt = jnp.concatenate([qt_ref[g * DH:(g + 1) * DH, :] for g in range(GQA)], axis=1)

    n_cmp = kcc_ref.shape[2]
    cmp_end = lax.broadcasted_iota(I32, (n_cmp, 1), 0) * STRIDE_CMP + (BLOCK_CMP - 1)
    cvis = cmp_end <= q_pos4
    s = jnp.where(cvis, _dot(kcc_ref[0, 0].astype(BF16), qt) * ATTN_SCALE, NEG)
    e = jnp.where(cvis, jnp.exp(s - jnp.max(s, axis=0, keepdims=True)), 0.0)
    p = e / jnp.maximum(jnp.sum(e, axis=0, keepdims=True), 1e-30)
    o_cmp = _dot(vcct_ref[0, 0].astype(BF16), p.astype(BF16))
    p_sum = p[:, 0:tq]
    for g in range(1, GQA):
        p_sum = p_sum + p[:, g * tq:(g + 1) * tq]

    ov_i = lax.broadcasted_iota(I32, (n_sel, n_cmp), 1) * STRIDE_CMP
    ov_j = lax.broadcasted_iota(I32, (n_sel, n_cmp), 0) * SEL_BLOCK
    ov = jnp.maximum(jnp.minimum(ov_i + BLOCK_CMP, ov_j + SEL_BLOCK) - jnp.maximum(ov_i, ov_j), 0)
    ov = (ov.astype(F32) * (1.0 / BLOCK_CMP)).astype(BF16)
    hi = p_sum.astype(BF16)
    r1 = p_sum - hi.astype(F32)
    mid = r1.astype(BF16)
    lo = (r1 - mid.astype(F32)).astype(BF16)
    imp = _dot(ov, hi) + _dot(ov, mid) + _dot(ov, lo)
    blk = lax.broadcasted_iota(I32, (n_sel, tq), 0)
    cur = q_pos // SEL_BLOCK
    forced = (blk == 0) | (blk == cur) | (blk == cur - 1)
    s_t = jnp.where(blk * SEL_BLOCK <= q_pos, jnp.where(forced, BIG, imp), -BIG)
    rank = jnp.zeros((n_sel, tq), F32)
    for ib in range(n_sel):
        row = s_t[ib:ib + 1, :]
        beats = (row > s_t) | ((row == s_t) & (blk > ib))
        rank = rank + jnp.where(beats, 1.0, 0.0)
    bias_scr[...] = jnp.where(rank < TOP_N, 0.0, NEG)

    acc_scr[...] = jnp.zeros(acc_scr.shape, F32)
    blocks_per_tile = tk // SEL_BLOCK

    def key_tile(j, m, l, causal):
        k0 = pl.multiple_of(j * tk, tk)
        bias = jnp.concatenate(
            [jnp.broadcast_to(bias_scr[pl.ds(j * blocks_per_tile + c, 1), :], (SEL_BLOCK, tq))
             for c in range(blocks_per_tile)], axis=0)
        if causal:
            bias = jnp.where(k0 + lax.broadcasted_iota(I32, (tk, 1), 0) <= q_pos, bias, NEG)
        s = _dot(ks_ref[pl.ds(k0, tk), :], qt) * ATTN_SCALE + heads(bias)
        m_new = jnp.maximum(m, jnp.max(s, axis=0, keepdims=True))
        alpha = jnp.exp(m - m_new)
        p = jnp.exp(s - m_new)
        acc_scr[...] = alpha * acc_scr[...] + _dot(vst_ref[:, pl.ds(k0, tk)], p.astype(BF16))
        return m_new, alpha * l + jnp.sum(p, axis=0, keepdims=True)

    n_past = t0 // tk
    m, l = lax.fori_loop(0, n_past, lambda j, c: key_tile(j, c[0], c[1], False),
                         (jnp.full((1, nq), NEG, F32), jnp.zeros((1, nq), F32)))
    m, l = key_tile(n_past, m, l, True)
    o_sel = acc_scr[...] / l

    span = WINDOW + tq
    w0 = pl.multiple_of(jnp.maximum(t0 - WINDOW, 0), tq)
    kw_pos = w0 + lax.broadcasted_iota(I32, (span, 1), 0)
    wvis = (kw_pos <= q_pos4) & (q_pos4 - kw_pos < WINDOW)
    s = jnp.where(wvis, _dot(kw_ref[pl.ds(w0, span), :], qt) * ATTN_SCALE, NEG)
    p = jnp.exp(s - jnp.max(s, axis=0, keepdims=True))
    o_win = _dot(vwt_ref[:, pl.ds(w0, span)], p.astype(BF16)) / jnp.sum(p, axis=0, keepdims=True)

    gt = gt_ref[...]
    gate = lambda c: jnp.concatenate([gt[g * N_GATE + c:g * N_GATE + c + 1, :] for g in range(GQA)], axis=1)
    o = gate(0) * o_cmp + gate(1) * o_sel + gate(2) * o_win
    for g in range(GQA):
        o_ref[:, g * DH:(g + 1) * DH] = o[:, g * tq:(g + 1) * tq].T


def _nsa_prompt(qt, kcc, vcct, ksb, vst, kwb, vwt, gates_t, batch, tq=128, tk=512):
    m = qt.shape[1]
    t = m // batch
    tk = min(tk, t)
    nt = t // tq
    n_cmp = kcc.shape[2]
    n_sel = max(t // SEL_BLOCK, 8)
    assert t % tk == 0 and tk % tq == 0 and t >= WINDOW + tq and tq % LANES == 0
    rows = lambda: pl.BlockSpec((t, DH), lambda b, k, i: (b, k))
    cols = lambda: pl.BlockSpec((DH, t), lambda b, k, i: (k, b))
    return pl.pallas_call(
        functools.partial(_nsa_prompt_body, tq=tq, tk=tk, n_sel=n_sel),
        grid=(batch, N_KV, nt),
        in_specs=[pl.BlockSpec((GQA * DH, tq), lambda b, k, i: (k, b * nt + i)),
                  pl.BlockSpec((1, 1, n_cmp, DH), lambda b, k, i: (b, k, 0, 0)),
                  pl.BlockSpec((1, 1, DH, n_cmp), lambda b, k, i: (b, k, 0, 0)),
                  rows(), cols(), rows(), cols(),
                  pl.BlockSpec((GATE_ROWS, tq), lambda b, k, i: (k, b * nt + i))],
        out_specs=pl.BlockSpec((tq, GQA * DH), lambda b, k, i: (b * nt + i, k)),
        out_shape=jax.ShapeDtypeStruct((m, ATTN_W), F32),
        scratch_shapes=[pltpu.VMEM((DH, GQA * tq), F32), pltpu.VMEM((n_sel, tq), F32)],
        compiler_params=_params("arbitrary", "arbitrary", "arbitrary"),
        name="nsa_prompt",
    )(qt, kcc, vcct, ksb, vst, kwb, vwt, gates_t)


def _ln_silu_rms(y, lng_ref, lnb_ref, gn_ref):
    mu = jnp.mean(y, axis=-1, keepdims=True)
    var = jnp.mean(jnp.square(y - mu), axis=-1, keepdims=True)
    y = (y - mu) * lax.rsqrt(var + EPS) * lng_ref[...] + lnb_ref[...]
    y = y * jax.nn.sigmoid(y)
    return (_rms(y) * gn_ref[...]).astype(BF16)


def _mix_out_prompt_body(oa_ref, u_ref, halo_ref, x_ref, cw_ref, cb_ref, lng_ref, lnb_ref, ga_ref, gc_ref,
                         woa_ref, woc_ref, o_ref, ext_scr, conv_scr, *, tm, tiles_per_seq, rc):
    i = pl.program_id(0)
    pad = halo_ref.shape[0]
    first = (i % tiles_per_seq) == 0
    ext_scr[0:pad, :] = jnp.where(first, 0.0, halo_ref[...])
    ext_scr[pad:, :] = u_ref[...]
    off = pad - (CONV_W - 1)
    span = rc + pad

    def chunk(r, carry):
        base = pl.multiple_of(r * rc, rc)
        window = ext_scr[pl.ds(base, span), :]
        acc = jnp.zeros((rc, CONV_CH), F32) + cb_ref[...]
        for res in range(8):
            shifted = window if res == 0 else pltpu.roll(window, span - res, axis=0)
            for k in range(CONV_W):
                if (k + off) % 8 == res:
                    a8 = k + off - res
                    acc = acc + cw_ref[k:k + 1, :] * shifted[a8:a8 + rc]
        conv_scr[pl.ds(base, rc), :] = acc
        return carry

    lax.fori_loop(0, tm // rc, chunk, 0)
    conv_n = _ln_silu_rms(conv_scr[...], lng_ref, lnb_ref, gc_ref)
    attn_n = (_rms(oa_ref[...]) * ga_ref[...]).astype(BF16)
    half = D_MODEL // 2
    for c in range(2):
        cs = slice(c * half, (c + 1) * half)
        o_ref[:, cs] = x_ref[:, cs] + _dot(attn_n, woa_ref[:, cs]) + _dot(conv_n, woc_ref[:, cs])


def _mix_out_prompt(o_attn, u, x, mw, batch, tm=256, rc=32):
    m = x.shape[0]
    t = m // batch
    tm = min(tm, t)
    pad = 32
    row = lambda w: pl.BlockSpec((tm, w), lambda i: (i, 0))
    halo = pl.BlockSpec((pad, CONV_CH), lambda i: (jnp.maximum(i * (tm // pad) - 1, 0), 0))
    return pl.pallas_call(
        functools.partial(_mix_out_prompt_body, tm=tm, tiles_per_seq=t // tm, rc=rc),
        grid=(m // tm,),
        in_specs=[row(ATTN_W), row(CONV_CH), halo, row(D_MODEL)] + [_const_spec(w.shape) for w in mw],
        out_specs=row(D_MODEL),
        out_shape=jax.ShapeDtypeStruct((m, D_MODEL), F32),
        scratch_shapes=[pltpu.VMEM((tm + pad, CONV_CH), F32), pltpu.VMEM((tm, CONV_CH), F32)],
        compiler_params=_params("arbitrary"),
        name="mix_out_prompt",
    )(o_attn, u, u, x, *mw)


def _mem_kv_body(mem_ref, g_ref, wk_ref, wv_ref, k_ref, v_ref):
    mb = (_rms(mem_ref[...]) * g_ref[...]).astype(BF16)
    k_ref[...] = _dot(mb, wk_ref[...])
    v_ref[...] = _dot(mb, wv_ref[...])


def _mem_kv(mem, g, wk, wv, tm=256):
    m = mem.shape[0]
    row = lambda w: pl.BlockSpec((tm, w), lambda i: (i, 0))
    return pl.pallas_call(
        _mem_kv_body,
        grid=(m // tm,),
        in_specs=[row(D_MODEL), _const_spec(g.shape), _const_spec(wk.shape), _const_spec(wv.shape)],
        out_specs=[row(MEM_W), row(MEM_W)],
        out_shape=[jax.ShapeDtypeStruct((m, MEM_W), F32)] * 2,
        compiler_params=_params("arbitrary"),
        name="mem_kv",
    )(mem, g, wk, wv)


def _mem_attn_core(q, mk, mv):
    outs = []
    for h in range(MEM_HEADS):
        hs = slice(h * MEM_DH, (h + 1) * MEM_DH)
        s = _dot_nt(q[:, hs].astype(BF16), mk[:, hs]) * MEM_SCALE
        outs.append(_softmax_av(s, mv[:, hs]))
    return jnp.concatenate(outs, axis=1).astype(BF16)


def _mem_attn_prompt_body(x_ref, g_ref, wq_ref, mk_ref, mv_ref, wo_ref, o_ref):
    x = x_ref[...]
    hb = (_rms(x) * g_ref[...]).astype(BF16)
    a = _mem_attn_core(_dot(hb, wq_ref[...]), mk_ref[0].astype(BF16), mv_ref[0].astype(BF16))
    o_ref[...] = x + _dot(a, wo_ref[...])


def _mem_attn_prompt(x, g, wq, mk, mv, wo, batch, tm=256):
    m = x.shape[0]
    tiles_per_seq = m // batch // tm
    n_mem = mk.shape[1]
    row = pl.BlockSpec((tm, D_MODEL), lambda i: (i, 0))
    mem = pl.BlockSpec((1, n_mem, MEM_W), lambda i: (i // tiles_per_seq, 0, 0))
    return pl.pallas_call(
        _mem_attn_prompt_body,
        grid=(m // tm,),
        in_specs=[row, _const_spec(g.shape), _const_spec(wq.shape), mem, mem, _const_spec(wo.shape)],
        out_specs=row,
        out_shape=jax.ShapeDtypeStruct((m, D_MODEL), F32),
        compiler_params=_params("arbitrary"),
        name="mem_attn_prompt",
    )(x, g, wq, mk, mv, wo)


def _ffn_tail(f, nf, x_ref, acc_scr, gf_ref, o_ref):
    @pl.when(f == nf - 1)
    def _():
        o_ref[...] = _rms(x_ref[...] + acc_scr[...]) * gf_ref[...]


def _ffn_prompt_body(x_ref, g_ref, wg_ref, wu_ref, cw_ref, cb_ref, wd_ref, gf_ref, o_ref, st_ref,
                     h_scr, acc_scr, carry_scr, gext_scr, *, tm, tiles_per_seq):
    i = pl.program_id(0)
    f = pl.program_id(1)
    nf = pl.num_programs(1)
    hist = carry_scr.shape[1]

    @pl.when(f == 0)
    def _():
        h_scr[...] = (_rms(x_ref[...]) * g_ref[...]).astype(BF16)
        acc_scr[...] = jnp.zeros(acc_scr.shape, F32)

    hb = h_scr[...]
    gate = _dot(hb, wg_ref[...])
    first = (i % tiles_per_seq) == 0
    gext_scr[0:hist, :] = jnp.where(first, 0.0, carry_scr[f])
    gext_scr[hist:, :] = gate
    tail = gate[tm - hist:, :]
    carry_scr[f] = tail
    st_ref[0] = tail
    conv = (cw_ref[0:1, :] * gext_scr[hist - 2:hist - 2 + tm, :] + cw_ref[1:2, :] * gext_scr[hist - 1:hist - 1 + tm, :]
            + cw_ref[2:3, :] * gate + cb_ref[...])
    a = conv * jax.nn.sigmoid(conv) * _dot(hb, wu_ref[...])
    acc_scr[...] += _dot(a.astype(BF16), wd_ref[...])
    _ffn_tail(f, nf, x_ref, acc_scr, gf_ref, o_ref)


def _ffn_prompt(x, g, wg, wu, cw, cb, wd, gf, batch, tm=512, tf=512):
    m = x.shape[0]
    t = m // batch
    tm = min(tm, t)
    nf = D_FF // tf
    hist = 8
    row = pl.BlockSpec((tm, D_MODEL), lambda i, f: (i, 0))
    col = lambda r: pl.BlockSpec((r, tf), lambda i, f: (0, f))
    return pl.pallas_call(
        functools.partial(_ffn_prompt_body, tm=tm, tiles_per_seq=t // tm),
        grid=(m // tm, nf),
        in_specs=[row, _const_spec(g.shape), col(D_MODEL), col(D_MODEL), col(cw.shape[0]), col(1),
                  pl.BlockSpec((tf, D_MODEL), lambda i, f: (f, 0)), _const_spec(gf.shape)],
        out_specs=[row, pl.BlockSpec((1, hist, tf), lambda i, f: (i, 0, f))],
        out_shape=[jax.ShapeDtypeStruct((m, D_MODEL), F32), jax.ShapeDtypeStruct((m // tm, hist, D_FF), F32)],
        scratch_shapes=[pltpu.VMEM((tm, D_MODEL), BF16), pltpu.VMEM((tm, D_MODEL), F32),
                        pltpu.VMEM((nf, hist, tf), F32), pltpu.VMEM((tm + hist, tf), F32)],
        compiler_params=_params("arbitrary", "arbitrary"),
        name="ffn_prompt",
    )(x, g, wg, wu, cw, cb, wd, gf)


PAGE = 128
CHUNKS_PER_PAGE = PAGE // STRIDE_CMP


def _compress_sample_body(pt_ref, pool_ref, pe_ref, w1l_ref, w1t_ref, b1_ref, w2_ref, o_ref, buf, sem, *, n_pages):
    b = pl.program_id(0)
    slot = b % 2

    def pages(bb, sl, start):
        def one(j, carry):
            page = pt_ref[bb, j]
            for k in range(N_KV):
                cp = pltpu.make_async_copy(pool_ref.at[page, :, k, :],
                                           buf.at[sl, k, pl.ds(pl.multiple_of(j * PAGE, PAGE), PAGE), :], sem.at[sl])
                cp.start() if start else cp.wait()
            return carry
        lax.fori_loop(0, n_pages, one, 0)

    @pl.when(b == 0)
    def _():
        pages(0, 0, True)

    @pl.when(b + 1 < pl.num_programs(0))
    def _():
        pages(b + 1, 1 - slot, True)

    pages(b, slot, False)
    n = n_pages * CHUNKS_PER_PAGE
    for k in range(N_KV):
        get = lambda l, k=k: buf[slot, k, pl.ds(l, n, stride=STRIDE_CMP), :]
        o_ref[0, k] = _compress_rows(get, n, pe_ref, w1l_ref, w1t_ref, b1_ref, w2_ref)


def _compress_sample(pool, page_table, cw):
    nb, n_pages = page_table.shape
    n = n_pages * CHUNKS_PER_PAGE
    grid_spec = pltpu.PrefetchScalarGridSpec(
        num_scalar_prefetch=1,
        grid=(nb,),
        in_specs=[pl.BlockSpec(memory_space=pl.ANY)] + [_const_spec(w.shape) for w in cw],
        out_specs=pl.BlockSpec((1, N_KV, n, DH), lambda b, pt: (b, 0, 0, 0)),
        scratch_shapes=[pltpu.VMEM((2, N_KV, n_pages * PAGE, DH), F32), pltpu.SemaphoreType.DMA((2,))],
    )
    return pl.pallas_call(
        functools.partial(_compress_sample_body, n_pages=n_pages),
        grid_spec=grid_spec,
        out_shape=jax.ShapeDtypeStruct((nb, N_KV, n, DH), F32),
        compiler_params=_params("arbitrary"),
        name="compress_sample",
    )(page_table, pool, *cw)


HEAD_ROWS = 8
TOKEN_ROWS = 16


def _cmp_select_sample_body(q_ref, kcc_ref, vcc_ref, oc_ref, idx_ref, *, past_len, n_blk):
    n_cmp = kcc_ref.shape[2]
    rows = GQA * HEAD_ROWS
    q_pos = past_len + lax.broadcasted_iota(I32, (rows, 1), 0) % HEAD_ROWS
    cmp_end = lax.broadcasted_iota(I32, (1, n_cmp), 1) * STRIDE_CMP + (BLOCK_CMP - 1)
    cmask = cmp_end <= q_pos
    q_pos8 = past_len + lax.broadcasted_iota(I32, (HEAD_ROWS, 1), 0)
    blk = lax.broadcasted_iota(I32, (1, n_blk), 1)
    cur = q_pos8 // SEL_BLOCK
    valid = blk * SEL_BLOCK <= q_pos8
    forced = (blk == 0) | (blk == cur) | (blk == cur - 1)
    lane = lax.broadcasted_iota(I32, (HEAD_ROWS, n_blk), 1).astype(F32)
    out_lane = lax.broadcasted_iota(I32, (HEAD_ROWS, LANES), 1)
    ov = _overlap_matrix(n_cmp, n_blk)
    for k in range(N_KV):
        kcc = kcc_ref[0, k].astype(BF16)
        vcc = vcc_ref[0, k].astype(BF16)
        s = jnp.where(cmask, _dot_nt(q_ref[0, k], kcc) * ATTN_SCALE, NEG)
        e = jnp.where(cmask, jnp.exp(s - jnp.max(s, axis=-1, keepdims=True)), 0.0)
        p = e / jnp.maximum(jnp.sum(e, axis=-1, keepdims=True), 1e-30)
        oc_ref[0, k] = _dot(p.astype(BF16), vcc)
        p_sum = p[0:HEAD_ROWS]
        for g in range(1, GQA):
            p_sum = p_sum + p[g * HEAD_ROWS:(g + 1) * HEAD_ROWS]
        score = jnp.where(valid, jnp.where(forced, BIG, _importance(p_sum, ov)), -BIG)
        picks = jnp.zeros((HEAD_ROWS, LANES), F32)
        for n in range(TOP_N):
            best = jnp.max(score, axis=-1, keepdims=True)
            pick = jnp.min(jnp.where(score == best, lane, float(n_blk)), axis=-1, keepdims=True)
            picks = jnp.where(out_lane == n, pick, picks)
            score = jnp.where(lane == pick, -3e38, score)
        idx_ref[0, k] = picks.astype(I32)


def _cmp_select_sample(q_hm, kcc, vcc, past_len):
    nb = q_hm.shape[0]
    n_cmp = kcc.shape[2]
    n_blk = -(-(past_len // SEL_BLOCK + 1) // LANES) * LANES
    rows = GQA * HEAD_ROWS
    spec = lambda r, w: pl.BlockSpec((1, N_KV, r, w), lambda b: (b, 0, 0, 0))
    return pl.pallas_call(
        functools.partial(_cmp_select_sample_body, past_len=past_len, n_blk=n_blk),
        grid=(nb,),
        in_specs=[spec(rows, DH), spec(n_cmp, DH), spec(n_cmp, DH)],
        out_specs=[spec(rows, DH), spec(HEAD_ROWS, LANES)],
        out_shape=[jax.ShapeDtypeStruct((nb, N_KV, rows, DH), F32),
                   jax.ShapeDtypeStruct((nb, N_KV, HEAD_ROWS, LANES), I32)],
        compiler_params=_params("arbitrary"),
        name="cmp_select_sample",
    )(q_hm, kcc, vcc)


def _attn_sample_body(pt_ref, idx_s_ref, q_ref, idx_v_ref, ksn_ref, vsn_ref, kwn_ref, vwn_ref, kwin_ref, vwin_ref,
                      oc_ref, gates_ref, kpool_ref, vpool_ref, o_ref, kbuf, vbuf, sem, *, dec_seq, n_pool_blk):
    b = pl.program_id(0)
    slot = b % 2
    n_copy = N_KV * dec_seq * TOP_N
    blocks_per_page = PAGE // SEL_BLOCK

    def gather(bb, sl, start):
        def one(it, carry):
            blk = idx_s_ref[bb, it]
            src = jnp.where(blk < n_pool_blk, blk, 0)
            page = pt_ref[bb, src // blocks_per_page]
            r0 = pl.multiple_of((src % blocks_per_page) * SEL_BLOCK, SEL_BLOCK)
            kt = it // TOP_N
            dst = pl.ds(pl.multiple_of((it % TOP_N) * SEL_BLOCK, SEL_BLOCK), SEL_BLOCK)
            k = kt // dec_seq
            ck = pltpu.make_async_copy(kpool_ref.at[page, pl.ds(r0, SEL_BLOCK), k], kbuf.at[sl, kt, dst], sem.at[sl, 0])
            cv = pltpu.make_async_copy(vpool_ref.at[page, pl.ds(r0, SEL_BLOCK), k], vbuf.at[sl, kt, dst], sem.at[sl, 1])
            if start:
                ck.start()
                cv.start()
            else:
                ck.wait()
                cv.wait()
            return carry
        lax.fori_loop(0, n_copy, one, 0)

    @pl.when(b == 0)
    def _():
        gather(0, 0, True)

    @pl.when(b + 1 < pl.num_programs(0))
    def _():
        gather(b + 1, 1 - slot, True)

    gather(b, slot, False)

    n_keys = TOP_N * SEL_BLOCK
    rows = dec_seq * TOKEN_ROWS
    key_slot = lax.broadcasted_iota(I32, (LANES, n_keys), 1) // SEL_BLOCK
    expand = (lax.broadcasted_iota(I32, (LANES, n_keys), 0) == key_slot).astype(BF16)
    new_col = lax.broadcasted_iota(I32, (1, TOKEN_ROWS), 1)
    t_row = lax.broadcasted_iota(I32, (rows, 1), 0) // TOKEN_ROWS
    n_win = kwin_ref.shape[1]
    win_old_vis = lax.broadcasted_iota(I32, (1, n_win), 1) > t_row
    win_new_vis = new_col <= t_row
    for k in range(N_KV):
        qk = q_ref[0, k]
        pool_ok = _dot((idx_v_ref[0, k] < n_pool_blk).astype(BF16), expand)
        ksn, vsn = ksn_ref[0, k], vsn_ref[0, k]
        hs = slice(k * DH, (k + 1) * DH)
        s_old = jnp.where(win_old_vis, _dot_nt(qk, kwin_ref[0, :, hs].astype(BF16)) * ATTN_SCALE, NEG)
        s_new = jnp.where(win_new_vis, _dot_nt(qk, kwn_ref[0, k]) * ATTN_SCALE, NEG)
        m = jnp.maximum(jnp.max(s_old, axis=-1, keepdims=True), jnp.max(s_new, axis=-1, keepdims=True))
        p_old, p_new = jnp.exp(s_old - m), jnp.exp(s_new - m)
        l = jnp.sum(p_old, axis=-1, keepdims=True) + jnp.sum(p_new, axis=-1, keepdims=True)
        o_win = (_dot(p_old.astype(BF16), vwin_ref[0, :, hs].astype(BF16)) + _dot(p_new.astype(BF16), vwn_ref[0, k])) / l
        gates = gates_ref[0, k]
        for t in range(dec_seq):
            ts = slice(t * TOKEN_ROWS, (t + 1) * TOKEN_ROWS)
            kt = k * dec_seq + t
            qt = qk[ts]
            s_pool = jnp.where(pool_ok[t:t + 1, :] > 0.5,
                               _dot_nt(qt, kbuf[slot, kt].astype(BF16)) * ATTN_SCALE, NEG)
            s_cur = jnp.where(new_col <= t, _dot_nt(qt, ksn) * ATTN_SCALE, NEG)
            m = jnp.maximum(jnp.max(s_pool, axis=-1, keepdims=True), jnp.max(s_cur, axis=-1, keepdims=True))
            p_pool, p_cur = jnp.exp(s_pool - m), jnp.exp(s_cur - m)
            l = jnp.sum(p_pool, axis=-1, keepdims=True) + jnp.sum(p_cur, axis=-1, keepdims=True)
            o_sel = (_dot(p_pool.astype(BF16), vbuf[slot, kt].astype(BF16)) + _dot(p_cur.astype(BF16), vsn)) / l
            gt = gates[ts]
            o_ref[0, k, ts, :] = (gt[:, 0:1] * oc_ref[0, k, ts, :] + gt[:, 1:2] * o_sel + gt[:, 2:3] * o_win[ts])


def _attn_sample(page_table, idx_flat, q_tm, idx_pad, ksn, vsn, kwn, vwn, k_win, v_win, oc_tm, gates_tm,
                 k_pool, v_pool, dec_seq, past_len):
    nb = q_tm.shape[0]
    rows = dec_seq * TOKEN_ROWS
    n_win = k_win.shape[1]
    spec = lambda r, w: pl.BlockSpec((1, N_KV, r, w), lambda b, *_: (b, 0, 0, 0))
    win = pl.BlockSpec((1, n_win, KV_W), lambda b, *_: (b, 0, 0))
    any_spec = pl.BlockSpec(memory_space=pl.ANY)
    grid_spec = pltpu.PrefetchScalarGridSpec(
        num_scalar_prefetch=2,
        grid=(nb,),
        in_specs=[spec(rows, DH), spec(TOKEN_ROWS, LANES), spec(TOKEN_ROWS, DH), spec(TOKEN_ROWS, DH),
                  spec(TOKEN_ROWS, DH), spec(TOKEN_ROWS, DH), win, win, spec(rows, DH), spec(rows, LANES),
                  any_spec, any_spec],
        out_specs=spec(rows, DH),
        scratch_shapes=[pltpu.VMEM((2, N_KV * dec_seq, TOP_N * SEL_BLOCK, DH), F32),
                        pltpu.VMEM((2, N_KV * dec_seq, TOP_N * SEL_BLOCK, DH), F32),
                        pltpu.SemaphoreType.DMA((2, 2))],
    )
    return pl.pallas_call(
        functools.partial(_attn_sample_body, dec_seq=dec_seq, n_pool_blk=past_len // SEL_BLOCK),
        grid_spec=grid_spec,
        out_shape=jax.ShapeDtypeStruct((nb, N_KV, rows, DH), F32),
        compiler_params=_params("arbitrary"),
        name="attn_sample",
    )(page_table, idx_flat, q_tm, idx_pad, ksn, vsn, kwn, vwn, k_win, v_win, oc_tm, gates_tm, k_pool, v_pool)


def _mix_out_sample_body(oa_ref, u_ref, st_ref, x_ref, cw_ref, cb_ref, lng_ref, lnb_ref, ga_ref, gc_ref,
                         woa_ref, woc_ref, gm_ref, wmq_ref, o_ref, qm_ref, conv_scr, *, nb, dec_seq):
    hist = CONV_W - 1

    def ext(j):
        if j < hist:
            return st_ref[j * nb:(j + 1) * nb, :]
        return u_ref[(j - hist) * nb:(j - hist + 1) * nb, :]

    for t in range(dec_seq):
        acc = jnp.zeros((nb, CONV_CH), F32) + cb_ref[...]
        for k in range(CONV_W):
            acc = acc + cw_ref[k:k + 1, :] * ext(t + k)
        conv_scr[t * nb:(t + 1) * nb, :] = acc
    conv_n = _ln_silu_rms(conv_scr[...], lng_ref, lnb_ref, gc_ref)
    attn_n = (_rms(oa_ref[...]) * ga_ref[...]).astype(BF16)
    x1 = x_ref[...] + _dot(attn_n, woa_ref[...]) + _dot(conv_n, woc_ref[...])
    o_ref[...] = x1
    qm_ref[...] = _dot((_rms(x1) * gm_ref[...]).astype(BF16), wmq_ref[...])


def _mix_out_sample(o_attn, u, conv_state, x, mw, g_mem, w_mq, nb, dec_seq):
    m = x.shape[0]
    args = (o_attn, u, conv_state, x) + tuple(mw) + (g_mem, w_mq)
    return pl.pallas_call(
        functools.partial(_mix_out_sample_body, nb=nb, dec_seq=dec_seq),
        grid=(1,),
        in_specs=[_const_spec(a.shape) for a in args],
        out_specs=[_const_spec((m, D_MODEL)), _const_spec((m, MEM_W))],
        out_shape=[jax.ShapeDtypeStruct((m, D_MODEL), F32), jax.ShapeDtypeStruct((m, MEM_W), F32)],
        scratch_shapes=[pltpu.VMEM((m, CONV_CH), F32)],
        compiler_params=_params("arbitrary"),
        name="mix_out_sample",
    )(*args)


def _mem_attn_sample_body(q_ref, mk_ref, mv_ref, o_ref):
    o_ref[0] = _mem_attn_core(q_ref[0], mk_ref[0].astype(BF16), mv_ref[0].astype(BF16))


def _mem_attn_sample(q_pad, mk, mv):
    nb, rows, _ = q_pad.shape
    n_mem = mk.shape[1]
    q_spec = pl.BlockSpec((1, rows, MEM_W), lambda b: (b, 0, 0))
    mem = pl.BlockSpec((1, n_mem, MEM_W), lambda b: (b, 0, 0))
    return pl.pallas_call(
        _mem_attn_sample_body,
        grid=(nb,),
        in_specs=[q_spec, mem, mem],
        out_specs=q_spec,
        out_shape=jax.ShapeDtypeStruct((nb, rows, MEM_W), BF16),
        compiler_params=_params("arbitrary"),
        name="mem_attn_sample",
    )(q_pad, mk, mv)


def _ffn_sample_body(x_ref, a_ref, wo_ref, g_ref, wg_ref, wu_ref, cw_ref, cb_ref, wd_ref, gf_ref, st_ref,
                     o_ref, sto_ref, x2_scr, h_scr, acc_scr, gate_scr, conv_scr, *, nb, dec_seq):
    f = pl.program_id(0)
    hist = FFN_CONV_W - 1

    @pl.when(f == 0)
    def _():
        x2 = x_ref[...] + _dot(a_ref[...], wo_ref[...])
        x2_scr[...] = x2
        h_scr[...] = (_rms(x2) * g_ref[...]).astype(BF16)
        acc_scr[...] = jnp.zeros(acc_scr.shape, F32)

    hb = h_scr[...]
    gate_scr[...] = _dot(hb, wg_ref[...])

    def ext(j):
        if j < hist:
            return st_ref[j * nb:(j + 1) * nb, :]
        return gate_scr[(j - hist) * nb:(j - hist + 1) * nb, :]

    for t in range(dec_seq):
        acc = cb_ref[...] + cw_ref[0:1, :] * ext(t)
        for k in range(1, FFN_CONV_W):
            acc = acc + cw_ref[k:k + 1, :] * ext(t + k)
        conv_scr[t * nb:(t + 1) * nb, :] = acc
    for j in range(hist):
        sto_ref[j * nb:(j + 1) * nb, :] = ext(dec_seq + j)
    conv = conv_scr[...]
    a = conv * jax.nn.sigmoid(conv) * _dot(hb, wu_ref[...])
    acc_scr[...] += _dot(a.astype(BF16), wd_ref[...])
    _ffn_tail(f, pl.num_programs(0), x2_scr, acc_scr, gf_ref, o_ref)


def _ffn_sample(x1, a, w_mo, fw, ffn_state, nb, dec_seq, tf=512):
    g, wg, wu, cw, cb, wd, gf = fw
    m = x1.shape[0]
    nf = D_FF // tf
    hist = FFN_CONV_W - 1
    col = lambda r: pl.BlockSpec((r, tf), lambda f: (0, f))
    full = lambda shape: pl.BlockSpec(shape, lambda f: (0,) * len(shape))
    return pl.pallas_call(
        functools.partial(_ffn_sample_body, nb=nb, dec_seq=dec_seq),
        grid=(nf,),
        in_specs=[full(x1.shape), full(a.shape), full(w_mo.shape), full(g.shape), col(D_MODEL), col(D_MODEL),
                  col(cw.shape[0]), col(1), pl.BlockSpec((tf, D_MODEL), lambda f: (f, 0)), full(gf.shape),
                  col(nb * hist)],
        out_specs=[full((m, D_MODEL)), col(nb * hist)],
        out_shape=[jax.ShapeDtypeStruct((m, D_MODEL), F32), jax.ShapeDtypeStruct((nb * hist, D_FF), F32)],
        scratch_shapes=[pltpu.VMEM((m, D_MODEL), F32), pltpu.VMEM((m, D_MODEL), BF16), pltpu.VMEM((m, D_MODEL), F32),
                        pltpu.VMEM((m, tf), F32), pltpu.VMEM((m, tf), F32)],
        compiler_params=_params("arbitrary"),
        name="ffn_sample",
    )(x1, a, w_mo, g, wg, wu, cw, cb, wd, gf, ffn_state)


def _prepare_weights(norm_mix_g, w_in, b_gate, cmp_k, cmp_v, conv_w, conv_b, conv_ln_g, conv_ln_b,
                     grp_norm_attn_g, grp_norm_conv_g, w_out, norm_mem_g, mem_norm_g, w_mq, w_mk, w_mv, w_mo,
                     norm_ffn_g, w_ffn_gate, w_ffn_up, ffn_conv_w, ffn_conv_b, w_ffn_down, norm_final_g):
    vec = lambda v: v.reshape(1, -1)
    kv_end = ATTN_W + 6 * KV_W
    n_gate_cols = N_KV * GQA * N_GATE
    per_kv = GQA * N_GATE
    wg = w_in[:, kv_end:kv_end + n_gate_cols].reshape(D_MODEL, N_KV, per_kv)
    wg = jnp.pad(wg, ((0, 0), (0, 0), (0, LANES - per_kv))).reshape(D_MODEL, N_KV * LANES)
    bg = jnp.pad(b_gate.reshape(N_KV, per_kv), ((0, 0), (0, LANES - per_kv))).reshape(1, N_KV * LANES)
    return dict(
        in_proj=(vec(norm_mix_g), w_in[:, :ATTN_W].astype(BF16), w_in[:, ATTN_W:kv_end].astype(BF16),
                 wg.astype(BF16), bg, w_in[:, kv_end + n_gate_cols:].astype(BF16)),
        cmp_k=_compress_weights(*cmp_k),
        cmp_v=_compress_weights(*cmp_v),
        mix=(jnp.pad(conv_w, ((0, 32 - CONV_W), (0, 0))), vec(conv_b), vec(conv_ln_g), vec(conv_ln_b),
             vec(grp_norm_attn_g), vec(grp_norm_conv_g), w_out[:ATTN_W].astype(BF16), w_out[ATTN_W:].astype(BF16)),
        mem_kv=(vec(mem_norm_g), w_mk.astype(BF16), w_mv.astype(BF16)),
        mem=(vec(norm_mem_g), w_mq.astype(BF16), w_mo.astype(BF16)),
        ffn=(vec(norm_ffn_g), w_ffn_gate.astype(BF16), w_ffn_up.astype(BF16),
             jnp.pad(ffn_conv_w, ((0, 8 - FFN_CONV_W), (0, 0))), vec(ffn_conv_b), w_ffn_down.astype(BF16),
             vec(norm_final_g)),
    )


def _prompt_forward(x_prompt, mem_prompt, w):
    batch, t, _ = x_prompt.shape
    x = x_prompt.reshape(batch * t, D_MODEL)
    (q, kc, vc, ks, vs, kw, vw, ksb, vsb, kwb, vwb, gates, u) = _in_proj(x, *w["in_proj"], tm=256)
    kcc = _compress_prompt(kc, w["cmp_k"], batch)
    vcc = _compress_prompt(vc, w["cmp_v"], batch)
    gates_t = gates.reshape(batch * t, N_KV, LANES)[:, :, :GATE_ROWS].transpose(1, 2, 0)
    o_attn = _nsa_prompt(q.T, kcc, vcc.transpose(0, 1, 3, 2), ksb, vsb.T, kwb, vwb.T,
                         gates_t.reshape(N_KV * GATE_ROWS, batch * t), batch)
    x1 = _mix_out_prompt(o_attn, u, x, w["mix"], batch)
    n_mem = mem_prompt.shape[1]
    mk, mv = _mem_kv(mem_prompt.reshape(batch * n_mem, D_MODEL), *w["mem_kv"])
    g_mem, w_mq, w_mo = w["mem"]
    x2 = _mem_attn_prompt(x1, g_mem, w_mq, mk.reshape(batch, n_mem, MEM_W), mv.reshape(batch, n_mem, MEM_W),
                          w_mo, batch)
    y, ffn_tail = _ffn_prompt(x2, *w["ffn"], batch=batch)
    kv5 = lambda a: a.reshape(1, batch, t, N_KV, DH)
    win = lambda a: a.reshape(batch, t, N_KV, DH)[None, :, t - min(WINDOW, t):]
    tiles = ffn_tail.shape[0] // batch
    new_ffn = ffn_tail.reshape(batch, tiles, ffn_tail.shape[1], D_FF)[:, -1, -(FFN_CONV_W - 1):]
    new_conv = u.reshape(batch, t, CONV_CH)[:, t - (CONV_W - 1):]
    mem5 = lambda a: a.reshape(1, batch, n_mem, MEM_HEADS, MEM_DH)
    return (y.reshape(batch, t, D_MODEL), kv5(kc), kv5(vc), kv5(ks), kv5(vs), win(kw), win(vw),
            new_conv[None], new_ffn[None], mem5(mk), mem5(mv))


def _pad_axis(a, axis, size):
    pads = [(0, 0)] * a.ndim
    pads[axis] = (0, size - a.shape[axis])
    return jnp.pad(a, pads)


def _sample_forward(x_sample, pools, k_win, v_win, conv_state, ffn_state, mem_k, mem_v, page_table, w):
    nb, dec_seq, _ = x_sample.shape
    m = nb * dec_seq
    past_len = page_table.shape[1] * PAGE
    assert dec_seq <= HEAD_ROWS and k_win.shape[1] == WINDOW
    x = x_sample.reshape(m, D_MODEL)
    (q, kc, vc, ks, vs, kw, vw, ksb, vsb, kwb, vwb, gates, u) = _in_proj(x, *w["in_proj"], tm=m)
    pool_kc, pool_vc, pool_ks, pool_vs = pools
    kcc = _compress_sample(pool_kc, page_table, w["cmp_k"])
    vcc = _compress_sample(pool_vc, page_table, w["cmp_v"])

    q5 = q.reshape(nb, dec_seq, N_KV, GQA, DH)
    q_hm = _pad_axis(q5.transpose(0, 2, 3, 1, 4), 3, HEAD_ROWS).reshape(nb, N_KV, GQA * HEAD_ROWS, DH)
    q_tm = _pad_axis(q5.transpose(0, 2, 1, 3, 4), 3, TOKEN_ROWS).reshape(nb, N_KV, dec_seq * TOKEN_ROWS, DH)
    oc_hm, idx = _cmp_select_sample(q_hm, kcc, vcc, past_len)
    oc_tm = oc_hm.reshape(nb, N_KV, GQA, HEAD_ROWS, DH)[:, :, :, :dec_seq].transpose(0, 1, 3, 2, 4)
    oc_tm = _pad_axis(oc_tm, 3, TOKEN_ROWS).reshape(nb, N_KV, dec_seq * TOKEN_ROWS, DH)
    idx_flat = idx[:, :, :dec_seq, :TOP_N].reshape(nb, N_KV * dec_seq * TOP_N)
    idx_pad = _pad_axis(idx, 2, TOKEN_ROWS)
    new_rows = lambda a: _pad_axis(a.reshape(nb, dec_seq, N_KV, DH).transpose(0, 2, 1, 3), 2, TOKEN_ROWS)
    gates_tm = gates.reshape(nb, dec_seq, N_KV, LANES)[..., :GQA * N_GATE].reshape(nb, dec_seq, N_KV, GQA, N_GATE)
    gates_tm = _pad_axis(_pad_axis(gates_tm.transpose(0, 2, 1, 3, 4), 3, TOKEN_ROWS), 4, LANES)
    gates_tm = gates_tm.reshape(nb, N_KV, dec_seq * TOKEN_ROWS, LANES)
    o_tm = _attn_sample(page_table, idx_flat, q_tm, idx_pad, new_rows(ksb), new_rows(vsb), new_rows(kwb),
                        new_rows(vwb), k_win.reshape(nb, WINDOW, KV_W), v_win.reshape(nb, WINDOW, KV_W),
                        oc_tm, gates_tm, pool_ks, pool_vs, dec_seq, past_len)
    o_attn = o_tm.reshape(nb, N_KV, dec_seq, TOKEN_ROWS, DH)[:, :, :, :GQA].transpose(2, 0, 1, 3, 4).reshape(m, ATTN_W)
    step_major = lambda a: a.reshape(nb, -1, a.shape[-1]).transpose(1, 0, 2).reshape(-1, a.shape[-1])
    batch_major = lambda a: a.reshape(-1, nb, a.shape[-1]).transpose(1, 0, 2)

    g_mem, w_mq, w_mo = w["mem"]
    x1, qm = _mix_out_sample(o_attn, step_major(u), step_major(conv_state), step_major(x), w["mix"], g_mem, w_mq,
                             nb, dec_seq)
    n_mem = mem_k.shape[1]
    q_pad = _pad_axis(batch_major(qm), 1, TOKEN_ROWS).astype(BF16)
    a = _mem_attn_sample(q_pad, mem_k.reshape(nb, n_mem, MEM_W), mem_v.reshape(nb, n_mem, MEM_W))
    y, new_ffn = _ffn_sample(x1, step_major(a[:, :dec_seq]), w_mo, w["ffn"], step_major(ffn_state), nb, dec_seq)

    kv5 = lambda a: a.reshape(1, nb, dec_seq, N_KV, DH)
    shift = lambda buf, new: jnp.concatenate([buf[:, dec_seq:], new.reshape((nb, dec_seq) + buf.shape[2:])], axis=1)[None]
    return (batch_major(y), kv5(kc), kv5(vc), kv5(ks), kv5(vs), shift(k_win, kw), shift(v_win, vw),
            shift(conv_state, u), batch_major(new_ffn)[None])


def kernel(x_prompt, x_sample, cache_k_cmp, cache_v_cmp, cache_k_sel, cache_v_sel, cache_k_win, cache_v_win,
           state_conv, state_ffn_conv, cache_mem_k, cache_mem_v, page_table, mem_prompt,
           norm_mix_g, w_in, b_gate, cmp_k_pe, cmp_k_w1, cmp_k_b1, cmp_k_w2, cmp_v_pe, cmp_v_w1, cmp_v_b1, cmp_v_w2,
           conv_w, conv_b, conv_ln_g, conv_ln_b, grp_norm_attn_g, grp_norm_conv_g, w_out,
           norm_mem_g, mem_norm_g, w_mq, w_mk, w_mv, w_mo,
           norm_ffn_g, w_ffn_gate, w_ffn_up, ffn_conv_w, ffn_conv_b, w_ffn_down, norm_final_g):
    assert w_in.shape[0] == 1, "single-layer step"
    w = _prepare_weights(norm_mix_g[0], w_in[0], b_gate[0],
                         (cmp_k_pe[0], cmp_k_w1[0], cmp_k_b1[0], cmp_k_w2[0]),
                         (cmp_v_pe[0], cmp_v_w1[0], cmp_v_b1[0], cmp_v_w2[0]),
                         conv_w[0], conv_b[0], conv_ln_g[0], conv_ln_b[0], grp_norm_attn_g[0], grp_norm_conv_g[0],
                         w_out[0], norm_mem_g[0], mem_norm_g[0], w_mq[0], w_mk[0], w_mv[0], w_mo[0],
                         norm_ffn_g[0], w_ffn_gate[0], w_ffn_up[0], ffn_conv_w[0], ffn_conv_b[0], w_ffn_down[0],
                         norm_final_g)
    p = _prompt_forward(x_prompt, mem_prompt, w)
    s = _sample_forward(x_sample, (cache_k_cmp[0], cache_v_cmp[0], cache_k_sel[0], cache_v_sel[0]),
                        cache_k_win[0], cache_v_win[0], state_conv[0], state_ffn_conv[0],
                        cache_mem_k[0], cache_mem_v[0], page_table, w)
    return (p[0], s[0]) + p[1:] + s[1:]
```

```python
import functools

import jax
import jax.numpy as jnp
from jax import lax
from jax.experimental import pallas as pl
from jax.experimental.pallas import tpu as pltpu

F32 = jnp.float32
BF16 = jnp.bfloat16
I32 = jnp.int32

D_MODEL = 2048
N_KV = 2
GQA = 4
DH = 128
ATTN_W = N_KV * GQA * DH
KV_W = N_KV * DH
N_GATE = 3
BLOCK_CMP = 32
STRIDE_CMP = 16
CMP_HID = 256
SEL_BLOCK = 64
TOP_N = 16
WINDOW = 512
CONV_CH = D_MODEL - ATTN_W
CONV_W = 31
D_FF = 5632
FFN_CONV_W = 3
MEM_HEADS = 4
MEM_DH = 128
MEM_W = MEM_HEADS * MEM_DH
ATTN_SCALE = DH ** -0.5
Q_PRESCALE = ATTN_SCALE * 1.4426950408889634
MEM_SCALE = MEM_DH ** -0.5
EPS = 1e-6
NEG = -1e30
BIG = 1e30
LANES = 128
VMEM_LIMIT = 56 * 1024 * 1024


def _dot(a, b):
    return jnp.dot(a, b, preferred_element_type=F32)


def _dot_nt(a, b):
    return lax.dot_general(a, b, (((1,), (1,)), ((), ())), preferred_element_type=F32)


def _rms(x):
    return x * lax.rsqrt(jnp.mean(x * x, axis=-1, keepdims=True) + EPS)


def _const_spec(shape):
    return pl.BlockSpec(shape, lambda *_: (0,) * len(shape), pipeline_mode=pl.Buffered(1))


def _params(*sem):
    return pltpu.CompilerParams(dimension_semantics=sem, vmem_limit_bytes=VMEM_LIMIT)


def _in_proj_body(x_ref, g_ref, wq_ref, wkv_ref, wg_ref, bg_ref, wglu_ref,
                  q_ref, kc_ref, vc_ref, ks_ref, vs_ref, kw_ref, vw_ref,
                  ksb_ref, vsb_ref, kwb_ref, vwb_ref, gates_ref, u_ref):
    hb = (_rms(x_ref[...]) * g_ref[...]).astype(BF16)
    half = ATTN_W // 2
    for c in range(2):
        q_ref[:, c * half:(c + 1) * half] = (_dot(hb, wq_ref[:, c * half:(c + 1) * half]) * Q_PRESCALE).astype(BF16)
    f32_outs = (kc_ref, vc_ref, ks_ref, vs_ref, kw_ref, vw_ref)
    bf_outs = (None, None, ksb_ref, vsb_ref, kwb_ref, vwb_ref)
    for c in range(6):
        r = _dot(hb, wkv_ref[:, c * KV_W:(c + 1) * KV_W])
        f32_outs[c][...] = r
        if bf_outs[c] is not None:
            bf_outs[c][...] = r.astype(BF16)
    gates_ref[...] = jax.nn.sigmoid(_dot(hb, wg_ref[...]) + bg_ref[...])
    cw = 256
    for c in range(CONV_CH // cw):
        a = _dot(hb, wglu_ref[:, c * cw:(c + 1) * cw])
        gt = _dot(hb, wglu_ref[:, CONV_CH + c * cw:CONV_CH + (c + 1) * cw])
        u_ref[:, c * cw:(c + 1) * cw] = a * jax.nn.sigmoid(gt)


def _in_proj(x, g, wq, wkv, wg, bg, wglu, tm):
    m = x.shape[0]
    row = lambda w: pl.BlockSpec((tm, w), lambda i: (i, 0))
    out_shape = ([jax.ShapeDtypeStruct((m, ATTN_W), BF16)]
                 + [jax.ShapeDtypeStruct((m, KV_W), F32)] * 6
                 + [jax.ShapeDtypeStruct((m, KV_W), BF16)] * 4
                 + [jax.ShapeDtypeStruct((m, N_KV * LANES), F32),
                    jax.ShapeDtypeStruct((m, CONV_CH), F32)])
    out_specs = ([row(ATTN_W)] + [row(KV_W)] * 10 + [row(N_KV * LANES), row(CONV_CH)])
    return pl.pallas_call(
        _in_proj_body,
        grid=(m // tm,),
        in_specs=[row(D_MODEL), _const_spec(g.shape), _const_spec(wq.shape), _const_spec(wkv.shape),
                  _const_spec(wg.shape), _const_spec(bg.shape), _const_spec(wglu.shape)],
        out_specs=out_specs,
        out_shape=out_shape,
        compiler_params=_params("arbitrary"),
        name="in_proj",
    )(x, g, wq, wkv, wg, bg, wglu)


def _compress_rows(get_lanes, n, pe_ref, w1l_ref, w1t_ref, b1_ref, w2_ref):
    xk = jnp.concatenate([get_lanes(l) for l in range(STRIDE_CMP)], axis=1)
    lead = _dot((xk + pe_ref[0:1, :]).astype(BF16), w1l_ref[...])
    trail = _dot((xk + pe_ref[1:2, :]).astype(BF16), w1t_ref[...])
    trail_next = pltpu.roll(trail, n - 1, axis=0)
    hid = jax.nn.gelu(lead + trail_next + b1_ref[...])
    out = _dot(hid.astype(BF16), w2_ref[...])
    rows = lax.broadcasted_iota(I32, (n, 1), 0)
    return jnp.where(rows < n - 1, out, 0.0)


def _compress_prompt_body(x_ref, pe_ref, w1l_ref, w1t_ref, b1_ref, w2_ref, o_ref):
    n = x_ref.shape[1]
    for k in range(N_KV):
        get = lambda l, k=k: x_ref[0, :, l * KV_W + k * DH:l * KV_W + (k + 1) * DH]
        o_ref[0, k] = _compress_rows(get, n, pe_ref, w1l_ref, w1t_ref, b1_ref, w2_ref)


def _compress_weights(pe, w1, b1, w2):
    half = STRIDE_CMP * DH
    pe2 = pe.reshape(2, half)
    return pe2, w1[:half].astype(BF16), w1[half:].astype(BF16), b1.reshape(1, CMP_HID), w2.astype(BF16)


def _compress_prompt(rows, cw, batch):
    n = rows.shape[0] // batch // STRIDE_CMP
    x = rows.reshape(batch, n, STRIDE_CMP * KV_W)
    return pl.pallas_call(
        _compress_prompt_body,
        grid=(batch,),
        in_specs=[pl.BlockSpec((1, n, STRIDE_CMP * KV_W), lambda b: (b, 0, 0))] + [_const_spec(w.shape) for w in cw],
        out_specs=pl.BlockSpec((1, N_KV, n, DH), lambda b: (b, 0, 0, 0)),
        out_shape=jax.ShapeDtypeStruct((batch, N_KV, n, DH), F32),
        compiler_params=_params("arbitrary"),
        name="compress_prompt",
    )(x, *cw)


def _overlap_matrix(n_cmp, n_blk=LANES):
    i = lax.broadcasted_iota(I32, (n_cmp, n_blk), 0) * STRIDE_CMP
    j = lax.broadcasted_iota(I32, (n_cmp, n_blk), 1) * SEL_BLOCK
    ov = jnp.maximum(jnp.minimum(i + BLOCK_CMP, j + SEL_BLOCK) - jnp.maximum(i, j), 0)
    return (ov.astype(F32) * (1.0 / BLOCK_CMP)).astype(BF16)


def _importance(p_sum, ov):
    hi = p_sum.astype(BF16)
    r1 = p_sum - hi.astype(F32)
    mid = r1.astype(BF16)
    lo = (r1 - mid.astype(F32)).astype(BF16)
    return _dot(hi, ov) + _dot(mid, ov) + _dot(lo, ov)


def _softmax_av(s, v):
    p = jnp.exp(s - jnp.max(s, axis=-1, keepdims=True))
    return _dot(p.astype(BF16), v) / jnp.sum(p, axis=-1, keepdims=True)


GATE_ROWS = 16


def _nsa_prompt_body(qt_ref, kcc_ref, vcct_ref, ks_ref, vst_ref, kw_ref, vwt_ref, gt_ref, o_ref,
                     acc_scr, bias_scr, *, tq, tk, n_sel):
    i = pl.program_id(2)
    t0 = i * tq
    nq = GQA * tq
    heads = lambda a: jnp.concatenate([a] * GQA, axis=1)
    q_pos = t0 + lax.broadcasted_iota(I32, (1, tq), 1)
    q_pos4 = heads(q_pos)
    qt = jnp.concatenate([qt_ref[g * DH:(g + 1) * DH, :] for g in range(GQA)], axis=1)

    n_cmp = kcc_ref.shape[2]
    cmp_end = lax.broadcasted_iota(I32, (n_cmp, 1), 0) * STRIDE_CMP + (BLOCK_CMP - 1)
    cvis = cmp_end <= q_pos4
    s = jnp.where(cvis, _dot(kcc_ref[0, 0].astype(BF16), qt), NEG)
    e = jnp.where(cvis, jnp.exp2(s - jnp.max(s, axis=0, keepdims=True)), 0.0)
    p = e / jnp.maximum(jnp.sum(e, axis=0, keepdims=True), 1e-30)
    o_cmp = _dot(vcct_ref[0, 0].astype(BF16), p.astype(BF16))
    p_sum = p[:, 0:tq]
    for g in range(1, GQA):
        p_sum = p_sum + p[:, g * tq:(g + 1) * tq]

    ov_i = lax.broadcasted_iota(I32, (n_sel, n_cmp), 1) * STRIDE_CMP
    ov_j = lax.broadcasted_iota(I32, (n_sel, n_cmp), 0) * SEL_BLOCK
    ov = jnp.maximum(jnp.minimum(ov_i + BLOCK_CMP, ov_j + SEL_BLOCK) - jnp.maximum(ov_i, ov_j), 0)
    ov = (ov.astype(F32) * (1.0 / BLOCK_CMP)).astype(BF16)
    hi = p_sum.astype(BF16)
    r1 = p_sum - hi.astype(F32)
    mid = r1.astype(BF16)
    lo = (r1 - mid.astype(F32)).astype(BF16)
    imp = _dot(ov, hi) + _dot(ov, mid) + _dot(ov, lo)
    blk = lax.broadcasted_iota(I32, (n_sel, tq), 0)
    cur = q_pos // SEL_BLOCK
    forced = (blk == 0) | (blk == cur) | (blk == cur - 1)
    s_t = jnp.where(blk * SEL_BLOCK <= q_pos, jnp.where(forced, BIG, imp), -BIG)
    rank = jnp.zeros((n_sel, tq), F32)
    for ib in range(n_sel):
        row = s_t[ib:ib + 1, :]
        beats = (row > s_t) | ((row == s_t) & (blk > ib))
        rank = rank + jnp.where(beats, 1.0, 0.0)
    bias_scr[...] = jnp.where(rank < TOP_N, 0.0, NEG)

    acc_scr[...] = jnp.zeros(acc_scr.shape, F32)
    blocks_per_tile = tk // SEL_BLOCK
    last_tile = ks_ref.shape[0] // tk - 1

    def key_tile(j, state, m, l, causal):
        jd = jnp.minimum(j, last_tile)
        k0 = pl.multiple_of(jd * tk, tk)
        bias = jnp.concatenate(
            [jnp.broadcast_to(bias_scr[pl.ds(jd * blocks_per_tile + c, 1), :], (SEL_BLOCK, tq))
             for c in range(blocks_per_tile)], axis=0)
        if causal:
            bias = jnp.where(j * tk + lax.broadcasted_iota(I32, (tk, 1), 0) <= q_pos, bias, NEG)
        s = _dot(ks_ref[pl.ds(k0, tk), :], qt) + heads(bias)
        m_new = jnp.maximum(m, jnp.max(s, axis=0, keepdims=True))
        alpha = jnp.exp2(m - m_new)
        p = jnp.exp2(s - m_new)
        acc_scr[state] = alpha * acc_scr[state] + _dot(vst_ref[:, pl.ds(k0, tk)], p.astype(BF16))
        return m_new, alpha * l + jnp.sum(p, axis=0, keepdims=True)

    def tile_pair(jp, c):
        m0, l0 = key_tile(2 * jp, 0, c[0], c[1], False)
        m1, l1 = key_tile(2 * jp + 1, 1, c[2], c[3], False)
        return m0, l0, m1, l1

    n_pairs = t0 // (2 * tk)
    empty_m, empty_l = jnp.full((1, nq), NEG, F32), jnp.zeros((1, nq), F32)
    c = lax.fori_loop(0, n_pairs, tile_pair, (empty_m, empty_l, empty_m, empty_l))
    m0, l0 = key_tile(2 * n_pairs, 0, c[0], c[1], True)
    m1, l1 = key_tile(2 * n_pairs + 1, 1, c[2], c[3], True)
    m = jnp.maximum(m0, m1)
    w0, w1 = jnp.exp2(m0 - m), jnp.exp2(m1 - m)
    o_sel = (w0 * acc_scr[0] + w1 * acc_scr[1]) / (w0 * l0 + w1 * l1)

    span = WINDOW + tq
    w0 = pl.multiple_of(jnp.maximum(t0 - WINDOW, 0), tq)
    kw_pos = w0 + lax.broadcasted_iota(I32, (span, 1), 0)
    wvis = (kw_pos <= q_pos4) & (q_pos4 - kw_pos < WINDOW)
    s = jnp.where(wvis, _dot(kw_ref[pl.ds(w0, span), :], qt), NEG)
    p = jnp.exp2(s - jnp.max(s, axis=0, keepdims=True))
    o_win = _dot(vwt_ref[:, pl.ds(w0, span)], p.astype(BF16)) / jnp.sum(p, axis=0, keepdims=True)

    gt = gt_ref[...]
    gate = lambda c: jnp.concatenate([gt[g * N_GATE + c:g * N_GATE + c + 1, :] for g in range(GQA)], axis=1)
    o = gate(0) * o_cmp + gate(1) * o_sel + gate(2) * o_win
    for g in range(GQA):
        o_ref[:, g * DH:(g + 1) * DH] = o[:, g * tq:(g + 1) * tq].T


def _nsa_prompt(qt, kcc, vcct, ksb, vst, kwb, vwt, gates_t, batch, tq=128, tk=512):
    m = qt.shape[1]
    t = m // batch
    tk = min(tk, t)
    nt = t // tq
    n_cmp = kcc.shape[2]
    n_sel = max(t // SEL_BLOCK, 8)
    assert t % tk == 0 and tk % tq == 0 and t >= WINDOW + tq and tq % LANES == 0
    rows = lambda: pl.BlockSpec((t, DH), lambda b, k, i: (b, k))
    cols = lambda: pl.BlockSpec((DH, t), lambda b, k, i: (k, b))
    return pl.pallas_call(
        functools.partial(_nsa_prompt_body, tq=tq, tk=tk, n_sel=n_sel),
        grid=(batch, N_KV, nt),
        in_specs=[pl.BlockSpec((GQA * DH, tq), lambda b, k, i: (k, b * nt + i)),
                  pl.BlockSpec((1, 1, n_cmp, DH), lambda b, k, i: (b, k, 0, 0)),
                  pl.BlockSpec((1, 1, DH, n_cmp), lambda b, k, i: (b, k, 0, 0)),
                  rows(), cols(), rows(), cols(),
                  pl.BlockSpec((GATE_ROWS, tq), lambda b, k, i: (k, b * nt + i))],
        out_specs=pl.BlockSpec((tq, GQA * DH), lambda b, k, i: (b * nt + i, k)),
        out_shape=jax.ShapeDtypeStruct((m, ATTN_W), F32),
        scratch_shapes=[pltpu.VMEM((2, DH, GQA * tq), F32), pltpu.VMEM((n_sel, tq), F32)],
        compiler_params=_params("arbitrary", "arbitrary", "arbitrary"),
        name="nsa_prompt",
    )(qt, kcc, vcct, ksb, vst, kwb, vwt, gates_t)


def _ln_silu_rms(y, lng_ref, lnb_ref, gn_ref):
    mu = jnp.mean(y, axis=-1, keepdims=True)
    var = jnp.mean(jnp.square(y - mu), axis=-1, keepdims=True)
    y = (y - mu) * lax.rsqrt(var + EPS) * lng_ref[...] + lnb_ref[...]
    y = y * jax.nn.sigmoid(y)
    return (_rms(y) * gn_ref[...]).astype(BF16)


def _mix_out_prompt_body(oa_ref, u_ref, halo_ref, x_ref, cw_ref, cb_ref, lng_ref, lnb_ref, ga_ref, gc_ref,
                         woa_ref, woc_ref, o_ref, ext_scr, conv_scr, *, tm, tiles_per_seq, rc):
    i = pl.program_id(0)
    pad = halo_ref.shape[0]
    first = (i % tiles_per_seq) == 0
    ext_scr[0:pad, :] = jnp.where(first, 0.0, halo_ref[...])
    ext_scr[pad:, :] = u_ref[...]
    off = pad - (CONV_W - 1)
    span = rc + pad

    def chunk(r, carry):
        base = pl.multiple_of(r * rc, rc)
        window = ext_scr[pl.ds(base, span), :]
        acc = jnp.zeros((rc, CONV_CH), F32) + cb_ref[...]
        for res in range(8):
            shifted = window if res == 0 else pltpu.roll(window, span - res, axis=0)
            for k in range(CONV_W):
                if (k + off) % 8 == res:
                    a8 = k + off - res
                    acc = acc + cw_ref[k:k + 1, :] * shifted[a8:a8 + rc]
        conv_scr[pl.ds(base, rc), :] = acc
        return carry

    lax.fori_loop(0, tm // rc, chunk, 0)
    conv_n = _ln_silu_rms(conv_scr[...], lng_ref, lnb_ref, gc_ref)
    attn_n = (_rms(oa_ref[...]) * ga_ref[...]).astype(BF16)
    half = D_MODEL // 2
    for c in range(2):
        cs = slice(c * half, (c + 1) * half)
        o_ref[:, cs] = x_ref[:, cs] + _dot(attn_n, woa_ref[:, cs]) + _dot(conv_n, woc_ref[:, cs])


def _mix_out_prompt(o_attn, u, x, mw, batch, tm=256, rc=32):
    m = x.shape[0]
    t = m // batch
    tm = min(tm, t)
    pad = 32
    row = lambda w: pl.BlockSpec((tm, w), lambda i: (i, 0))
    halo = pl.BlockSpec((pad, CONV_CH), lambda i: (jnp.maximum(i * (tm // pad) - 1, 0), 0))
    return pl.pallas_call(
        functools.partial(_mix_out_prompt_body, tm=tm, tiles_per_seq=t // tm, rc=rc),
        grid=(m // tm,),
        in_specs=[row(ATTN_W), row(CONV_CH), halo, row(D_MODEL)] + [_const_spec(w.shape) for w in mw],
        out_specs=row(D_MODEL),
        out_shape=jax.ShapeDtypeStruct((m, D_MODEL), F32),
        scratch_shapes=[pltpu.VMEM((tm + pad, CONV_CH), F32), pltpu.VMEM((tm, CONV_CH), F32)],
        compiler_params=_params("arbitrary"),
        name="mix_out_prompt",
    )(o_attn, u, u, x, *mw)


def _mem_kv_body(mem_ref, g_ref, wk_ref, wv_ref, k_ref, v_ref):
    mb = (_rms(mem_ref[...]) * g_ref[...]).astype(BF16)
    k_ref[...] = _dot(mb, wk_ref[...])
    v_ref[...] = _dot(mb, wv_ref[...])


def _mem_kv(mem, g, wk, wv, tm=256):
    m = mem.shape[0]
    row = lambda w: pl.BlockSpec((tm, w), lambda i: (i, 0))
    return pl.pallas_call(
        _mem_kv_body,
        grid=(m // tm,),
        in_specs=[row(D_MODEL), _const_spec(g.shape), _const_spec(wk.shape), _const_spec(wv.shape)],
        out_specs=[row(MEM_W), row(MEM_W)],
        out_shape=[jax.ShapeDtypeStruct((m, MEM_W), F32)] * 2,
        compiler_params=_params("arbitrary"),
        name="mem_kv",
    )(mem, g, wk, wv)


def _mem_attn_core(q, mk, mv):
    outs = []
    for h in range(MEM_HEADS):
        hs = slice(h * MEM_DH, (h + 1) * MEM_DH)
        s = _dot_nt(q[:, hs].astype(BF16), mk[:, hs]) * MEM_SCALE
        outs.append(_softmax_av(s, mv[:, hs]))
    return jnp.concatenate(outs, axis=1).astype(BF16)


def _mem_attn_prompt_body(x_ref, g_ref, wq_ref, mk_ref, mv_ref, wo_ref, o_ref):
    x = x_ref[...]
    hb = (_rms(x) * g_ref[...]).astype(BF16)
    a = _mem_attn_core(_dot(hb, wq_ref[...]), mk_ref[0].astype(BF16), mv_ref[0].astype(BF16))
    o_ref[...] = x + _dot(a, wo_ref[...])


def _mem_attn_prompt(x, g, wq, mk, mv, wo, batch, tm=256):
    m = x.shape[0]
    tiles_per_seq = m // batch // tm
    n_mem = mk.shape[1]
    row = pl.BlockSpec((tm, D_MODEL), lambda i: (i, 0))
    mem = pl.BlockSpec((1, n_mem, MEM_W), lambda i: (i // tiles_per_seq, 0, 0))
    return pl.pallas_call(
        _mem_attn_prompt_body,
        grid=(m // tm,),
        in_specs=[row, _const_spec(g.shape), _const_spec(wq.shape), mem, mem, _const_spec(wo.shape)],
        out_specs=row,
        out_shape=jax.ShapeDtypeStruct((m, D_MODEL), F32),
        compiler_params=_params("arbitrary"),
        name="mem_attn_prompt",
    )(x, g, wq, mk, mv, wo)


def _ffn_tail(f, nf, x_ref, acc_scr, gf_ref, o_ref):
    @pl.when(f == nf - 1)
    def _():
        o_ref[...] = _rms(x_ref[...] + acc_scr[...]) * gf_ref[...]


def _ffn_prompt_body(x_ref, g_ref, wg_ref, wu_ref, cw_ref, cb_ref, wd_ref, gf_ref, o_ref, st_ref,
                     h_scr, acc_scr, carry_scr, gext_scr, *, tm, tiles_per_seq):
    i = pl.program_id(0)
    f = pl.program_id(1)
    nf = pl.num_programs(1)
    hist = carry_scr.shape[1]

    @pl.when(f == 0)
    def _():
        h_scr[...] = (_rms(x_ref[...]) * g_ref[...]).astype(BF16)
        acc_scr[...] = jnp.zeros(acc_scr.shape, F32)

    hb = h_scr[...]
    gate = _dot(hb, wg_ref[...])
    first = (i % tiles_per_seq) == 0
    gext_scr[0:hist, :] = jnp.where(first, 0.0, carry_scr[f])
    gext_scr[hist:, :] = gate
    tail = gate[tm - hist:, :]
    carry_scr[f] = tail
    st_ref[0] = tail
    conv = (cw_ref[0:1, :] * gext_scr[hist - 2:hist - 2 + tm, :] + cw_ref[1:2, :] * gext_scr[hist - 1:hist - 1 + tm, :]
            + cw_ref[2:3, :] * gate + cb_ref[...])
    a = conv * jax.nn.sigmoid(conv) * _dot(hb, wu_ref[...])
    acc_scr[...] += _dot(a.astype(BF16), wd_ref[...])
    _ffn_tail(f, nf, x_ref, acc_scr, gf_ref, o_ref)


def _ffn_prompt(x, g, wg, wu, cw, cb, wd, gf, batch, tm=512, tf=512):
    m = x.shape[0]
    t = m // batch
    tm = min(tm, t)
    nf = D_FF // tf
    hist = 8
    row = pl.BlockSpec((tm, D_MODEL), lambda i, f: (i, 0))
    col = lambda r: pl.BlockSpec((r, tf), lambda i, f: (0, f))
    return pl.pallas_call(
        functools.partial(_ffn_prompt_body, tm=tm, tiles_per_seq=t // tm),
        grid=(m // tm, nf),
        in_specs=[row, _const_spec(g.shape), col(D_MODEL), col(D_MODEL), col(cw.shape[0]), col(1),
                  pl.BlockSpec((tf, D_MODEL), lambda i, f: (f, 0)), _const_spec(gf.shape)],
        out_specs=[row, pl.BlockSpec((1, hist, tf), lambda i, f: (i, 0, f))],
        out_shape=[jax.ShapeDtypeStruct((m, D_MODEL), F32), jax.ShapeDtypeStruct((m // tm, hist, D_FF), F32)],
        scratch_shapes=[pltpu.VMEM((tm, D_MODEL), BF16), pltpu.VMEM((tm, D_MODEL), F32),
                        pltpu.VMEM((nf, hist, tf), F32), pltpu.VMEM((tm + hist, tf), F32)],
        compiler_params=_params("arbitrary", "arbitrary"),
        name="ffn_prompt",
    )(x, g, wg, wu, cw, cb, wd, gf)


PAGE = 128
CHUNKS_PER_PAGE = PAGE // STRIDE_CMP


def _compress_sample_body(pt_ref, pool_ref, pe_ref, w1l_ref, w1t_ref, b1_ref, w2_ref, o_ref, buf, sem, *, n_pages):
    b = pl.program_id(0)
    slot = b % 2

    def pages(bb, sl, start):
        def one(j, carry):
            page = pt_ref[bb, j]
            for k in range(N_KV):
                cp = pltpu.make_async_copy(pool_ref.at[page, :, k, :],
                                           buf.at[sl, k, pl.ds(pl.multiple_of(j * PAGE, PAGE), PAGE), :], sem.at[sl])
                cp.start() if start else cp.wait()
            return carry
        lax.fori_loop(0, n_pages, one, 0)

    @pl.when(b == 0)
    def _():
        pages(0, 0, True)

    @pl.when(b + 1 < pl.num_programs(0))
    def _():
        pages(b + 1, 1 - slot, True)

    pages(b, slot, False)
    n = n_pages * CHUNKS_PER_PAGE
    for k in range(N_KV):
        get = lambda l, k=k: buf[slot, k, pl.ds(l, n, stride=STRIDE_CMP), :]
        o_ref[0, k] = _compress_rows(get, n, pe_ref, w1l_ref, w1t_ref, b1_ref, w2_ref)


def _compress_sample(pool, page_table, cw):
    nb, n_pages = page_table.shape
    n = n_pages * CHUNKS_PER_PAGE
    grid_spec = pltpu.PrefetchScalarGridSpec(
        num_scalar_prefetch=1,
        grid=(nb,),
        in_specs=[pl.BlockSpec(memory_space=pl.ANY)] + [_const_spec(w.shape) for w in cw],
        out_specs=pl.BlockSpec((1, N_KV, n, DH), lambda b, pt: (b, 0, 0, 0)),
        scratch_shapes=[pltpu.VMEM((2, N_KV, n_pages * PAGE, DH), F32), pltpu.SemaphoreType.DMA((2,))],
    )
    return pl.pallas_call(
        functools.partial(_compress_sample_body, n_pages=n_pages),
        grid_spec=grid_spec,
        out_shape=jax.ShapeDtypeStruct((nb, N_KV, n, DH), F32),
        compiler_params=_params("arbitrary"),
        name="compress_sample",
    )(page_table, pool, *cw)


HEAD_ROWS = 8
TOKEN_ROWS = 16


def _cmp_select_sample_body(q_ref, kcc_ref, vcc_ref, oc_ref, idx_ref, *, past_len, n_blk):
    bb = q_ref.shape[0]
    n_cmp = kcc_ref.shape[2]
    rows = GQA * HEAD_ROWS
    q_pos = past_len + lax.broadcasted_iota(I32, (rows, 1), 0) % HEAD_ROWS
    cmp_end = lax.broadcasted_iota(I32, (1, n_cmp), 1) * STRIDE_CMP + (BLOCK_CMP - 1)
    cmask = cmp_end <= q_pos
    p_sums = []
    for bi in range(bb):
        for k in range(N_KV):
            s = jnp.where(cmask, _dot_nt(q_ref[bi, k], kcc_ref[bi, k].astype(BF16)), NEG)
            e = jnp.where(cmask, jnp.exp2(s - jnp.max(s, axis=-1, keepdims=True)), 0.0)
            p = e / jnp.maximum(jnp.sum(e, axis=-1, keepdims=True), 1e-30)
            oc_ref[bi, k] = _dot(p.astype(BF16), vcc_ref[bi, k].astype(BF16))
            p_sum = p[0:HEAD_ROWS]
            for g in range(1, GQA):
                p_sum = p_sum + p[g * HEAD_ROWS:(g + 1) * HEAD_ROWS]
            p_sums.append(p_sum)
    n_rows = bb * N_KV * HEAD_ROWS
    imp = _importance(jnp.concatenate(p_sums, axis=0), _overlap_matrix(n_cmp, n_blk))
    q_pos_r = past_len + lax.broadcasted_iota(I32, (n_rows, 1), 0) % HEAD_ROWS
    blk = lax.broadcasted_iota(I32, (1, n_blk), 1)
    cur = q_pos_r // SEL_BLOCK
    forced = (blk == 0) | (blk == cur) | (blk == cur - 1)
    score = jnp.where(blk * SEL_BLOCK <= q_pos_r, jnp.where(forced, BIG, imp), -BIG)
    lane = lax.broadcasted_iota(I32, (n_rows, n_blk), 1).astype(F32)
    out_lane = lax.broadcasted_iota(I32, (n_rows, LANES), 1)
    picks = jnp.zeros((n_rows, LANES), F32)
    for n in range(TOP_N):
        best = jnp.max(score, axis=-1, keepdims=True)
        pick = jnp.min(jnp.where(score == best, lane, float(n_blk)), axis=-1, keepdims=True)
        picks = jnp.where(out_lane == n, pick, picks)
        score = jnp.where(lane == pick, -3e38, score)
    picks = picks.astype(I32)
    for r in range(bb * N_KV):
        idx_ref[r // N_KV, r % N_KV] = picks[r * HEAD_ROWS:(r + 1) * HEAD_ROWS]


def _cmp_select_sample(q_hm, kcc, vcc, past_len, bb=8):
    nb = q_hm.shape[0]
    bb = min(bb, nb)
    assert nb % bb == 0
    n_cmp = kcc.shape[2]
    n_blk = -(-(past_len // SEL_BLOCK + 1) // LANES) * LANES
    rows = GQA * HEAD_ROWS
    spec = lambda r, w: pl.BlockSpec((bb, N_KV, r, w), lambda b: (b, 0, 0, 0))
    return pl.pallas_call(
        functools.partial(_cmp_select_sample_body, past_len=past_len, n_blk=n_blk),
        grid=(nb // bb,),
        in_specs=[spec(rows, DH), spec(n_cmp, DH), spec(n_cmp, DH)],
        out_specs=[spec(rows, DH), spec(HEAD_ROWS, LANES)],
        out_shape=[jax.ShapeDtypeStruct((nb, N_KV, rows, DH), F32),
                   jax.ShapeDtypeStruct((nb, N_KV, HEAD_ROWS, LANES), I32)],
        compiler_params=_params("arbitrary"),
        name="cmp_select_sample",
    )(q_hm, kcc, vcc)


def _attn_sample_body(pt_ref, idx_s_ref, q_ref, idx_v_ref, ksn_ref, vsn_ref, kwn_ref, vwn_ref, kwin_ref, vwin_ref,
                      oc_ref, gates_ref, kpool_ref, vpool_ref, o_ref, kbuf, vbuf, sem, *, dec_seq, n_pool_blk):
    b = pl.program_id(0)
    slot = b % 2
    assert PAGE == 2 * SEL_BLOCK

    def gather(bb, sl):
        for kt in range(N_KV * dec_seq):
            k = kt // dec_seq
            for n in range(TOP_N):
                blk = idx_s_ref[bb, kt * TOP_N + n]
                src = jnp.where(blk < n_pool_blk, blk, 0)
                page = pt_ref[bb, lax.shift_right_logical(src, 1)]
                rows = pl.ds(pl.multiple_of((src & 1) * SEL_BLOCK, SEL_BLOCK), SEL_BLOCK)
                dst = pl.ds(n * SEL_BLOCK, SEL_BLOCK)
                pltpu.make_async_copy(kpool_ref.at[page, rows, k], kbuf.at[sl, kt, dst], sem.at[sl, 0]).start()
                pltpu.make_async_copy(vpool_ref.at[page, rows, k], vbuf.at[sl, kt, dst], sem.at[sl, 1]).start()

    @pl.when(b == 0)
    def _():
        gather(0, 0)

    @pl.when(b + 1 < pl.num_programs(0))
    def _():
        gather(b + 1, 1 - slot)

    pltpu.make_async_copy(kbuf.at[slot], kbuf.at[slot], sem.at[slot, 0]).wait()
    pltpu.make_async_copy(vbuf.at[slot], vbuf.at[slot], sem.at[slot, 1]).wait()

    n_keys = TOP_N * SEL_BLOCK
    rows = dec_seq * TOKEN_ROWS
    key_slot = lax.broadcasted_iota(I32, (LANES, n_keys), 1) // SEL_BLOCK
    expand = (lax.broadcasted_iota(I32, (LANES, n_keys), 0) == key_slot).astype(BF16)
    new_col = lax.broadcasted_iota(I32, (1, TOKEN_ROWS), 1)
    t_row = lax.broadcasted_iota(I32, (rows, 1), 0) // TOKEN_ROWS
    n_win = kwin_ref.shape[1]
    win_old_vis = lax.broadcasted_iota(I32, (1, n_win), 1) > t_row
    win_new_vis = new_col <= t_row
    for k in range(N_KV):
        qk = q_ref[0, k]
        pool_ok = _dot((idx_v_ref[0, k] < n_pool_blk).astype(BF16), expand)
        ksn, vsn = ksn_ref[0, k], vsn_ref[0, k]
        hs = slice(k * DH, (k + 1) * DH)
        s_old = jnp.where(win_old_vis, _dot_nt(qk, kwin_ref[0, :, hs].astype(BF16)), NEG)
        s_new = jnp.where(win_new_vis, _dot_nt(qk, kwn_ref[0, k]), NEG)
        m = jnp.maximum(jnp.max(s_old, axis=-1, keepdims=True), jnp.max(s_new, axis=-1, keepdims=True))
        p_old, p_new = jnp.exp2(s_old - m), jnp.exp2(s_new - m)
        l = jnp.sum(p_old, axis=-1, keepdims=True) + jnp.sum(p_new, axis=-1, keepdims=True)
        o_win = (_dot(p_old.astype(BF16), vwin_ref[0, :, hs].astype(BF16)) + _dot(p_new.astype(BF16), vwn_ref[0, k])) / l
        gates = gates_ref[0, k]
        for t in range(dec_seq):
            ts = slice(t * TOKEN_ROWS, (t + 1) * TOKEN_ROWS)
            kt = k * dec_seq + t
            qt = qk[ts]
            s_pool = jnp.where(pool_ok[t:t + 1, :] > 0.5, _dot_nt(qt, kbuf[slot, kt].astype(BF16)), NEG)
            s_cur = jnp.where(new_col <= t, _dot_nt(qt, ksn), NEG)
            m = jnp.maximum(jnp.max(s_pool, axis=-1, keepdims=True), jnp.max(s_cur, axis=-1, keepdims=True))
            p_pool, p_cur = jnp.exp2(s_pool - m), jnp.exp2(s_cur - m)
            l = jnp.sum(p_pool, axis=-1, keepdims=True) + jnp.sum(p_cur, axis=-1, keepdims=True)
            o_sel = (_dot(p_pool.astype(BF16), vbuf[slot, kt].astype(BF16)) + _dot(p_cur.astype(BF16), vsn)) / l
            gt = gates[ts]
            o_ref[0, k, ts, :] = (gt[:, 0:1] * oc_ref[0, k, ts, :] + gt[:, 1:2] * o_sel + gt[:, 2:3] * o_win[ts])


def _attn_sample(page_table, idx_flat, q_tm, idx_pad, ksn, vsn, kwn, vwn, k_win, v_win, oc_tm, gates_tm,
                 k_pool, v_pool, dec_seq, past_len):
    nb = q_tm.shape[0]
    rows = dec_seq * TOKEN_ROWS
    n_win = k_win.shape[1]
    spec = lambda r, w: pl.BlockSpec((1, N_KV, r, w), lambda b, *_: (b, 0, 0, 0))
    win = pl.BlockSpec((1, n_win, KV_W), lambda b, *_: (b, 0, 0))
    any_spec = pl.BlockSpec(memory_space=pl.ANY)
    grid_spec = pltpu.PrefetchScalarGridSpec(
        num_scalar_prefetch=2,
        grid=(nb,),
        in_specs=[spec(rows, DH), spec(TOKEN_ROWS, LANES), spec(TOKEN_ROWS, DH), spec(TOKEN_ROWS, DH),
                  spec(TOKEN_ROWS, DH), spec(TOKEN_ROWS, DH), win, win, spec(rows, DH), spec(rows, LANES),
                  any_spec, any_spec],
        out_specs=spec(rows, DH),
        scratch_shapes=[pltpu.VMEM((2, N_KV * dec_seq, TOP_N * SEL_BLOCK, DH), F32),
                        pltpu.VMEM((2, N_KV * dec_seq, TOP_N * SEL_BLOCK, DH), F32),
                        pltpu.SemaphoreType.DMA((2, 2))],
    )
    return pl.pallas_call(
        functools.partial(_attn_sample_body, dec_seq=dec_seq, n_pool_blk=past_len // SEL_BLOCK),
        grid_spec=grid_spec,
        out_shape=jax.ShapeDtypeStruct((nb, N_KV, rows, DH), F32),
        compiler_params=_params("arbitrary"),
        name="attn_sample",
    )(page_table, idx_flat, q_tm, idx_pad, ksn, vsn, kwn, vwn, k_win, v_win, oc_tm, gates_tm, k_pool, v_pool)


def _mix_out_sample_body(oa_ref, u_ref, st_ref, x_ref, cw_ref, cb_ref, lng_ref, lnb_ref, ga_ref, gc_ref,
                         woa_ref, woc_ref, gm_ref, wmq_ref, o_ref, qm_ref, conv_scr, *, nb, dec_seq):
    hist = CONV_W - 1

    def ext(j):
        if j < hist:
            return st_ref[j * nb:(j + 1) * nb, :]
        return u_ref[(j - hist) * nb:(j - hist + 1) * nb, :]

    for t in range(dec_seq):
        acc = jnp.zeros((nb, CONV_CH), F32) + cb_ref[...]
        for k in range(CONV_W):
            acc = acc + cw_ref[k:k + 1, :] * ext(t + k)
        conv_scr[t * nb:(t + 1) * nb, :] = acc
    conv_n = _ln_silu_rms(conv_scr[...], lng_ref, lnb_ref, gc_ref)
    attn_n = (_rms(oa_ref[...]) * ga_ref[...]).astype(BF16)
    x1 = x_ref[...] + _dot(attn_n, woa_ref[...]) + _dot(conv_n, woc_ref[...])
    o_ref[...] = x1
    qm_ref[...] = _dot((_rms(x1) * gm_ref[...]).astype(BF16), wmq_ref[...])


def _mix_out_sample(o_attn, u, conv_state, x, mw, g_mem, w_mq, nb, dec_seq):
    m = x.shape[0]
    args = (o_attn, u, conv_state, x) + tuple(mw) + (g_mem, w_mq)
    return pl.pallas_call(
        functools.partial(_mix_out_sample_body, nb=nb, dec_seq=dec_seq),
        grid=(1,),
        in_specs=[_const_spec(a.shape) for a in args],
        out_specs=[_const_spec((m, D_MODEL)), _const_spec((m, MEM_W))],
        out_shape=[jax.ShapeDtypeStruct((m, D_MODEL), F32), jax.ShapeDtypeStruct((m, MEM_W), F32)],
        scratch_shapes=[pltpu.VMEM((m, CONV_CH), F32)],
        compiler_params=_params("arbitrary"),
        name="mix_out_sample",
    )(*args)


def _mem_attn_sample_body(q_ref, mk_ref, mv_ref, o_ref):
    o_ref[0] = _mem_attn_core(q_ref[0], mk_ref[0].astype(BF16), mv_ref[0].astype(BF16))


def _mem_attn_sample(q_pad, mk, mv):
    nb, rows, _ = q_pad.shape
    n_mem = mk.shape[1]
    q_spec = pl.BlockSpec((1, rows, MEM_W), lambda b: (b, 0, 0))
    mem = pl.BlockSpec((1, n_mem, MEM_W), lambda b: (b, 0, 0))
    return pl.pallas_call(
        _mem_attn_sample_body,
        grid=(nb,),
        in_specs=[q_spec, mem, mem],
        out_specs=q_spec,
        out_shape=jax.ShapeDtypeStruct((nb, rows, MEM_W), BF16),
        compiler_params=_params("arbitrary"),
        name="mem_attn_sample",
    )(q_pad, mk, mv)


def _ffn_sample_body(x_ref, a_ref, wo_ref, g_ref, wg_ref, wu_ref, cw_ref, cb_ref, wd_ref, gf_ref, st_ref,
                     o_ref, sto_ref, x2_scr, h_scr, acc_scr, gate_scr, conv_scr, *, nb, dec_seq):
    f = pl.program_id(0)
    hist = FFN_CONV_W - 1

    @pl.when(f == 0)
    def _():
        x2 = x_ref[...] + _dot(a_ref[...], wo_ref[...])
        x2_scr[...] = x2
        h_scr[...] = (_rms(x2) * g_ref[...]).astype(BF16)
        acc_scr[...] = jnp.zeros(acc_scr.shape, F32)

    hb = h_scr[...]
    gate_scr[...] = _dot(hb, wg_ref[...])

    def ext(j):
        if j < hist:
            return st_ref[j * nb:(j + 1) * nb, :]
        return gate_scr[(j - hist) * nb:(j - hist + 1) * nb, :]

    for t in range(dec_seq):
        acc = cb_ref[...] + cw_ref[0:1, :] * ext(t)
        for k in range(1, FFN_CONV_W):
            acc = acc + cw_ref[k:k + 1, :] * ext(t + k)
        conv_scr[t * nb:(t + 1) * nb, :] = acc
    for j in range(hist):
        sto_ref[j * nb:(j + 1) * nb, :] = ext(dec_seq + j)
    conv = conv_scr[...]
    a = conv * jax.nn.sigmoid(conv) * _dot(hb, wu_ref[...])
    acc_scr[...] += _dot(a.astype(BF16), wd_ref[...])
    _ffn_tail(f, pl.num_programs(0), x2_scr, acc_scr, gf_ref, o_ref)


def _ffn_sample(x1, a, w_mo, fw, ffn_state, nb, dec_seq, tf=512):
    g, wg, wu, cw, cb, wd, gf = fw
    m = x1.shape[0]
    nf = D_FF // tf
    hist = FFN_CONV_W - 1
    col = lambda r: pl.BlockSpec((r, tf), lambda f: (0, f))
    full = lambda shape: pl.BlockSpec(shape, lambda f: (0,) * len(shape))
    return pl.pallas_call(
        functools.partial(_ffn_sample_body, nb=nb, dec_seq=dec_seq),
        grid=(nf,),
        in_specs=[full(x1.shape), full(a.shape), full(w_mo.shape), full(g.shape), col(D_MODEL), col(D_MODEL),
                  col(cw.shape[0]), col(1), pl.BlockSpec((tf, D_MODEL), lambda f: (f, 0)), full(gf.shape),
                  col(nb * hist)],
        out_specs=[full((m, D_MODEL)), col(nb * hist)],
        out_shape=[jax.ShapeDtypeStruct((m, D_MODEL), F32), jax.ShapeDtypeStruct((nb * hist, D_FF), F32)],
        scratch_shapes=[pltpu.VMEM((m, D_MODEL), F32), pltpu.VMEM((m, D_MODEL), BF16), pltpu.VMEM((m, D_MODEL), F32),
                        pltpu.VMEM((m, tf), F32), pltpu.VMEM((m, tf), F32)],
        compiler_params=_params("arbitrary"),
        name="ffn_sample",
    )(x1, a, w_mo, g, wg, wu, cw, cb, wd, gf, ffn_state)


def _prepare_weights(norm_mix_g, w_in, b_gate, cmp_k, cmp_v, conv_w, conv_b, conv_ln_g, conv_ln_b,
                     grp_norm_attn_g, grp_norm_conv_g, w_out, norm_mem_g, mem_norm_g, w_mq, w_mk, w_mv, w_mo,
                     norm_ffn_g, w_ffn_gate, w_ffn_up, ffn_conv_w, ffn_conv_b, w_ffn_down, norm_final_g):
    vec = lambda v: v.reshape(1, -1)
    kv_end = ATTN_W + 6 * KV_W
    n_gate_cols = N_KV * GQA * N_GATE
    per_kv = GQA * N_GATE
    wg = w_in[:, kv_end:kv_end + n_gate_cols].reshape(D_MODEL, N_KV, per_kv)
    wg = jnp.pad(wg, ((0, 0), (0, 0), (0, LANES - per_kv))).reshape(D_MODEL, N_KV * LANES)
    bg = jnp.pad(b_gate.reshape(N_KV, per_kv), ((0, 0), (0, LANES - per_kv))).reshape(1, N_KV * LANES)
    return dict(
        in_proj=(vec(norm_mix_g), w_in[:, :ATTN_W].astype(BF16), w_in[:, ATTN_W:kv_end].astype(BF16),
                 wg.astype(BF16), bg, w_in[:, kv_end + n_gate_cols:].astype(BF16)),
        cmp_k=_compress_weights(*cmp_k),
        cmp_v=_compress_weights(*cmp_v),
        mix=(jnp.pad(conv_w, ((0, 32 - CONV_W), (0, 0))), vec(conv_b), vec(conv_ln_g), vec(conv_ln_b),
             vec(grp_norm_attn_g), vec(grp_norm_conv_g), w_out[:ATTN_W].astype(BF16), w_out[ATTN_W:].astype(BF16)),
        mem_kv=(vec(mem_norm_g), w_mk.astype(BF16), w_mv.astype(BF16)),
        mem=(vec(norm_mem_g), w_mq.astype(BF16), w_mo.astype(BF16)),
        ffn=(vec(norm_ffn_g), w_ffn_gate.astype(BF16), w_ffn_up.astype(BF16),
             jnp.pad(ffn_conv_w, ((0, 8 - FFN_CONV_W), (0, 0))), vec(ffn_conv_b), w_ffn_down.astype(BF16),
             vec(norm_final_g)),
    )


def _prompt_forward(x_prompt, mem_prompt, w):
    batch, t, _ = x_prompt.shape
    x = x_prompt.reshape(batch * t, D_MODEL)
    (q, kc, vc, ks, vs, kw, vw, ksb, vsb, kwb, vwb, gates, u) = _in_proj(x, *w["in_proj"], tm=256)
    kcc = _compress_prompt(kc, w["cmp_k"], batch)
    vcc = _compress_prompt(vc, w["cmp_v"], batch)
    gates_t = gates.reshape(batch * t, N_KV, LANES)[:, :, :GATE_ROWS].transpose(1, 2, 0)
    o_attn = _nsa_prompt(q.T, kcc, vcc.transpose(0, 1, 3, 2), ksb, vsb.T, kwb, vwb.T,
                         gates_t.reshape(N_KV * GATE_ROWS, batch * t), batch)
    x1 = _mix_out_prompt(o_attn, u, x, w["mix"], batch)
    n_mem = mem_prompt.shape[1]
    mk, mv = _mem_kv(mem_prompt.reshape(batch * n_mem, D_MODEL), *w["mem_kv"])
    g_mem, w_mq, w_mo = w["mem"]
    x2 = _mem_attn_prompt(x1, g_mem, w_mq, mk.reshape(batch, n_mem, MEM_W), mv.reshape(batch, n_mem, MEM_W),
                          w_mo, batch)
    y, ffn_tail = _ffn_prompt(x2, *w["ffn"], batch=batch)
    kv5 = lambda a: a.reshape(1, batch, t, N_KV, DH)
    win = lambda a: a.reshape(batch, t, N_KV, DH)[None, :, t - min(WINDOW, t):]
    tiles = ffn_tail.shape[0] // batch
    new_ffn = ffn_tail.reshape(batch, tiles, ffn_tail.shape[1], D_FF)[:, -1, -(FFN_CONV_W - 1):]
    new_conv = u.reshape(batch, t, CONV_CH)[:, t - (CONV_W - 1):]
    mem5 = lambda a: a.reshape(1, batch, n_mem, MEM_HEADS, MEM_DH)
    return (y.reshape(batch, t, D_MODEL), kv5(kc), kv5(vc), kv5(ks), kv5(vs), win(kw), win(vw),
            new_conv[None], new_ffn[None], mem5(mk), mem5(mv))


def _pad_axis(a, axis, size):
    pads = [(0, 0)] * a.ndim
    pads[axis] = (0, size - a.shape[axis])
    return jnp.pad(a, pads)


def _sample_forward(x_sample, pools, k_win, v_win, conv_state, ffn_state, mem_k, mem_v, page_table, w):
    nb, dec_seq, _ = x_sample.shape
    m = nb * dec_seq
    past_len = page_table.shape[1] * PAGE
    assert dec_seq <= HEAD_ROWS and k_win.shape[1] == WINDOW
    x = x_sample.reshape(m, D_MODEL)
    (q, kc, vc, ks, vs, kw, vw, ksb, vsb, kwb, vwb, gates, u) = _in_proj(x, *w["in_proj"], tm=m)
    pool_kc, pool_vc, pool_ks, pool_vs = pools
    kcc = _compress_sample(pool_kc, page_table, w["cmp_k"])
    vcc = _compress_sample(pool_vc, page_table, w["cmp_v"])

    q5 = q.reshape(nb, dec_seq, N_KV, GQA, DH)
    q_hm = _pad_axis(q5.transpose(0, 2, 3, 1, 4), 3, HEAD_ROWS).reshape(nb, N_KV, GQA * HEAD_ROWS, DH)
    q_tm = _pad_axis(q5.transpose(0, 2, 1, 3, 4), 3, TOKEN_ROWS).reshape(nb, N_KV, dec_seq * TOKEN_ROWS, DH)
    oc_hm, idx = _cmp_select_sample(q_hm, kcc, vcc, past_len)
    oc_tm = oc_hm.reshape(nb, N_KV, GQA, HEAD_ROWS, DH)[:, :, :, :dec_seq].transpose(0, 1, 3, 2, 4)
    oc_tm = _pad_axis(oc_tm, 3, TOKEN_ROWS).reshape(nb, N_KV, dec_seq * TOKEN_ROWS, DH)
    idx_flat = idx[:, :, :dec_seq, :TOP_N].reshape(nb, N_KV * dec_seq * TOP_N)
    idx_pad = _pad_axis(idx, 2, TOKEN_ROWS)
    new_rows = lambda a: _pad_axis(a.reshape(nb, dec_seq, N_KV, DH).transpose(0, 2, 1, 3), 2, TOKEN_ROWS)
    gates_tm = gates.reshape(nb, dec_seq, N_KV, LANES)[..., :GQA * N_GATE].reshape(nb, dec_seq, N_KV, GQA, N_GATE)
    gates_tm = _pad_axis(_pad_axis(gates_tm.transpose(0, 2, 1, 3, 4), 3, TOKEN_ROWS), 4, LANES)
    gates_tm = gates_tm.reshape(nb, N_KV, dec_seq * TOKEN_ROWS, LANES)
    o_tm = _attn_sample(page_table, idx_flat, q_tm, idx_pad, new_rows(ksb), new_rows(vsb), new_rows(kwb),
                        new_rows(vwb), k_win.reshape(nb, WINDOW, KV_W), v_win.reshape(nb, WINDOW, KV_W),
                        oc_tm, gates_tm, pool_ks, pool_vs, dec_seq, past_len)
    o_attn = o_tm.reshape(nb, N_KV, dec_seq, TOKEN_ROWS, DH)[:, :, :, :GQA].transpose(2, 0, 1, 3, 4).reshape(m, ATTN_W)
    step_major = lambda a: a.reshape(nb, -1, a.shape[-1]).transpose(1, 0, 2).reshape(-1, a.shape[-1])
    batch_major = lambda a: a.reshape(-1, nb, a.shape[-1]).transpose(1, 0, 2)

    g_mem, w_mq, w_mo = w["mem"]
    x1, qm = _mix_out_sample(o_attn, step_major(u), step_major(conv_state), step_major(x), w["mix"], g_mem, w_mq,
                             nb, dec_seq)
    n_mem = mem_k.shape[1]
    q_pad = _pad_axis(batch_major(qm), 1, TOKEN_ROWS).astype(BF16)
    a = _mem_attn_sample(q_pad, mem_k.reshape(nb, n_mem, MEM_W), mem_v.reshape(nb, n_mem, MEM_W))
    y, new_ffn = _ffn_sample(x1, step_major(a[:, :dec_seq]), w_mo, w["ffn"], step_major(ffn_state), nb, dec_seq)

    kv5 = lambda a: a.reshape(1, nb, dec_seq, N_KV, DH)
    shift = lambda buf, new: jnp.concatenate([buf[:, dec_seq:], new.reshape((nb, dec_seq) + buf.shape[2:])], axis=1)[None]
    return (batch_major(y), kv5(kc), kv5(vc), kv5(ks), kv5(vs), shift(k_win, kw), shift(v_win, vw),
            shift(conv_state, u), batch_major(new_ffn)[None])


def kernel(x_prompt, x_sample, cache_k_cmp, cache_v_cmp, cache_k_sel, cache_v_sel, cache_k_win, cache_v_win,
           state_conv, state_ffn_conv, cache_mem_k, cache_mem_v, page_table, mem_prompt,
           norm_mix_g, w_in, b_gate, cmp_k_pe, cmp_k_w1, cmp_k_b1, cmp_k_w2, cmp_v_pe, cmp_v_w1, cmp_v_b1, cmp_v_w2,
           conv_w, conv_b, conv_ln_g, conv_ln_b, grp_norm_attn_g, grp_norm_conv_g, w_out,
           norm_mem_g, mem_norm_g, w_mq, w_mk, w_mv, w_mo,
           norm_ffn_g, w_ffn_gate, w_ffn_up, ffn_conv_w, ffn_conv_b, w_ffn_down, norm_final_g):
    assert w_in.shape[0] == 1, "single-layer step"
    w = _prepare_weights(norm_mix_g[0], w_in[0], b_gate[0],
                         (cmp_k_pe[0], cmp_k_w1[0], cmp_k_b1[0], cmp_k_w2[0]),
                         (cmp_v_pe[0], cmp_v_w1[0], cmp_v_b1[0], cmp_v_w2[0]),
                         conv_w[0], conv_b[0], conv_ln_g[0], conv_ln_b[0], grp_norm_attn_g[0], grp_norm_conv_g[0],
                         w_out[0], norm_mem_g[0], mem_norm_g[0], w_mq[0], w_mk[0], w_mv[0], w_mo[0],
                         norm_ffn_g[0], w_ffn_gate[0], w_ffn_up[0], ffn_conv_w[0], ffn_conv_b[0], w_ffn_down[0],
                         norm_final_g)
    p = _prompt_forward(x_prompt, mem_prompt, w)
    s = _sample_forward(x_sample, (cache_k_cmp[0], cache_v_cmp[0], cache_k_sel[0], cache_v_sel[0]),
                        cache_k_win[0], cache_v_win[0], state_conv[0], state_ffn_conv[0],
                        cache_mem_k[0], cache_mem_v[0], page_table, w)
    return (p[0], s[0]) + p[1:] + s[1:]
```

```python
import functools

import jax
import jax.numpy as jnp
from jax import lax
from jax.experimental import pallas as pl
from jax.experimental.pallas import tpu as pltpu

F32 = jnp.float32
BF16 = jnp.bfloat16
I32 = jnp.int32

D_MODEL = 2048
N_KV = 2
GQA = 4
DH = 128
ATTN_W = N_KV * GQA * DH
KV_W = N_KV * DH
N_GATE = 3
BLOCK_CMP = 32
STRIDE_CMP = 16
CMP_HID = 256
SEL_BLOCK = 64
TOP_N = 16
WINDOW = 512
CONV_CH = D_MODEL - ATTN_W
CONV_W = 31
D_FF = 5632
FFN_CONV_W = 3
MEM_HEADS = 4
MEM_DH = 128
MEM_W = MEM_HEADS * MEM_DH
ATTN_SCALE = DH ** -0.5
Q_PRESCALE = ATTN_SCALE * 1.4426950408889634
MEM_SCALE = MEM_DH ** -0.5
EPS = 1e-6
NEG = -1e30
BIG = 1e30
LANES = 128
VMEM_LIMIT = 56 * 1024 * 1024


def _dot(a, b):
    return jnp.dot(a, b, preferred_element_type=F32)


def _dot_nt(a, b):
    return lax.dot_general(a, b, (((1,), (1,)), ((), ())), preferred_element_type=F32)


def _rms(x):
    return x * lax.rsqrt(jnp.mean(x * x, axis=-1, keepdims=True) + EPS)


def _const_spec(shape):
    return pl.BlockSpec(shape, lambda *_: (0,) * len(shape), pipeline_mode=pl.Buffered(1))


def _params(*sem):
    return pltpu.CompilerParams(dimension_semantics=sem, vmem_limit_bytes=VMEM_LIMIT)


def _in_proj_body(x_ref, g_ref, wq_ref, wkv_ref, wg_ref, bg_ref, wglu_ref,
                  q_ref, kc_ref, vc_ref, ks_ref, vs_ref, kw_ref, vw_ref,
                  ksb_ref, vsb_ref, kwb_ref, vwb_ref, gates_ref, u_ref):
    hb = (_rms(x_ref[...]) * g_ref[...]).astype(BF16)
    half = ATTN_W // 2
    for c in range(2):
        q_ref[:, c * half:(c + 1) * half] = (_dot(hb, wq_ref[:, c * half:(c + 1) * half]) * Q_PRESCALE).astype(BF16)
    f32_outs = (kc_ref, vc_ref, ks_ref, vs_ref, kw_ref, vw_ref)
    bf_outs = (None, None, ksb_ref, vsb_ref, kwb_ref, vwb_ref)
    for c in range(6):
        r = _dot(hb, wkv_ref[:, c * KV_W:(c + 1) * KV_W])
        for k in range(N_KV):
            f32_outs[c][:, k, :] = r[:, k * DH:(k + 1) * DH]
        if bf_outs[c] is not None:
            bf_outs[c][...] = r.astype(BF16)
    gates_ref[...] = jax.nn.sigmoid(_dot(hb, wg_ref[...]) + bg_ref[...])
    cw = 256
    for c in range(CONV_CH // cw):
        a = _dot(hb, wglu_ref[:, c * cw:(c + 1) * cw])
        gt = _dot(hb, wglu_ref[:, CONV_CH + c * cw:CONV_CH + (c + 1) * cw])
        u_ref[:, c * cw:(c + 1) * cw] = a * jax.nn.sigmoid(gt)


def _in_proj(x, g, wq, wkv, wg, bg, wglu, tm):
    m = x.shape[0]
    row = lambda w: pl.BlockSpec((tm, w), lambda i: (i, 0))
    out_shape = ([jax.ShapeDtypeStruct((m, ATTN_W), BF16)]
                 + [jax.ShapeDtypeStruct((m, N_KV, DH), F32)] * 6
                 + [jax.ShapeDtypeStruct((m, KV_W), BF16)] * 4
                 + [jax.ShapeDtypeStruct((m, N_KV * LANES), F32),
                    jax.ShapeDtypeStruct((m, CONV_CH), F32)])
    state = pl.BlockSpec((tm, N_KV, DH), lambda i: (i, 0, 0))
    out_specs = ([row(ATTN_W)] + [state] * 6 + [row(KV_W)] * 4 + [row(N_KV * LANES), row(CONV_CH)])
    return pl.pallas_call(
        _in_proj_body,
        grid=(m // tm,),
        in_specs=[row(D_MODEL), _const_spec(g.shape), _const_spec(wq.shape), _const_spec(wkv.shape),
                  _const_spec(wg.shape), _const_spec(bg.shape), _const_spec(wglu.shape)],
        out_specs=out_specs,
        out_shape=out_shape,
        compiler_params=_params("arbitrary"),
        name="in_proj",
    )(x, g, wq, wkv, wg, bg, wglu)


def _compress_rows(get_lanes, n, pe_ref, w1l_ref, w1t_ref, b1_ref, w2_ref):
    xk = jnp.concatenate([get_lanes(l) for l in range(STRIDE_CMP)], axis=1)
    lead = _dot((xk + pe_ref[0:1, :]).astype(BF16), w1l_ref[...])
    trail = _dot((xk + pe_ref[1:2, :]).astype(BF16), w1t_ref[...])
    trail_next = pltpu.roll(trail, n - 1, axis=0)
    hid = jax.nn.gelu(lead + trail_next + b1_ref[...])
    out = _dot(hid.astype(BF16), w2_ref[...])
    rows = lax.broadcasted_iota(I32, (n, 1), 0)
    return jnp.where(rows < n - 1, out, 0.0)


def _compress_prompt_body(x_ref, pe_ref, w1l_ref, w1t_ref, b1_ref, w2_ref, o_ref):
    n = x_ref.shape[0] // STRIDE_CMP
    for k in range(N_KV):
        get = lambda l, k=k: x_ref[pl.ds(l, n, stride=STRIDE_CMP), k, :]
        o_ref[0, k] = _compress_rows(get, n, pe_ref, w1l_ref, w1t_ref, b1_ref, w2_ref)


def _compress_weights(pe, w1, b1, w2):
    half = STRIDE_CMP * DH
    pe2 = pe.reshape(2, half)
    return pe2, w1[:half].astype(BF16), w1[half:].astype(BF16), b1.reshape(1, CMP_HID), w2.astype(BF16)


def _compress_prompt(rows, cw, batch):
    t = rows.shape[0] // batch
    n = t // STRIDE_CMP
    return pl.pallas_call(
        _compress_prompt_body,
        grid=(batch,),
        in_specs=[pl.BlockSpec((t, N_KV, DH), lambda b: (b, 0, 0))] + [_const_spec(w.shape) for w in cw],
        out_specs=pl.BlockSpec((1, N_KV, n, DH), lambda b: (b, 0, 0, 0)),
        out_shape=jax.ShapeDtypeStruct((batch, N_KV, n, DH), F32),
        compiler_params=_params("arbitrary"),
        name="compress_prompt",
    )(rows, *cw)


def _overlap_matrix(n_cmp, n_blk=LANES):
    i = lax.broadcasted_iota(I32, (n_cmp, n_blk), 0) * STRIDE_CMP
    j = lax.broadcasted_iota(I32, (n_cmp, n_blk), 1) * SEL_BLOCK
    ov = jnp.maximum(jnp.minimum(i + BLOCK_CMP, j + SEL_BLOCK) - jnp.maximum(i, j), 0)
    return (ov.astype(F32) * (1.0 / BLOCK_CMP)).astype(BF16)


def _importance(p_sum, ov):
    hi = p_sum.astype(BF16)
    r1 = p_sum - hi.astype(F32)
    mid = r1.astype(BF16)
    lo = (r1 - mid.astype(F32)).astype(BF16)
    return _dot(hi, ov) + _dot(mid, ov) + _dot(lo, ov)


def _softmax_av(s, v):
    p = jnp.exp(s - jnp.max(s, axis=-1, keepdims=True))
    return _dot(p.astype(BF16), v) / jnp.sum(p, axis=-1, keepdims=True)


GATE_ROWS = 16
VROWS = DH + 16


def _nsa_prompt_body(qt_ref, kcc_ref, vcct_ref, ks_ref, vst_ref, kw_ref, vwt_ref, gt_ref, o_ref,
                     acc_scr, bias_scr, *, tq, tk, n_sel):
    i = pl.program_id(2)
    t0 = i * tq
    nq = GQA * tq
    heads = lambda a: jnp.concatenate([a] * GQA, axis=1)
    q_pos = t0 + lax.broadcasted_iota(I32, (1, tq), 1)
    q_pos4 = heads(q_pos)
    qt = jnp.concatenate([qt_ref[g * DH:(g + 1) * DH, :] for g in range(GQA)], axis=1)

    n_cmp = kcc_ref.shape[2]
    cmp_end = lax.broadcasted_iota(I32, (n_cmp, 1), 0) * STRIDE_CMP + (BLOCK_CMP - 1)
    cvis = cmp_end <= q_pos4
    s = jnp.where(cvis, _dot(kcc_ref[0, 0].astype(BF16), qt), NEG)
    e = jnp.where(cvis, jnp.exp2(s - jnp.max(s, axis=0, keepdims=True)), 0.0)
    p = e / jnp.maximum(jnp.sum(e, axis=0, keepdims=True), 1e-30)
    o_cmp = _dot(vcct_ref[0, 0].astype(BF16), p.astype(BF16))
    p_sum = p[:, 0:tq]
    for g in range(1, GQA):
        p_sum = p_sum + p[:, g * tq:(g + 1) * tq]

    ov_i = lax.broadcasted_iota(I32, (n_sel, n_cmp), 1) * STRIDE_CMP
    ov_j = lax.broadcasted_iota(I32, (n_sel, n_cmp), 0) * SEL_BLOCK
    ov = jnp.maximum(jnp.minimum(ov_i + BLOCK_CMP, ov_j + SEL_BLOCK) - jnp.maximum(ov_i, ov_j), 0)
    ov = (ov.astype(F32) * (1.0 / BLOCK_CMP)).astype(BF16)
    hi = p_sum.astype(BF16)
    r1 = p_sum - hi.astype(F32)
    mid = r1.astype(BF16)
    lo = (r1 - mid.astype(F32)).astype(BF16)
    imp = _dot(ov, hi) + _dot(ov, mid) + _dot(ov, lo)
    blk = lax.broadcasted_iota(I32, (n_sel, tq), 0)
    cur = q_pos // SEL_BLOCK
    forced = (blk == 0) | (blk == cur) | (blk == cur - 1)
    s_t = jnp.where(blk * SEL_BLOCK <= q_pos, jnp.where(forced, BIG, imp), -BIG)
    rank = jnp.zeros((n_sel, tq), F32)
    for ib in range(n_sel):
        row = s_t[ib:ib + 1, :]
        beats = (row > s_t) | ((row == s_t) & (blk > ib))
        rank = rank + jnp.where(beats, 1.0, 0.0)
    bias_scr[...] = jnp.where(rank < TOP_N, 0.0, NEG)

    acc_scr[...] = jnp.zeros(acc_scr.shape, F32)
    blocks_per_tile = tk // SEL_BLOCK
    last_tile = ks_ref.shape[0] // tk - 1

    def key_tile(j, state, m, causal):
        jd = jnp.minimum(j, last_tile)
        k0 = pl.multiple_of(jd * tk, tk)
        bias = jnp.concatenate(
            [jnp.broadcast_to(bias_scr[pl.ds(jd * blocks_per_tile + c, 1), :], (SEL_BLOCK, tq))
             for c in range(blocks_per_tile)], axis=0)
        if causal:
            bias = jnp.where(j * tk + lax.broadcasted_iota(I32, (tk, 1), 0) <= q_pos, bias, NEG)
        s = _dot(ks_ref[pl.ds(k0, tk), :], qt) + heads(bias)
        m_new = jnp.maximum(m, jnp.max(s, axis=0, keepdims=True))
        p = jnp.exp2(s - m_new).astype(BF16)
        acc_scr[state] = jnp.exp2(m - m_new) * acc_scr[state] + _dot(vst_ref[:, pl.ds(k0, tk)], p)
        return m_new

    def tile_pair(jp, c):
        return key_tile(2 * jp, 0, c[0], False), key_tile(2 * jp + 1, 1, c[1], False)

    n_pairs = t0 // (2 * tk)
    empty = jnp.full((1, nq), NEG, F32)
    c = lax.fori_loop(0, n_pairs, tile_pair, (empty, empty))
    m0 = key_tile(2 * n_pairs, 0, c[0], True)
    m1 = key_tile(2 * n_pairs + 1, 1, c[1], True)
    m = jnp.maximum(m0, m1)
    merged = jnp.exp2(m0 - m) * acc_scr[0] + jnp.exp2(m1 - m) * acc_scr[1]
    o_sel = merged[0:DH] / merged[DH:DH + 1]

    span = WINDOW + tq
    win0 = pl.multiple_of(jnp.maximum(t0 - WINDOW, 0), tq)
    kw_pos = win0 + lax.broadcasted_iota(I32, (span, 1), 0)
    wvis = (kw_pos <= q_pos4) & (q_pos4 - kw_pos < WINDOW)
    s = jnp.where(wvis, _dot(kw_ref[pl.ds(win0, span), :], qt), NEG)
    p = jnp.exp2(s - jnp.max(s, axis=0, keepdims=True)).astype(BF16)
    r = _dot(vwt_ref[:, pl.ds(win0, span)], p)
    o_win = r[0:DH] / r[DH:DH + 1]

    gt = gt_ref[...]
    gate = lambda c: jnp.concatenate([gt[g * N_GATE + c:g * N_GATE + c + 1, :] for g in range(GQA)], axis=1)
    o = gate(0) * o_cmp + gate(1) * o_sel + gate(2) * o_win
    for g in range(GQA):
        o_ref[:, g * DH:(g + 1) * DH] = o[:, g * tq:(g + 1) * tq].T


def _values_with_ones(v):
    m = v.shape[0]
    vt = v.T.reshape(N_KV, DH, m)
    return jnp.concatenate([vt, jnp.ones((N_KV, VROWS - DH, m), v.dtype)], axis=1).reshape(N_KV * VROWS, m)


def _nsa_prompt(qt, kcc, vcct, ksb, vst, kwb, vwt, gates_t, batch, tq=128, tk=512):
    m = qt.shape[1]
    t = m // batch
    tk = min(tk, t)
    nt = t // tq
    n_cmp = kcc.shape[2]
    n_sel = max(t // SEL_BLOCK, 8)
    assert t % tk == 0 and tk % tq == 0 and t >= WINDOW + tq and tq % LANES == 0
    rows = lambda: pl.BlockSpec((t, DH), lambda b, k, i: (b, k))
    cols = lambda: pl.BlockSpec((VROWS, t), lambda b, k, i: (k, b))
    return pl.pallas_call(
        functools.partial(_nsa_prompt_body, tq=tq, tk=tk, n_sel=n_sel),
        grid=(batch, N_KV, nt),
        in_specs=[pl.BlockSpec((GQA * DH, tq), lambda b, k, i: (k, b * nt + i)),
                  pl.BlockSpec((1, 1, n_cmp, DH), lambda b, k, i: (b, k, 0, 0)),
                  pl.BlockSpec((1, 1, DH, n_cmp), lambda b, k, i: (b, k, 0, 0)),
                  rows(), cols(), rows(), cols(),
                  pl.BlockSpec((GATE_ROWS, tq), lambda b, k, i: (k, b * nt + i))],
        out_specs=pl.BlockSpec((tq, GQA * DH), lambda b, k, i: (b * nt + i, k)),
        out_shape=jax.ShapeDtypeStruct((m, ATTN_W), F32),
        scratch_shapes=[pltpu.VMEM((2, VROWS, GQA * tq), F32), pltpu.VMEM((n_sel, tq), F32)],
        compiler_params=_params("arbitrary", "arbitrary", "arbitrary"),
        name="nsa_prompt",
    )(qt, kcc, vcct, ksb, vst, kwb, vwt, gates_t)


def _ln_silu_rms(y, lng_ref, lnb_ref, gn_ref):
    mu = jnp.mean(y, axis=-1, keepdims=True)
    var = jnp.mean(jnp.square(y - mu), axis=-1, keepdims=True)
    y = (y - mu) * lax.rsqrt(var + EPS) * lng_ref[...] + lnb_ref[...]
    y = y * jax.nn.sigmoid(y)
    return (_rms(y) * gn_ref[...]).astype(BF16)


def _mix_out_prompt_body(oa_ref, u_ref, halo_ref, x_ref, cw_ref, cb_ref, lng_ref, lnb_ref, ga_ref, gc_ref,
                         woa_ref, woc_ref, o_ref, ext_scr, conv_scr, *, tm, tiles_per_seq, rc):
    i = pl.program_id(0)
    pad = halo_ref.shape[0]
    first = (i % tiles_per_seq) == 0
    ext_scr[0:pad, :] = jnp.where(first, 0.0, halo_ref[...])
    ext_scr[pad:, :] = u_ref[...]
    off = pad - (CONV_W - 1)
    span = rc + pad

    def chunk(r, carry):
        base = pl.multiple_of(r * rc, rc)
        window = ext_scr[pl.ds(base, span), :]
        acc = jnp.zeros((rc, CONV_CH), F32) + cb_ref[...]
        for res in range(8):
            shifted = window if res == 0 else pltpu.roll(window, span - res, axis=0)
            for k in range(CONV_W):
                if (k + off) % 8 == res:
                    a8 = k + off - res
                    acc = acc + cw_ref[k:k + 1, :] * shifted[a8:a8 + rc]
        conv_scr[pl.ds(base, rc), :] = acc
        return carry

    lax.fori_loop(0, tm // rc, chunk, 0)
    conv_n = _ln_silu_rms(conv_scr[...], lng_ref, lnb_ref, gc_ref)
    attn_n = (_rms(oa_ref[...]) * ga_ref[...]).astype(BF16)
    half = D_MODEL // 2
    for c in range(2):
        cs = slice(c * half, (c + 1) * half)
        o_ref[:, cs] = x_ref[:, cs] + _dot(attn_n, woa_ref[:, cs]) + _dot(conv_n, woc_ref[:, cs])


def _mix_out_prompt(o_attn, u, x, mw, batch, tm=256, rc=32):
    m = x.shape[0]
    t = m // batch
    tm = min(tm, t)
    pad = 32
    row = lambda w: pl.BlockSpec((tm, w), lambda i: (i, 0))
    halo = pl.BlockSpec((pad, CONV_CH), lambda i: (jnp.maximum(i * (tm // pad) - 1, 0), 0))
    return pl.pallas_call(
        functools.partial(_mix_out_prompt_body, tm=tm, tiles_per_seq=t // tm, rc=rc),
        grid=(m // tm,),
        in_specs=[row(ATTN_W), row(CONV_CH), halo, row(D_MODEL)] + [_const_spec(w.shape) for w in mw],
        out_specs=row(D_MODEL),
        out_shape=jax.ShapeDtypeStruct((m, D_MODEL), F32),
        scratch_shapes=[pltpu.VMEM((tm + pad, CONV_CH), F32), pltpu.VMEM((tm, CONV_CH), F32)],
        compiler_params=_params("arbitrary"),
        name="mix_out_prompt",
    )(o_attn, u, u, x, *mw)


def _mem_kv_body(mem_ref, g_ref, wk_ref, wv_ref, k_ref, v_ref):
    mb = (_rms(mem_ref[...]) * g_ref[...]).astype(BF16)
    k_ref[...] = _dot(mb, wk_ref[...])
    v_ref[...] = _dot(mb, wv_ref[...])


def _mem_kv(mem, g, wk, wv, tm=256):
    m = mem.shape[0]
    row = lambda w: pl.BlockSpec((tm, w), lambda i: (i, 0))
    return pl.pallas_call(
        _mem_kv_body,
        grid=(m // tm,),
        in_specs=[row(D_MODEL), _const_spec(g.shape), _const_spec(wk.shape), _const_spec(wv.shape)],
        out_specs=[row(MEM_W), row(MEM_W)],
        out_shape=[jax.ShapeDtypeStruct((m, MEM_W), F32)] * 2,
        compiler_params=_params("arbitrary"),
        name="mem_kv",
    )(mem, g, wk, wv)


def _mem_attn_core(q, mem_k, mem_v):
    outs = []
    for h in range(MEM_HEADS):
        s = _dot_nt(q[:, h * MEM_DH:(h + 1) * MEM_DH].astype(BF16), mem_k(h)) * MEM_SCALE
        outs.append(_softmax_av(s, mem_v(h)))
    return jnp.concatenate(outs, axis=1).astype(BF16)


def _mem_attn_prompt_body(x_ref, g_ref, wq_ref, mk_ref, mv_ref, wo_ref, o_ref):
    x = x_ref[...]
    hb = (_rms(x) * g_ref[...]).astype(BF16)
    head = lambda ref: lambda h: ref[0, :, h * MEM_DH:(h + 1) * MEM_DH].astype(BF16)
    a = _mem_attn_core(_dot(hb, wq_ref[...]), head(mk_ref), head(mv_ref))
    o_ref[...] = x + _dot(a, wo_ref[...])


def _mem_attn_prompt(x, g, wq, mk, mv, wo, batch, tm=256):
    m = x.shape[0]
    tiles_per_seq = m // batch // tm
    n_mem = mk.shape[1]
    row = pl.BlockSpec((tm, D_MODEL), lambda i: (i, 0))
    mem = pl.BlockSpec((1, n_mem, MEM_W), lambda i: (i // tiles_per_seq, 0, 0))
    return pl.pallas_call(
        _mem_attn_prompt_body,
        grid=(m // tm,),
        in_specs=[row, _const_spec(g.shape), _const_spec(wq.shape), mem, mem, _const_spec(wo.shape)],
        out_specs=row,
        out_shape=jax.ShapeDtypeStruct((m, D_MODEL), F32),
        compiler_params=_params("arbitrary"),
        name="mem_attn_prompt",
    )(x, g, wq, mk, mv, wo)


def _ffn_tail(f, nf, x_ref, acc_scr, gf_ref, o_ref):
    @pl.when(f == nf - 1)
    def _():
        o_ref[...] = _rms(x_ref[...] + acc_scr[...]) * gf_ref[...]


def _ffn_prompt_body(x_ref, g_ref, wg_ref, wu_ref, cw_ref, cb_ref, wd_ref, gf_ref, o_ref, st_ref,
                     h_scr, acc_scr, carry_scr, gext_scr, *, tm, tiles_per_seq):
    i = pl.program_id(0)
    f = pl.program_id(1)
    nf = pl.num_programs(1)
    hist = carry_scr.shape[1]

    @pl.when(f == 0)
    def _():
        h_scr[...] = (_rms(x_ref[...]) * g_ref[...]).astype(BF16)
        acc_scr[...] = jnp.zeros(acc_scr.shape, F32)

    hb = h_scr[...]
    gate = _dot(hb, wg_ref[...])
    first = (i % tiles_per_seq) == 0
    gext_scr[0:hist, :] = jnp.where(first, 0.0, carry_scr[f])
    gext_scr[hist:, :] = gate
    tail = gate[tm - hist:, :]
    carry_scr[f] = tail
    st_ref[0] = tail
    conv = (cw_ref[0:1, :] * gext_scr[hist - 2:hist - 2 + tm, :] + cw_ref[1:2, :] * gext_scr[hist - 1:hist - 1 + tm, :]
            + cw_ref[2:3, :] * gate + cb_ref[...])
    a = conv * jax.nn.sigmoid(conv) * _dot(hb, wu_ref[...])
    acc_scr[...] += _dot(a.astype(BF16), wd_ref[...])
    _ffn_tail(f, nf, x_ref, acc_scr, gf_ref, o_ref)


def _ffn_prompt(x, g, wg, wu, cw, cb, wd, gf, batch, tm=512, tf=512):
    m = x.shape[0]
    t = m // batch
    tm = min(tm, t)
    nf = D_FF // tf
    hist = 8
    row = pl.BlockSpec((tm, D_MODEL), lambda i, f: (i, 0))
    col = lambda r: pl.BlockSpec((r, tf), lambda i, f: (0, f))
    return pl.pallas_call(
        functools.partial(_ffn_prompt_body, tm=tm, tiles_per_seq=t // tm),
        grid=(m // tm, nf),
        in_specs=[row, _const_spec(g.shape), col(D_MODEL), col(D_MODEL), col(cw.shape[0]), col(1),
                  pl.BlockSpec((tf, D_MODEL), lambda i, f: (f, 0)), _const_spec(gf.shape)],
        out_specs=[row, pl.BlockSpec((1, hist, tf), lambda i, f: (i, 0, f))],
        out_shape=[jax.ShapeDtypeStruct((m, D_MODEL), F32), jax.ShapeDtypeStruct((m // tm, hist, D_FF), F32)],
        scratch_shapes=[pltpu.VMEM((tm, D_MODEL), BF16), pltpu.VMEM((tm, D_MODEL), F32),
                        pltpu.VMEM((nf, hist, tf), F32), pltpu.VMEM((tm + hist, tf), F32)],
        compiler_params=_params("arbitrary", "arbitrary"),
        name="ffn_prompt",
    )(x, g, wg, wu, cw, cb, wd, gf)


PAGE = 128
CHUNKS_PER_PAGE = PAGE // STRIDE_CMP


def _compress_sample_body(pt_ref, pool_ref, pe_ref, w1l_ref, w1t_ref, b1_ref, w2_ref, o_ref, buf, sem, *, n_pages):
    b = pl.program_id(0)
    slot = b % 2

    def pages(bb, sl, start):
        def one(j, carry):
            page = pt_ref[bb, j]
            for k in range(N_KV):
                cp = pltpu.make_async_copy(pool_ref.at[page, :, k, :],
                                           buf.at[sl, k, pl.ds(pl.multiple_of(j * PAGE, PAGE), PAGE), :], sem.at[sl])
                cp.start() if start else cp.wait()
            return carry
        lax.fori_loop(0, n_pages, one, 0)

    @pl.when(b == 0)
    def _():
        pages(0, 0, True)

    @pl.when(b + 1 < pl.num_programs(0))
    def _():
        pages(b + 1, 1 - slot, True)

    pages(b, slot, False)
    n = n_pages * CHUNKS_PER_PAGE
    for k in range(N_KV):
        get = lambda l, k=k: buf[slot, k, pl.ds(l, n, stride=STRIDE_CMP), :]
        o_ref[0, k] = _compress_rows(get, n, pe_ref, w1l_ref, w1t_ref, b1_ref, w2_ref)


def _compress_sample(pool, page_table, cw):
    nb, n_pages = page_table.shape
    n = n_pages * CHUNKS_PER_PAGE
    grid_spec = pltpu.PrefetchScalarGridSpec(
        num_scalar_prefetch=1,
        grid=(nb,),
        in_specs=[pl.BlockSpec(memory_space=pl.ANY)] + [_const_spec(w.shape) for w in cw],
        out_specs=pl.BlockSpec((1, N_KV, n, DH), lambda b, pt: (b, 0, 0, 0)),
        scratch_shapes=[pltpu.VMEM((2, N_KV, n_pages * PAGE, DH), F32), pltpu.SemaphoreType.DMA((2,))],
    )
    return pl.pallas_call(
        functools.partial(_compress_sample_body, n_pages=n_pages),
        grid_spec=grid_spec,
        out_shape=jax.ShapeDtypeStruct((nb, N_KV, n, DH), F32),
        compiler_params=_params("arbitrary"),
        name="compress_sample",
    )(page_table, pool, *cw)


HEAD_ROWS = 8
TOKEN_ROWS = 16


def _cmp_select_sample_body(q_ref, kcc_ref, vcc_ref, oc_ref, idx_ref, *, past_len, n_blk):
    bb = q_ref.shape[0]
    n_cmp = kcc_ref.shape[2]
    rows = GQA * HEAD_ROWS
    q_pos = past_len + lax.broadcasted_iota(I32, (rows, 1), 0) % HEAD_ROWS
    cmp_end = lax.broadcasted_iota(I32, (1, n_cmp), 1) * STRIDE_CMP + (BLOCK_CMP - 1)
    cmask = cmp_end <= q_pos
    p_sums = []
    for bi in range(bb):
        for k in range(N_KV):
            s = jnp.where(cmask, _dot_nt(q_ref[bi, k], kcc_ref[bi, k].astype(BF16)), NEG)
            e = jnp.where(cmask, jnp.exp2(s - jnp.max(s, axis=-1, keepdims=True)), 0.0)
            p = e / jnp.maximum(jnp.sum(e, axis=-1, keepdims=True), 1e-30)
            oc_ref[bi, k] = _dot(p.astype(BF16), vcc_ref[bi, k].astype(BF16))
            p_sum = p[0:HEAD_ROWS]
            for g in range(1, GQA):
                p_sum = p_sum + p[g * HEAD_ROWS:(g + 1) * HEAD_ROWS]
            p_sums.append(p_sum)
    n_rows = bb * N_KV * HEAD_ROWS
    imp = _importance(jnp.concatenate(p_sums, axis=0), _overlap_matrix(n_cmp, n_blk))
    q_pos_r = past_len + lax.broadcasted_iota(I32, (n_rows, 1), 0) % HEAD_ROWS
    blk = lax.broadcasted_iota(I32, (1, n_blk), 1)
    cur = q_pos_r // SEL_BLOCK
    forced = (blk == 0) | (blk == cur) | (blk == cur - 1)
    score = jnp.where(blk * SEL_BLOCK <= q_pos_r, jnp.where(forced, BIG, imp), -BIG)
    lane = lax.broadcasted_iota(I32, (n_rows, n_blk), 1).astype(F32)
    out_lane = lax.broadcasted_iota(I32, (n_rows, LANES), 1)
    picks = jnp.zeros((n_rows, LANES), F32)
    for n in range(TOP_N):
        best = jnp.max(score, axis=-1, keepdims=True)
        pick = jnp.min(jnp.where(score == best, lane, float(n_blk)), axis=-1, keepdims=True)
        picks = jnp.where(out_lane == n, pick, picks)
        score = jnp.where(lane == pick, -3e38, score)
    picks = picks.astype(I32)
    for r in range(bb * N_KV):
        idx_ref[r // N_KV, r % N_KV] = picks[r * HEAD_ROWS:(r + 1) * HEAD_ROWS]


def _cmp_select_sample(q_hm, kcc, vcc, past_len, bb=8):
    nb = q_hm.shape[0]
    bb = min(bb, nb)
    assert nb % bb == 0
    n_cmp = kcc.shape[2]
    n_blk = -(-(past_len // SEL_BLOCK + 1) // LANES) * LANES
    rows = GQA * HEAD_ROWS
    spec = lambda r, w: pl.BlockSpec((bb, N_KV, r, w), lambda b: (b, 0, 0, 0))
    return pl.pallas_call(
        functools.partial(_cmp_select_sample_body, past_len=past_len, n_blk=n_blk),
        grid=(nb // bb,),
        in_specs=[spec(rows, DH), spec(n_cmp, DH), spec(n_cmp, DH)],
        out_specs=[spec(rows, DH), spec(HEAD_ROWS, LANES)],
        out_shape=[jax.ShapeDtypeStruct((nb, N_KV, rows, DH), F32),
                   jax.ShapeDtypeStruct((nb, N_KV, HEAD_ROWS, LANES), I32)],
        compiler_params=_params("arbitrary"),
        name="cmp_select_sample",
    )(q_hm, kcc, vcc)


def _attn_sample_body(pt_ref, idx_s_ref, q_ref, idx_v_ref, ksn_ref, vsn_ref, kwn_ref, vwn_ref, kwin_ref, vwin_ref,
                      oc_ref, gates_ref, kpool_ref, vpool_ref, o_ref, kbuf, vbuf, sem, *, dec_seq, n_pool_blk):
    b = pl.program_id(0)
    slot = b % 2
    assert PAGE == 2 * SEL_BLOCK

    def gather(bb, sl):
        for kt in range(N_KV * dec_seq):
            k = kt // dec_seq
            for n in range(TOP_N):
                blk = idx_s_ref[bb, kt * TOP_N + n]
                src = jnp.where(blk < n_pool_blk, blk, 0)
                page = pt_ref[bb, lax.shift_right_logical(src, 1)]
                rows = pl.ds(pl.multiple_of((src & 1) * SEL_BLOCK, SEL_BLOCK), SEL_BLOCK)
                dst = pl.ds(n * SEL_BLOCK, SEL_BLOCK)
                pltpu.make_async_copy(kpool_ref.at[page, rows, k], kbuf.at[sl, kt, dst], sem.at[sl, 0]).start()
                pltpu.make_async_copy(vpool_ref.at[page, rows, k], vbuf.at[sl, kt, dst], sem.at[sl, 1]).start()

    @pl.when(b == 0)
    def _():
        gather(0, 0)

    @pl.when(b + 1 < pl.num_programs(0))
    def _():
        gather(b + 1, 1 - slot)

    pltpu.make_async_copy(kbuf.at[slot], kbuf.at[slot], sem.at[slot, 0]).wait()
    pltpu.make_async_copy(vbuf.at[slot], vbuf.at[slot], sem.at[slot, 1]).wait()

    n_keys = TOP_N * SEL_BLOCK
    rows = dec_seq * TOKEN_ROWS
    key_slot = lax.broadcasted_iota(I32, (LANES, n_keys), 1) // SEL_BLOCK
    expand = (lax.broadcasted_iota(I32, (LANES, n_keys), 0) == key_slot).astype(BF16)
    new_col = lax.broadcasted_iota(I32, (1, TOKEN_ROWS), 1)
    t_row = lax.broadcasted_iota(I32, (rows, 1), 0) // TOKEN_ROWS
    n_win = kwin_ref.shape[1]
    win_old_vis = lax.broadcasted_iota(I32, (1, n_win), 1) > t_row
    win_new_vis = new_col <= t_row
    for k in range(N_KV):
        qk = q_ref[0, k]
        pool_ok = _dot((idx_v_ref[0, k] < n_pool_blk).astype(BF16), expand)
        ksn, vsn = ksn_ref[0, k], vsn_ref[0, k]
        s_old = jnp.where(win_old_vis, _dot_nt(qk, kwin_ref[0, :, k, :].astype(BF16)), NEG)
        s_new = jnp.where(win_new_vis, _dot_nt(qk, kwn_ref[0, k]), NEG)
        m = jnp.maximum(jnp.max(s_old, axis=-1, keepdims=True), jnp.max(s_new, axis=-1, keepdims=True))
        p_old, p_new = jnp.exp2(s_old - m), jnp.exp2(s_new - m)
        l = jnp.sum(p_old, axis=-1, keepdims=True) + jnp.sum(p_new, axis=-1, keepdims=True)
        o_win = (_dot(p_old.astype(BF16), vwin_ref[0, :, k, :].astype(BF16)) + _dot(p_new.astype(BF16), vwn_ref[0, k])) / l
        gates = gates_ref[0, k]
        for t in range(dec_seq):
            ts = slice(t * TOKEN_ROWS, (t + 1) * TOKEN_ROWS)
            kt = k * dec_seq + t
            qt = qk[ts]
            s_pool = jnp.where(pool_ok[t:t + 1, :] > 0.5, _dot_nt(qt, kbuf[slot, kt].astype(BF16)), NEG)
            s_cur = jnp.where(new_col <= t, _dot_nt(qt, ksn), NEG)
            m = jnp.maximum(jnp.max(s_pool, axis=-1, keepdims=True), jnp.max(s_cur, axis=-1, keepdims=True))
            p_pool, p_cur = jnp.exp2(s_pool - m), jnp.exp2(s_cur - m)
            l = jnp.sum(p_pool, axis=-1, keepdims=True) + jnp.sum(p_cur, axis=-1, keepdims=True)
            o_sel = (_dot(p_pool.astype(BF16), vbuf[slot, kt].astype(BF16)) + _dot(p_cur.astype(BF16), vsn)) / l
            gt = gates[ts]
            o_ref[0, k, ts, :] = (gt[:, 0:1] * oc_ref[0, k, ts, :] + gt[:, 1:2] * o_sel + gt[:, 2:3] * o_win[ts])


def _attn_sample(page_table, idx_flat, q_tm, idx_pad, ksn, vsn, kwn, vwn, k_win, v_win, oc_tm, gates_tm,
                 k_pool, v_pool, dec_seq, past_len):
    nb = q_tm.shape[0]
    rows = dec_seq * TOKEN_ROWS
    n_win = k_win.shape[1]
    spec = lambda r, w: pl.BlockSpec((1, N_KV, r, w), lambda b, *_: (b, 0, 0, 0))
    win = pl.BlockSpec((1, n_win, N_KV, DH), lambda b, *_: (b, 0, 0, 0))
    any_spec = pl.BlockSpec(memory_space=pl.ANY)
    grid_spec = pltpu.PrefetchScalarGridSpec(
        num_scalar_prefetch=2,
        grid=(nb,),
        in_specs=[spec(rows, DH), spec(TOKEN_ROWS, LANES), spec(TOKEN_ROWS, DH), spec(TOKEN_ROWS, DH),
                  spec(TOKEN_ROWS, DH), spec(TOKEN_ROWS, DH), win, win, spec(rows, DH), spec(rows, LANES),
                  any_spec, any_spec],
        out_specs=spec(rows, DH),
        scratch_shapes=[pltpu.VMEM((2, N_KV * dec_seq, TOP_N * SEL_BLOCK, DH), F32),
                        pltpu.VMEM((2, N_KV * dec_seq, TOP_N * SEL_BLOCK, DH), F32),
                        pltpu.SemaphoreType.DMA((2, 2))],
    )
    return pl.pallas_call(
        functools.partial(_attn_sample_body, dec_seq=dec_seq, n_pool_blk=past_len // SEL_BLOCK),
        grid_spec=grid_spec,
        out_shape=jax.ShapeDtypeStruct((nb, N_KV, rows, DH), F32),
        compiler_params=_params("arbitrary"),
        name="attn_sample",
    )(page_table, idx_flat, q_tm, idx_pad, ksn, vsn, kwn, vwn, k_win, v_win, oc_tm, gates_tm, k_pool, v_pool)


def _mix_out_sample_body(oa_ref, u_ref, st_ref, x_ref, cw_ref, cb_ref, lng_ref, lnb_ref, ga_ref, gc_ref,
                         woa_ref, woc_ref, gm_ref, wmq_ref, o_ref, qm_ref, conv_scr, *, nb, dec_seq):
    hist = CONV_W - 1

    def ext(j):
        if j < hist:
            return st_ref[j * nb:(j + 1) * nb, :]
        return u_ref[(j - hist) * nb:(j - hist + 1) * nb, :]

    for t in range(dec_seq):
        acc = jnp.zeros((nb, CONV_CH), F32) + cb_ref[...]
        for k in range(CONV_W):
            acc = acc + cw_ref[k:k + 1, :] * ext(t + k)
        conv_scr[t * nb:(t + 1) * nb, :] = acc
    conv_n = _ln_silu_rms(conv_scr[...], lng_ref, lnb_ref, gc_ref)
    attn_n = (_rms(oa_ref[...]) * ga_ref[...]).astype(BF16)
    x1 = x_ref[...] + _dot(attn_n, woa_ref[...]) + _dot(conv_n, woc_ref[...])
    o_ref[...] = x1
    qm_ref[...] = _dot((_rms(x1) * gm_ref[...]).astype(BF16), wmq_ref[...])


def _mix_out_sample(o_attn, u, conv_state, x, mw, g_mem, w_mq, nb, dec_seq):
    m = x.shape[0]
    args = (o_attn, u, conv_state, x) + tuple(mw) + (g_mem, w_mq)
    return pl.pallas_call(
        functools.partial(_mix_out_sample_body, nb=nb, dec_seq=dec_seq),
        grid=(1,),
        in_specs=[_const_spec(a.shape) for a in args],
        out_specs=[_const_spec((m, D_MODEL)), _const_spec((m, MEM_W))],
        out_shape=[jax.ShapeDtypeStruct((m, D_MODEL), F32), jax.ShapeDtypeStruct((m, MEM_W), F32)],
        scratch_shapes=[pltpu.VMEM((m, CONV_CH), F32)],
        compiler_params=_params("arbitrary"),
        name="mix_out_sample",
    )(*args)


def _mem_attn_sample_body(q_ref, mk_ref, mv_ref, o_ref):
    head = lambda ref: lambda h: ref[0, :, h, :].astype(BF16)
    o_ref[0] = _mem_attn_core(q_ref[0], head(mk_ref), head(mv_ref))


def _mem_attn_sample(q_pad, mk, mv):
    nb, rows, _ = q_pad.shape
    n_mem = mk.shape[1]
    q_spec = pl.BlockSpec((1, rows, MEM_W), lambda b: (b, 0, 0))
    mem = pl.BlockSpec((1, n_mem, MEM_HEADS, MEM_DH), lambda b: (b, 0, 0, 0))
    return pl.pallas_call(
        _mem_attn_sample_body,
        grid=(nb,),
        in_specs=[q_spec, mem, mem],
        out_specs=q_spec,
        out_shape=jax.ShapeDtypeStruct((nb, rows, MEM_W), BF16),
        compiler_params=_params("arbitrary"),
        name="mem_attn_sample",
    )(q_pad, mk, mv)


def _ffn_sample_body(x_ref, a_ref, wo_ref, g_ref, wg_ref, wu_ref, cw_ref, cb_ref, wd_ref, gf_ref, st_ref,
                     o_ref, sto_ref, x2_scr, h_scr, acc_scr, gate_scr, conv_scr, *, nb, dec_seq):
    f = pl.program_id(0)
    hist = FFN_CONV_W - 1

    @pl.when(f == 0)
    def _():
        x2 = x_ref[...] + _dot(a_ref[...], wo_ref[...])
        x2_scr[...] = x2
        h_scr[...] = (_rms(x2) * g_ref[...]).astype(BF16)
        acc_scr[...] = jnp.zeros(acc_scr.shape, F32)

    hb = h_scr[...]
    gate_scr[...] = _dot(hb, wg_ref[...])

    def ext(j):
        if j < hist:
            return st_ref[j * nb:(j + 1) * nb, :]
        return gate_scr[(j - hist) * nb:(j - hist + 1) * nb, :]

    for t in range(dec_seq):
        acc = cb_ref[...] + cw_ref[0:1, :] * ext(t)
        for k in range(1, FFN_CONV_W):
            acc = acc + cw_ref[k:k + 1, :] * ext(t + k)
        conv_scr[t * nb:(t + 1) * nb, :] = acc
    for j in range(hist):
        sto_ref[j * nb:(j + 1) * nb, :] = ext(dec_seq + j)
    conv = conv_scr[...]
    a = conv * jax.nn.sigmoid(conv) * _dot(hb, wu_ref[...])
    acc_scr[...] += _dot(a.astype(BF16), wd_ref[...])
    _ffn_tail(f, pl.num_programs(0), x2_scr, acc_scr, gf_ref, o_ref)


def _ffn_sample(x1, a, w_mo, fw, ffn_state, nb, dec_seq, tf=512):
    g, wg, wu, cw, cb, wd, gf = fw
    m = x1.shape[0]
    nf = D_FF // tf
    hist = FFN_CONV_W - 1
    col = lambda r: pl.BlockSpec((r, tf), lambda f: (0, f))
    full = lambda shape: pl.BlockSpec(shape, lambda f: (0,) * len(shape))
    return pl.pallas_call(
        functools.partial(_ffn_sample_body, nb=nb, dec_seq=dec_seq),
        grid=(nf,),
        in_specs=[full(x1.shape), full(a.shape), full(w_mo.shape), full(g.shape), col(D_MODEL), col(D_MODEL),
                  col(cw.shape[0]), col(1), pl.BlockSpec((tf, D_MODEL), lambda f: (f, 0)), full(gf.shape),
                  col(nb * hist)],
        out_specs=[full((m, D_MODEL)), col(nb * hist)],
        out_shape=[jax.ShapeDtypeStruct((m, D_MODEL), F32), jax.ShapeDtypeStruct((nb * hist, D_FF), F32)],
        scratch_shapes=[pltpu.VMEM((m, D_MODEL), F32), pltpu.VMEM((m, D_MODEL), BF16), pltpu.VMEM((m, D_MODEL), F32),
                        pltpu.VMEM((m, tf), F32), pltpu.VMEM((m, tf), F32)],
        compiler_params=_params("arbitrary"),
        name="ffn_sample",
    )(x1, a, w_mo, g, wg, wu, cw, cb, wd, gf, ffn_state)


def _prepare_weights(norm_mix_g, w_in, b_gate, cmp_k, cmp_v, conv_w, conv_b, conv_ln_g, conv_ln_b,
                     grp_norm_attn_g, grp_norm_conv_g, w_out, norm_mem_g, mem_norm_g, w_mq, w_mk, w_mv, w_mo,
                     norm_ffn_g, w_ffn_gate, w_ffn_up, ffn_conv_w, ffn_conv_b, w_ffn_down, norm_final_g):
    vec = lambda v: v.reshape(1, -1)
    kv_end = ATTN_W + 6 * KV_W
    n_gate_cols = N_KV * GQA * N_GATE
    per_kv = GQA * N_GATE
    wg = w_in[:, kv_end:kv_end + n_gate_cols].reshape(D_MODEL, N_KV, per_kv)
    wg = jnp.pad(wg, ((0, 0), (0, 0), (0, LANES - per_kv))).reshape(D_MODEL, N_KV * LANES)
    bg = jnp.pad(b_gate.reshape(N_KV, per_kv), ((0, 0), (0, LANES - per_kv))).reshape(1, N_KV * LANES)
    return dict(
        in_proj=(vec(norm_mix_g), w_in[:, :ATTN_W].astype(BF16), w_in[:, ATTN_W:kv_end].astype(BF16),
                 wg.astype(BF16), bg, w_in[:, kv_end + n_gate_cols:].astype(BF16)),
        cmp_k=_compress_weights(*cmp_k),
        cmp_v=_compress_weights(*cmp_v),
        mix=(jnp.pad(conv_w, ((0, 32 - CONV_W), (0, 0))), vec(conv_b), vec(conv_ln_g), vec(conv_ln_b),
             vec(grp_norm_attn_g), vec(grp_norm_conv_g), w_out[:ATTN_W].astype(BF16), w_out[ATTN_W:].astype(BF16)),
        mem_kv=(vec(mem_norm_g), w_mk.astype(BF16), w_mv.astype(BF16)),
        mem=(vec(norm_mem_g), w_mq.astype(BF16), w_mo.astype(BF16)),
        ffn=(vec(norm_ffn_g), w_ffn_gate.astype(BF16), w_ffn_up.astype(BF16),
             jnp.pad(ffn_conv_w, ((0, 8 - FFN_CONV_W), (0, 0))), vec(ffn_conv_b), w_ffn_down.astype(BF16),
             vec(norm_final_g)),
    )


def _prompt_forward(x_prompt, mem_prompt, w):
    batch, t, _ = x_prompt.shape
    x = x_prompt.reshape(batch * t, D_MODEL)
    (q, kc, vc, ks, vs, kw, vw, ksb, vsb, kwb, vwb, gates, u) = _in_proj(x, *w["in_proj"], tm=256)
    kcc = _compress_prompt(kc, w["cmp_k"], batch)
    vcc = _compress_prompt(vc, w["cmp_v"], batch)
    gates_t = gates.reshape(batch * t, N_KV, LANES)[:, :, :GATE_ROWS].transpose(1, 2, 0)
    o_attn = _nsa_prompt(q.T, kcc, vcc.transpose(0, 1, 3, 2), ksb, _values_with_ones(vsb), kwb, _values_with_ones(vwb),
                         gates_t.reshape(N_KV * GATE_ROWS, batch * t), batch)
    x1 = _mix_out_prompt(o_attn, u, x, w["mix"], batch)
    n_mem = mem_prompt.shape[1]
    mk, mv = _mem_kv(mem_prompt.reshape(batch * n_mem, D_MODEL), *w["mem_kv"])
    g_mem, w_mq, w_mo = w["mem"]
    x2 = _mem_attn_prompt(x1, g_mem, w_mq, mk.reshape(batch, n_mem, MEM_W), mv.reshape(batch, n_mem, MEM_W),
                          w_mo, batch)
    y, ffn_tail = _ffn_prompt(x2, *w["ffn"], batch=batch)
    kv5 = lambda a: a.reshape(1, batch, t, N_KV, DH)
    win = lambda a: a.reshape(batch, t, N_KV, DH)[None, :, t - min(WINDOW, t):]
    tiles = ffn_tail.shape[0] // batch
    new_ffn = ffn_tail.reshape(batch, tiles, ffn_tail.shape[1], D_FF)[:, -1, -(FFN_CONV_W - 1):]
    new_conv = u.reshape(batch, t, CONV_CH)[:, t - (CONV_W - 1):]
    mem5 = lambda a: a.reshape(1, batch, n_mem, MEM_HEADS, MEM_DH)
    return (y.reshape(batch, t, D_MODEL), kv5(kc), kv5(vc), kv5(ks), kv5(vs), win(kw), win(vw),
            new_conv[None], new_ffn[None], mem5(mk), mem5(mv))


def _pad_axis(a, axis, size):
    pads = [(0, 0)] * a.ndim
    pads[axis] = (0, size - a.shape[axis])
    return jnp.pad(a, pads)


def _sample_forward(x_sample, pools, k_win, v_win, conv_state, ffn_state, mem_k, mem_v, page_table, w):
    nb, dec_seq, _ = x_sample.shape
    m = nb * dec_seq
    past_len = page_table.shape[1] * PAGE
    assert dec_seq <= HEAD_ROWS and k_win.shape[1] == WINDOW
    x = x_sample.reshape(m, D_MODEL)
    (q, kc, vc, ks, vs, kw, vw, ksb, vsb, kwb, vwb, gates, u) = _in_proj(x, *w["in_proj"], tm=m)
    pool_kc, pool_vc, pool_ks, pool_vs = pools
    kcc = _compress_sample(pool_kc, page_table, w["cmp_k"])
    vcc = _compress_sample(pool_vc, page_table, w["cmp_v"])

    q5 = q.reshape(nb, dec_seq, N_KV, GQA, DH)
    q_hm = _pad_axis(q5.transpose(0, 2, 3, 1, 4), 3, HEAD_ROWS).reshape(nb, N_KV, GQA * HEAD_ROWS, DH)
    q_tm = _pad_axis(q5.transpose(0, 2, 1, 3, 4), 3, TOKEN_ROWS).reshape(nb, N_KV, dec_seq * TOKEN_ROWS, DH)
    oc_hm, idx = _cmp_select_sample(q_hm, kcc, vcc, past_len)
    oc_tm = oc_hm.reshape(nb, N_KV, GQA, HEAD_ROWS, DH)[:, :, :, :dec_seq].transpose(0, 1, 3, 2, 4)
    oc_tm = _pad_axis(oc_tm, 3, TOKEN_ROWS).reshape(nb, N_KV, dec_seq * TOKEN_ROWS, DH)
    idx_flat = idx[:, :, :dec_seq, :TOP_N].reshape(nb, N_KV * dec_seq * TOP_N)
    idx_pad = _pad_axis(idx, 2, TOKEN_ROWS)
    new_rows = lambda a: _pad_axis(a.reshape(nb, dec_seq, N_KV, DH).transpose(0, 2, 1, 3), 2, TOKEN_ROWS)
    gates_tm = gates.reshape(nb, dec_seq, N_KV, LANES)[..., :GQA * N_GATE].reshape(nb, dec_seq, N_KV, GQA, N_GATE)
    gates_tm = _pad_axis(_pad_axis(gates_tm.transpose(0, 2, 1, 3, 4), 3, TOKEN_ROWS), 4, LANES)
    gates_tm = gates_tm.reshape(nb, N_KV, dec_seq * TOKEN_ROWS, LANES)
    o_tm = _attn_sample(page_table, idx_flat, q_tm, idx_pad, new_rows(ksb), new_rows(vsb), new_rows(kwb),
                        new_rows(vwb), k_win, v_win,
                        oc_tm, gates_tm, pool_ks, pool_vs, dec_seq, past_len)
    o_attn = o_tm.reshape(nb, N_KV, dec_seq, TOKEN_ROWS, DH)[:, :, :, :GQA].transpose(2, 0, 1, 3, 4).reshape(m, ATTN_W)
    step_major = lambda a: a.reshape(nb, -1, a.shape[-1]).transpose(1, 0, 2).reshape(-1, a.shape[-1])
    batch_major = lambda a: a.reshape(-1, nb, a.shape[-1]).transpose(1, 0, 2)

    g_mem, w_mq, w_mo = w["mem"]
    x1, qm = _mix_out_sample(o_attn, step_major(u), step_major(conv_state), step_major(x), w["mix"], g_mem, w_mq,
                             nb, dec_seq)
    n_mem = mem_k.shape[1]
    q_pad = _pad_axis(batch_major(qm), 1, TOKEN_ROWS).astype(BF16)
    a = _mem_attn_sample(q_pad, mem_k, mem_v)
    y, new_ffn = _ffn_sample(x1, step_major(a[:, :dec_seq]), w_mo, w["ffn"], step_major(ffn_state), nb, dec_seq)

    kv5 = lambda a: a.reshape(1, nb, dec_seq, N_KV, DH)
    shift = lambda buf, new: jnp.concatenate([buf[:, dec_seq:], new.reshape((nb, dec_seq) + buf.shape[2:])], axis=1)[None]
    return (batch_major(y), kv5(kc), kv5(vc), kv5(ks), kv5(vs), shift(k_win, kw), shift(v_win, vw),
            shift(conv_state, u), batch_major(new_ffn)[None])


def kernel(x_prompt, x_sample, cache_k_cmp, cache_v_cmp, cache_k_sel, cache_v_sel, cache_k_win, cache_v_win,
           state_conv, state_ffn_conv, cache_mem_k, cache_mem_v, page_table, mem_prompt,
           norm_mix_g, w_in, b_gate, cmp_k_pe, cmp_k_w1, cmp_k_b1, cmp_k_w2, cmp_v_pe, cmp_v_w1, cmp_v_b1, cmp_v_w2,
           conv_w, conv_b, conv_ln_g, conv_ln_b, grp_norm_attn_g, grp_norm_conv_g, w_out,
           norm_mem_g, mem_norm_g, w_mq, w_mk, w_mv, w_mo,
           norm_ffn_g, w_ffn_gate, w_ffn_up, ffn_conv_w, ffn_conv_b, w_ffn_down, norm_final_g):
    assert w_in.shape[0] == 1, "single-layer step"
    w = _prepare_weights(norm_mix_g[0], w_in[0], b_gate[0],
                         (cmp_k_pe[0], cmp_k_w1[0], cmp_k_b1[0], cmp_k_w2[0]),
                         (cmp_v_pe[0], cmp_v_w1[0], cmp_v_b1[0], cmp_v_w2[0]),
                         conv_w[0], conv_b[0], conv_ln_g[0], conv_ln_b[0], grp_norm_attn_g[0], grp_norm_conv_g[0],
                         w_out[0], norm_mem_g[0], mem_norm_g[0], w_mq[0], w_mk[0], w_mv[0], w_mo[0],
                         norm_ffn_g[0], w_ffn_gate[0], w_ffn_up[0], ffn_conv_w[0], ffn_conv_b[0], w_ffn_down[0],
                         norm_final_g)
    p = _prompt_forward(x_prompt, mem_prompt, w)
    s = _sample_forward(x_sample, (cache_k_cmp[0], cache_v_cmp[0], cache_k_sel[0], cache_v_sel[0]),
                        cache_k_win[0], cache_v_win[0], state_conv[0], state_ffn_conv[0],
                        cache_mem_k[0], cache_mem_v[0], page_table, w)
    return (p[0], s[0]) + p[1:] + s[1:]
```

```python
import functools

import jax
import jax.numpy as jnp
from jax import lax
from jax.experimental import pallas as pl
from jax.experimental.pallas import tpu as pltpu

F32 = jnp.float32
BF16 = jnp.bfloat16
I32 = jnp.int32

D_MODEL = 2048
N_KV = 2
GQA = 4
DH = 128
ATTN_W = N_KV * GQA * DH
KV_W = N_KV * DH
N_GATE = 3
BLOCK_CMP = 32
STRIDE_CMP = 16
CMP_HID = 256
SEL_BLOCK = 64
TOP_N = 16
WINDOW = 512
CONV_CH = D_MODEL - ATTN_W
CONV_W = 31
D_FF = 5632
FFN_CONV_W = 3
MEM_HEADS = 4
MEM_DH = 128
MEM_W = MEM_HEADS * MEM_DH
ATTN_SCALE = DH ** -0.5
Q_PRESCALE = ATTN_SCALE * 1.4426950408889634
MEM_SCALE = MEM_DH ** -0.5
EPS = 1e-6
NEG = -1e30
BIG = 1e30
LANES = 128
VMEM_LIMIT = 56 * 1024 * 1024


def _dot(a, b):
    return jnp.dot(a, b, preferred_element_type=F32)


def _dot_nt(a, b):
    return lax.dot_general(a, b, (((1,), (1,)), ((), ())), preferred_element_type=F32)


def _rms(x):
    return x * lax.rsqrt(jnp.mean(x * x, axis=-1, keepdims=True) + EPS)


def _const_spec(shape):
    return pl.BlockSpec(shape, lambda *_: (0,) * len(shape), pipeline_mode=pl.Buffered(1))


def _params(*sem):
    return pltpu.CompilerParams(dimension_semantics=sem, vmem_limit_bytes=VMEM_LIMIT)


def _in_proj_body(x_ref, g_ref, wq_ref, wkv_ref, wg_ref, bg_ref, wglu_ref,
                  q_ref, kc_ref, vc_ref, ks_ref, vs_ref, kw_ref, vw_ref,
                  ksb_ref, vsb_ref, kwb_ref, vwb_ref, gates_ref, u_ref):
    hb = (_rms(x_ref[...]) * g_ref[...]).astype(BF16)
    half = ATTN_W // 2
    for c in range(2):
        q_ref[:, c * half:(c + 1) * half] = (_dot(hb, wq_ref[:, c * half:(c + 1) * half]) * Q_PRESCALE).astype(BF16)
    f32_outs = (kc_ref, vc_ref, ks_ref, vs_ref, kw_ref, vw_ref)
    bf_outs = (None, None, ksb_ref, vsb_ref, kwb_ref, vwb_ref)
    for c in range(6):
        r = _dot(hb, wkv_ref[:, c * KV_W:(c + 1) * KV_W])
        for k in range(N_KV):
            f32_outs[c][:, k, :] = r[:, k * DH:(k + 1) * DH]
        if bf_outs[c] is not None:
            bf_outs[c][...] = r.astype(BF16)
    gates_ref[...] = jax.nn.sigmoid(_dot(hb, wg_ref[...]) + bg_ref[...])
    cw = 256
    for c in range(CONV_CH // cw):
        a = _dot(hb, wglu_ref[:, c * cw:(c + 1) * cw])
        gt = _dot(hb, wglu_ref[:, CONV_CH + c * cw:CONV_CH + (c + 1) * cw])
        u_ref[:, c * cw:(c + 1) * cw] = a * jax.nn.sigmoid(gt)


def _in_proj(x, g, wq, wkv, wg, bg, wglu, tm):
    m = x.shape[0]
    row = lambda w: pl.BlockSpec((tm, w), lambda i: (i, 0))
    out_shape = ([jax.ShapeDtypeStruct((m, ATTN_W), BF16)]
                 + [jax.ShapeDtypeStruct((m, N_KV, DH), F32)] * 6
                 + [jax.ShapeDtypeStruct((m, KV_W), BF16)] * 4
                 + [jax.ShapeDtypeStruct((m, N_KV * LANES), F32),
                    jax.ShapeDtypeStruct((m, CONV_CH), F32)])
    state = pl.BlockSpec((tm, N_KV, DH), lambda i: (i, 0, 0))
    out_specs = ([row(ATTN_W)] + [state] * 6 + [row(KV_W)] * 4 + [row(N_KV * LANES), row(CONV_CH)])
    return pl.pallas_call(
        _in_proj_body,
        grid=(m // tm,),
        in_specs=[row(D_MODEL), _const_spec(g.shape), _const_spec(wq.shape), _const_spec(wkv.shape),
                  _const_spec(wg.shape), _const_spec(bg.shape), _const_spec(wglu.shape)],
        out_specs=out_specs,
        out_shape=out_shape,
        compiler_params=_params("arbitrary"),
        name="in_proj",
    )(x, g, wq, wkv, wg, bg, wglu)


def _compress_rows(get_lanes, n, pe_ref, w1l_ref, w1t_ref, b1_ref, w2_ref):
    xk = jnp.concatenate([get_lanes(l) for l in range(STRIDE_CMP)], axis=1)
    lead = _dot((xk + pe_ref[0:1, :]).astype(BF16), w1l_ref[...])
    trail = _dot((xk + pe_ref[1:2, :]).astype(BF16), w1t_ref[...])
    trail_next = pltpu.roll(trail, n - 1, axis=0)
    hid = jax.nn.gelu(lead + trail_next + b1_ref[...])
    out = _dot(hid.astype(BF16), w2_ref[...])
    rows = lax.broadcasted_iota(I32, (n, 1), 0)
    return jnp.where(rows < n - 1, out, 0.0)


def _compress_prompt_body(x_ref, pe_ref, w1l_ref, w1t_ref, b1_ref, w2_ref, o_ref):
    n = x_ref.shape[0] // STRIDE_CMP
    for k in range(N_KV):
        get = lambda l, k=k: x_ref[pl.ds(l, n, stride=STRIDE_CMP), k, :]
        o_ref[0, k] = _compress_rows(get, n, pe_ref, w1l_ref, w1t_ref, b1_ref, w2_ref)


def _compress_weights(pe, w1, b1, w2):
    half = STRIDE_CMP * DH
    pe2 = pe.reshape(2, half)
    return pe2, w1[:half].astype(BF16), w1[half:].astype(BF16), b1.reshape(1, CMP_HID), w2.astype(BF16)


def _compress_prompt(rows, cw, batch):
    t = rows.shape[0] // batch
    n = t // STRIDE_CMP
    return pl.pallas_call(
        _compress_prompt_body,
        grid=(batch,),
        in_specs=[pl.BlockSpec((t, N_KV, DH), lambda b: (b, 0, 0))] + [_const_spec(w.shape) for w in cw],
        out_specs=pl.BlockSpec((1, N_KV, n, DH), lambda b: (b, 0, 0, 0)),
        out_shape=jax.ShapeDtypeStruct((batch, N_KV, n, DH), F32),
        compiler_params=_params("arbitrary"),
        name="compress_prompt",
    )(rows, *cw)


def _overlap_matrix(n_cmp, n_blk=LANES):
    i = lax.broadcasted_iota(I32, (n_cmp, n_blk), 0) * STRIDE_CMP
    j = lax.broadcasted_iota(I32, (n_cmp, n_blk), 1) * SEL_BLOCK
    ov = jnp.maximum(jnp.minimum(i + BLOCK_CMP, j + SEL_BLOCK) - jnp.maximum(i, j), 0)
    return (ov.astype(F32) * (1.0 / BLOCK_CMP)).astype(BF16)


def _importance(p_sum, ov):
    hi = p_sum.astype(BF16)
    r1 = p_sum - hi.astype(F32)
    mid = r1.astype(BF16)
    lo = (r1 - mid.astype(F32)).astype(BF16)
    return _dot(hi, ov) + _dot(mid, ov) + _dot(lo, ov)


def _softmax_av(s, v):
    p = jnp.exp(s - jnp.max(s, axis=-1, keepdims=True))
    return _dot(p.astype(BF16), v) / jnp.sum(p, axis=-1, keepdims=True)


GATE_ROWS = 16
VROWS = DH + 16


def _nsa_prompt_body(qt_ref, kcc_ref, vcct_ref, ks_ref, vst_ref, kw_ref, vwt_ref, gt_ref, o_ref,
                     acc_scr, bias_scr, *, tq, tk, n_sel):
    i = pl.program_id(2)
    t0 = i * tq
    nq = GQA * tq
    heads = lambda a: jnp.concatenate([a] * GQA, axis=1)
    q_pos = t0 + lax.broadcasted_iota(I32, (1, tq), 1)
    q_pos4 = heads(q_pos)
    qt = jnp.concatenate([qt_ref[g * DH:(g + 1) * DH, :] for g in range(GQA)], axis=1)

    n_cmp = kcc_ref.shape[2]
    cmp_end = lax.broadcasted_iota(I32, (n_cmp, 1), 0) * STRIDE_CMP + (BLOCK_CMP - 1)
    cvis = cmp_end <= q_pos4
    s = jnp.where(cvis, _dot(kcc_ref[0, 0].astype(BF16), qt), NEG)
    e = jnp.where(cvis, jnp.exp2(s - jnp.max(s, axis=0, keepdims=True)), 0.0)
    p = e / jnp.maximum(jnp.sum(e, axis=0, keepdims=True), 1e-30)
    o_cmp = _dot(vcct_ref[0, 0].astype(BF16), p.astype(BF16))
    p_sum = p[:, 0:tq]
    for g in range(1, GQA):
        p_sum = p_sum + p[:, g * tq:(g + 1) * tq]

    ov_i = lax.broadcasted_iota(I32, (n_sel, n_cmp), 1) * STRIDE_CMP
    ov_j = lax.broadcasted_iota(I32, (n_sel, n_cmp), 0) * SEL_BLOCK
    ov = jnp.maximum(jnp.minimum(ov_i + BLOCK_CMP, ov_j + SEL_BLOCK) - jnp.maximum(ov_i, ov_j), 0)
    ov = (ov.astype(F32) * (1.0 / BLOCK_CMP)).astype(BF16)
    hi = p_sum.astype(BF16)
    r1 = p_sum - hi.astype(F32)
    mid = r1.astype(BF16)
    lo = (r1 - mid.astype(F32)).astype(BF16)
    imp = _dot(ov, hi) + _dot(ov, mid) + _dot(ov, lo)
    blk = lax.broadcasted_iota(I32, (n_sel, tq), 0)
    cur = q_pos // SEL_BLOCK
    forced = (blk == 0) | (blk == cur) | (blk == cur - 1)
    s_t = jnp.where(blk * SEL_BLOCK <= q_pos, jnp.where(forced, BIG, imp), -BIG)
    rank = jnp.zeros((n_sel, tq), F32)
    for ib in range(n_sel):
        row = s_t[ib:ib + 1, :]
        beats = (row > s_t) | ((row == s_t) & (blk > ib))
        rank = rank + jnp.where(beats, 1.0, 0.0)
    bias_scr[...] = jnp.where(rank < TOP_N, 0.0, NEG)

    acc_scr[...] = jnp.zeros(acc_scr.shape, F32)
    blocks_per_tile = tk // SEL_BLOCK
    last_tile = ks_ref.shape[0] // tk - 1

    def key_tile(j, state, m, causal):
        jd = jnp.minimum(j, last_tile)
        k0 = pl.multiple_of(jd * tk, tk)
        bias = jnp.concatenate(
            [jnp.broadcast_to(bias_scr[pl.ds(jd * blocks_per_tile + c, 1), :], (SEL_BLOCK, tq))
             for c in range(blocks_per_tile)], axis=0)
        if causal:
            bias = jnp.where(j * tk + lax.broadcasted_iota(I32, (tk, 1), 0) <= q_pos, bias, NEG)
        s = _dot(ks_ref[pl.ds(k0, tk), :], qt) + heads(bias)
        m_new = jnp.maximum(m, jnp.max(s, axis=0, keepdims=True))
        p = jnp.exp2(s - m_new).astype(BF16)
        acc_scr[state] = jnp.exp2(m - m_new) * acc_scr[state] + _dot(vst_ref[:, pl.ds(k0, tk)], p)
        return m_new

    def tile_pair(jp, c):
        return key_tile(2 * jp, 0, c[0], False), key_tile(2 * jp + 1, 1, c[1], False)

    n_pairs = t0 // (2 * tk)
    empty = jnp.full((1, nq), NEG, F32)
    c = lax.fori_loop(0, n_pairs, tile_pair, (empty, empty))
    m0 = key_tile(2 * n_pairs, 0, c[0], True)
    second_is_live = (t0 // tk) % 2 == 1
    m1 = lax.cond(second_is_live, lambda: key_tile(2 * n_pairs + 1, 1, c[1], True), lambda: c[1])
    m = jnp.maximum(m0, m1)
    merged = jnp.exp2(m0 - m) * acc_scr[0] + jnp.exp2(m1 - m) * acc_scr[1]
    o_sel = merged[0:DH] / merged[DH:DH + 1]

    span = WINDOW + tq
    win0 = pl.multiple_of(jnp.maximum(t0 - WINDOW, 0), tq)
    kw_pos = win0 + lax.broadcasted_iota(I32, (span, 1), 0)
    wvis = (kw_pos <= q_pos4) & (q_pos4 - kw_pos < WINDOW)
    s = jnp.where(wvis, _dot(kw_ref[pl.ds(win0, span), :], qt), NEG)
    p = jnp.exp2(s - jnp.max(s, axis=0, keepdims=True)).astype(BF16)
    r = _dot(vwt_ref[:, pl.ds(win0, span)], p)
    o_win = r[0:DH] / r[DH:DH + 1]

    gt = gt_ref[...]
    gate = lambda c: jnp.concatenate([gt[g * N_GATE + c:g * N_GATE + c + 1, :] for g in range(GQA)], axis=1)
    o = gate(0) * o_cmp + gate(1) * o_sel + gate(2) * o_win
    for g in range(GQA):
        o_ref[:, g * DH:(g + 1) * DH] = o[:, g * tq:(g + 1) * tq].T


def _values_with_ones(v):
    m = v.shape[0]
    vt = v.T.reshape(N_KV, DH, m)
    return jnp.concatenate([vt, jnp.ones((N_KV, VROWS - DH, m), v.dtype)], axis=1).reshape(N_KV * VROWS, m)


def _nsa_prompt(qt, kcc, vcct, ksb, vst, kwb, vwt, gates_t, batch, tq=512, tk=512):
    m = qt.shape[1]
    t = m // batch
    tk = min(tk, t)
    nt = t // tq
    n_cmp = kcc.shape[2]
    n_sel = max(t // SEL_BLOCK, 8)
    assert t % tk == 0 and tk % tq == 0 and t >= WINDOW + tq and tq % LANES == 0
    rows = lambda: pl.BlockSpec((t, DH), lambda b, k, i: (b, k))
    cols = lambda: pl.BlockSpec((VROWS, t), lambda b, k, i: (k, b))
    return pl.pallas_call(
        functools.partial(_nsa_prompt_body, tq=tq, tk=tk, n_sel=n_sel),
        grid=(batch, N_KV, nt),
        in_specs=[pl.BlockSpec((GQA * DH, tq), lambda b, k, i: (k, b * nt + i)),
                  pl.BlockSpec((1, 1, n_cmp, DH), lambda b, k, i: (b, k, 0, 0)),
                  pl.BlockSpec((1, 1, DH, n_cmp), lambda b, k, i: (b, k, 0, 0)),
                  rows(), cols(), rows(), cols(),
                  pl.BlockSpec((GATE_ROWS, tq), lambda b, k, i: (k, b * nt + i))],
        out_specs=pl.BlockSpec((tq, GQA * DH), lambda b, k, i: (b * nt + i, k)),
        out_shape=jax.ShapeDtypeStruct((m, ATTN_W), F32),
        scratch_shapes=[pltpu.VMEM((2, VROWS, GQA * tq), F32), pltpu.VMEM((n_sel, tq), F32)],
        compiler_params=_params("arbitrary", "arbitrary", "arbitrary"),
        name="nsa_prompt",
    )(qt, kcc, vcct, ksb, vst, kwb, vwt, gates_t)


def _ln_silu_rms(y, lng_ref, lnb_ref, gn_ref):
    mu = jnp.mean(y, axis=-1, keepdims=True)
    var = jnp.mean(jnp.square(y - mu), axis=-1, keepdims=True)
    y = (y - mu) * lax.rsqrt(var + EPS) * lng_ref[...] + lnb_ref[...]
    y = y * jax.nn.sigmoid(y)
    return (_rms(y) * gn_ref[...]).astype(BF16)


def _mix_out_prompt_body(oa_ref, u_ref, halo_ref, x_ref, cw_ref, cb_ref, lng_ref, lnb_ref, ga_ref, gc_ref,
                         woa_ref, woc_ref, o_ref, ext_scr, conv_scr, *, tm, tiles_per_seq, rc):
    i = pl.program_id(0)
    pad = halo_ref.shape[0]
    first = (i % tiles_per_seq) == 0
    ext_scr[0:pad, :] = jnp.where(first, 0.0, halo_ref[...])
    ext_scr[pad:, :] = u_ref[...]
    off = pad - (CONV_W - 1)
    span = rc + pad

    def chunk(r, carry):
        base = pl.multiple_of(r * rc, rc)
        window = ext_scr[pl.ds(base, span), :]
        acc = jnp.zeros((rc, CONV_CH), F32) + cb_ref[...]
        for res in range(8):
            shifted = window if res == 0 else pltpu.roll(window, span - res, axis=0)
            for k in range(CONV_W):
                if (k + off) % 8 == res:
                    a8 = k + off - res
                    acc = acc + cw_ref[k:k + 1, :] * shifted[a8:a8 + rc]
        conv_scr[pl.ds(base, rc), :] = acc
        return carry

    lax.fori_loop(0, tm // rc, chunk, 0)
    conv_n = _ln_silu_rms(conv_scr[...], lng_ref, lnb_ref, gc_ref)
    attn_n = (_rms(oa_ref[...]) * ga_ref[...]).astype(BF16)
    half = D_MODEL // 2
    for c in range(2):
        cs = slice(c * half, (c + 1) * half)
        o_ref[:, cs] = x_ref[:, cs] + _dot(attn_n, woa_ref[:, cs]) + _dot(conv_n, woc_ref[:, cs])


def _mix_out_prompt(o_attn, u, x, mw, batch, tm=256, rc=32):
    m = x.shape[0]
    t = m // batch
    tm = min(tm, t)
    pad = 32
    row = lambda w: pl.BlockSpec((tm, w), lambda i: (i, 0))
    halo = pl.BlockSpec((pad, CONV_CH), lambda i: (jnp.maximum(i * (tm // pad) - 1, 0), 0))
    return pl.pallas_call(
        functools.partial(_mix_out_prompt_body, tm=tm, tiles_per_seq=t // tm, rc=rc),
        grid=(m // tm,),
        in_specs=[row(ATTN_W), row(CONV_CH), halo, row(D_MODEL)] + [_const_spec(w.shape) for w in mw],
        out_specs=row(D_MODEL),
        out_shape=jax.ShapeDtypeStruct((m, D_MODEL), F32),
        scratch_shapes=[pltpu.VMEM((tm + pad, CONV_CH), F32), pltpu.VMEM((tm, CONV_CH), F32)],
        compiler_params=_params("arbitrary"),
        name="mix_out_prompt",
    )(o_attn, u, u, x, *mw)


def _mem_kv_body(mem_ref, g_ref, wk_ref, wv_ref, k_ref, v_ref):
    mb = (_rms(mem_ref[...]) * g_ref[...]).astype(BF16)
    k_ref[...] = _dot(mb, wk_ref[...])
    v_ref[...] = _dot(mb, wv_ref[...])


def _mem_kv(mem, g, wk, wv, tm=256):
    m = mem.shape[0]
    row = lambda w: pl.BlockSpec((tm, w), lambda i: (i, 0))
    return pl.pallas_call(
        _mem_kv_body,
        grid=(m // tm,),
        in_specs=[row(D_MODEL), _const_spec(g.shape), _const_spec(wk.shape), _const_spec(wv.shape)],
        out_specs=[row(MEM_W), row(MEM_W)],
        out_shape=[jax.ShapeDtypeStruct((m, MEM_W), F32)] * 2,
        compiler_params=_params("arbitrary"),
        name="mem_kv",
    )(mem, g, wk, wv)


def _mem_attn_core(q, mem_k, mem_v):
    outs = []
    for h in range(MEM_HEADS):
        s = _dot_nt(q[:, h * MEM_DH:(h + 1) * MEM_DH].astype(BF16), mem_k(h)) * MEM_SCALE
        outs.append(_softmax_av(s, mem_v(h)))
    return jnp.concatenate(outs, axis=1).astype(BF16)


def _mem_attn_prompt_body(x_ref, g_ref, wq_ref, mk_ref, mv_ref, wo_ref, o_ref):
    x = x_ref[...]
    hb = (_rms(x) * g_ref[...]).astype(BF16)
    head = lambda ref: lambda h: ref[0, :, h * MEM_DH:(h + 1) * MEM_DH].astype(BF16)
    a = _mem_attn_core(_dot(hb, wq_ref[...]), head(mk_ref), head(mv_ref))
    o_ref[...] = x + _dot(a, wo_ref[...])


def _mem_attn_prompt(x, g, wq, mk, mv, wo, batch, tm=256):
    m = x.shape[0]
    tiles_per_seq = m // batch // tm
    n_mem = mk.shape[1]
    row = pl.BlockSpec((tm, D_MODEL), lambda i: (i, 0))
    mem = pl.BlockSpec((1, n_mem, MEM_W), lambda i: (i // tiles_per_seq, 0, 0))
    return pl.pallas_call(
        _mem_attn_prompt_body,
        grid=(m // tm,),
        in_specs=[row, _const_spec(g.shape), _const_spec(wq.shape), mem, mem, _const_spec(wo.shape)],
        out_specs=row,
        out_shape=jax.ShapeDtypeStruct((m, D_MODEL), F32),
        compiler_params=_params("arbitrary"),
        name="mem_attn_prompt",
    )(x, g, wq, mk, mv, wo)


def _ffn_tail(f, nf, x_ref, acc_scr, gf_ref, o_ref):
    @pl.when(f == nf - 1)
    def _():
        o_ref[...] = _rms(x_ref[...] + acc_scr[...]) * gf_ref[...]


def _ffn_prompt_body(x_ref, g_ref, wg_ref, wu_ref, cw_ref, cb_ref, wd_ref, gf_ref, o_ref, st_ref,
                     h_scr, acc_scr, carry_scr, gext_scr, *, tm, tiles_per_seq):
    i = pl.program_id(0)
    f = pl.program_id(1)
    nf = pl.num_programs(1)
    hist = carry_scr.shape[1]

    @pl.when(f == 0)
    def _():
        h_scr[...] = (_rms(x_ref[...]) * g_ref[...]).astype(BF16)
        acc_scr[...] = jnp.zeros(acc_scr.shape, F32)

    hb = h_scr[...]
    gate = _dot(hb, wg_ref[...])
    first = (i % tiles_per_seq) == 0
    gext_scr[0:hist, :] = jnp.where(first, 0.0, carry_scr[f])
    gext_scr[hist:, :] = gate
    tail = gate[tm - hist:, :]
    carry_scr[f] = tail
    st_ref[0] = tail
    conv = (cw_ref[0:1, :] * gext_scr[hist - 2:hist - 2 + tm, :] + cw_ref[1:2, :] * gext_scr[hist - 1:hist - 1 + tm, :]
            + cw_ref[2:3, :] * gate + cb_ref[...])
    a = conv * jax.nn.sigmoid(conv) * _dot(hb, wu_ref[...])
    acc_scr[...] += _dot(a.astype(BF16), wd_ref[...])
    _ffn_tail(f, nf, x_ref, acc_scr, gf_ref, o_ref)


def _ffn_prompt(x, g, wg, wu, cw, cb, wd, gf, batch, tm=512, tf=512):
    m = x.shape[0]
    t = m // batch
    tm = min(tm, t)
    nf = D_FF // tf
    hist = 8
    row = pl.BlockSpec((tm, D_MODEL), lambda i, f: (i, 0))
    col = lambda r: pl.BlockSpec((r, tf), lambda i, f: (0, f))
    return pl.pallas_call(
        functools.partial(_ffn_prompt_body, tm=tm, tiles_per_seq=t // tm),
        grid=(m // tm, nf),
        in_specs=[row, _const_spec(g.shape), col(D_MODEL), col(D_MODEL), col(cw.shape[0]), col(1),
                  pl.BlockSpec((tf, D_MODEL), lambda i, f: (f, 0)), _const_spec(gf.shape)],
        out_specs=[row, pl.BlockSpec((1, hist, tf), lambda i, f: (i, 0, f))],
        out_shape=[jax.ShapeDtypeStruct((m, D_MODEL), F32), jax.ShapeDtypeStruct((m // tm, hist, D_FF), F32)],
        scratch_shapes=[pltpu.VMEM((tm, D_MODEL), BF16), pltpu.VMEM((tm, D_MODEL), F32),
                        pltpu.VMEM((nf, hist, tf), F32), pltpu.VMEM((tm + hist, tf), F32)],
        compiler_params=_params("arbitrary", "arbitrary"),
        name="ffn_prompt",
    )(x, g, wg, wu, cw, cb, wd, gf)


PAGE = 128
CHUNKS_PER_PAGE = PAGE // STRIDE_CMP


def _compress_sample_body(pt_ref, pool_ref, pe_ref, w1l_ref, w1t_ref, b1_ref, w2_ref, o_ref, buf, sem, *, n_pages):
    b = pl.program_id(0)
    slot = b % 2

    def fetch(bb, sl):
        def one(j, carry):
            page = pt_ref[bb, j]
            for k in range(N_KV):
                pltpu.make_async_copy(pool_ref.at[page, :, k, :],
                                      buf.at[sl, k, pl.ds(pl.multiple_of(j * PAGE, PAGE), PAGE), :], sem.at[sl]).start()
            return carry
        lax.fori_loop(0, n_pages, one, 0, unroll=8)

    @pl.when(b == 0)
    def _():
        fetch(0, 0)

    @pl.when(b + 1 < pl.num_programs(0))
    def _():
        fetch(b + 1, 1 - slot)

    pltpu.make_async_copy(buf.at[slot], buf.at[slot], sem.at[slot]).wait()
    n = n_pages * CHUNKS_PER_PAGE
    for k in range(N_KV):
        get = lambda l, k=k: buf[slot, k, pl.ds(l, n, stride=STRIDE_CMP), :]
        o_ref[0, k] = _compress_rows(get, n, pe_ref, w1l_ref, w1t_ref, b1_ref, w2_ref)


def _compress_sample(pool, page_table, cw):
    nb, n_pages = page_table.shape
    n = n_pages * CHUNKS_PER_PAGE
    grid_spec = pltpu.PrefetchScalarGridSpec(
        num_scalar_prefetch=1,
        grid=(nb,),
        in_specs=[pl.BlockSpec(memory_space=pl.ANY)] + [_const_spec(w.shape) for w in cw],
        out_specs=pl.BlockSpec((1, N_KV, n, DH), lambda b, pt: (b, 0, 0, 0)),
        scratch_shapes=[pltpu.VMEM((2, N_KV, n_pages * PAGE, DH), F32), pltpu.SemaphoreType.DMA((2,))],
    )
    return pl.pallas_call(
        functools.partial(_compress_sample_body, n_pages=n_pages),
        grid_spec=grid_spec,
        out_shape=jax.ShapeDtypeStruct((nb, N_KV, n, DH), F32),
        compiler_params=_params("arbitrary"),
        name="compress_sample",
    )(page_table, pool, *cw)


HEAD_ROWS = 8
TOKEN_ROWS = 16


def _cmp_select_sample_body(q_ref, kcc_ref, vcc_ref, oc_ref, idx_ref, *, past_len, n_blk):
    bb = q_ref.shape[0]
    n_cmp = kcc_ref.shape[2]
    rows = GQA * HEAD_ROWS
    q_pos = past_len + lax.broadcasted_iota(I32, (rows, 1), 0) % HEAD_ROWS
    cmp_end = lax.broadcasted_iota(I32, (1, n_cmp), 1) * STRIDE_CMP + (BLOCK_CMP - 1)
    cmask = cmp_end <= q_pos
    p_sums = []
    for bi in range(bb):
        for k in range(N_KV):
            s = jnp.where(cmask, _dot_nt(q_ref[bi, k], kcc_ref[bi, k].astype(BF16)), NEG)
            e = jnp.where(cmask, jnp.exp2(s - jnp.max(s, axis=-1, keepdims=True)), 0.0)
            p = e / jnp.maximum(jnp.sum(e, axis=-1, keepdims=True), 1e-30)
            oc_ref[bi, k] = _dot(p.astype(BF16), vcc_ref[bi, k].astype(BF16))
            p_sum = p[0:HEAD_ROWS]
            for g in range(1, GQA):
                p_sum = p_sum + p[g * HEAD_ROWS:(g + 1) * HEAD_ROWS]
            p_sums.append(p_sum)
    n_rows = bb * N_KV * HEAD_ROWS
    imp = _importance(jnp.concatenate(p_sums, axis=0), _overlap_matrix(n_cmp, n_blk))
    q_pos_r = past_len + lax.broadcasted_iota(I32, (n_rows, 1), 0) % HEAD_ROWS
    blk = lax.broadcasted_iota(I32, (1, n_blk), 1)
    cur = q_pos_r // SEL_BLOCK
    forced = (blk == 0) | (blk == cur) | (blk == cur - 1)
    score = jnp.where(blk * SEL_BLOCK <= q_pos_r, jnp.where(forced, BIG, imp), -BIG)
    lane = lax.broadcasted_iota(I32, (n_rows, n_blk), 1).astype(F32)
    out_lane = lax.broadcasted_iota(I32, (n_rows, LANES), 1)
    picks = jnp.zeros((n_rows, LANES), F32)
    for n in range(TOP_N):
        best = jnp.max(score, axis=-1, keepdims=True)
        pick = jnp.min(jnp.where(score == best, lane, float(n_blk)), axis=-1, keepdims=True)
        picks = jnp.where(out_lane == n, pick, picks)
        score = jnp.where(lane == pick, -3e38, score)
    picks = picks.astype(I32)
    for r in range(bb * N_KV):
        idx_ref[r // N_KV, r % N_KV] = picks[r * HEAD_ROWS:(r + 1) * HEAD_ROWS]


def _cmp_select_sample(q_hm, kcc, vcc, past_len, bb=8):
    nb = q_hm.shape[0]
    bb = min(bb, nb)
    assert nb % bb == 0
    n_cmp = kcc.shape[2]
    n_blk = -(-(past_len // SEL_BLOCK + 1) // LANES) * LANES
    rows = GQA * HEAD_ROWS
    spec = lambda r, w: pl.BlockSpec((bb, N_KV, r, w), lambda b: (b, 0, 0, 0))
    return pl.pallas_call(
        functools.partial(_cmp_select_sample_body, past_len=past_len, n_blk=n_blk),
        grid=(nb // bb,),
        in_specs=[spec(rows, DH), spec(n_cmp, DH), spec(n_cmp, DH)],
        out_specs=[spec(rows, DH), spec(HEAD_ROWS, LANES)],
        out_shape=[jax.ShapeDtypeStruct((nb, N_KV, rows, DH), F32),
                   jax.ShapeDtypeStruct((nb, N_KV, HEAD_ROWS, LANES), I32)],
        compiler_params=_params("arbitrary"),
        name="cmp_select_sample",
    )(q_hm, kcc, vcc)


def _attn_sample_body(pt_ref, idx_s_ref, q_ref, idx_v_ref, ksn_ref, vsn_ref, kwn_ref, vwn_ref, kwin_ref, vwin_ref,
                      oc_ref, gates_ref, kpool_ref, vpool_ref, o_ref, kbuf, vbuf, sem, *, dec_seq, n_pool_blk):
    b = pl.program_id(0)
    slot = b % 2
    assert PAGE == 2 * SEL_BLOCK

    def gather(bb, sl):
        for kt in range(N_KV * dec_seq):
            k = kt // dec_seq
            for n in range(TOP_N):
                blk = idx_s_ref[bb, kt * TOP_N + n]
                src = jnp.where(blk < n_pool_blk, blk, 0)
                page = pt_ref[bb, lax.shift_right_logical(src, 1)]
                rows = pl.ds(pl.multiple_of((src & 1) * SEL_BLOCK, SEL_BLOCK), SEL_BLOCK)
                dst = pl.ds(n * SEL_BLOCK, SEL_BLOCK)
                pltpu.make_async_copy(kpool_ref.at[page, rows, k], kbuf.at[sl, kt, dst], sem.at[sl, 0]).start()
                pltpu.make_async_copy(vpool_ref.at[page, rows, k], vbuf.at[sl, kt, dst], sem.at[sl, 1]).start()

    @pl.when(b == 0)
    def _():
        gather(0, 0)

    @pl.when(b + 1 < pl.num_programs(0))
    def _():
        gather(b + 1, 1 - slot)

    pltpu.make_async_copy(kbuf.at[slot], kbuf.at[slot], sem.at[slot, 0]).wait()
    pltpu.make_async_copy(vbuf.at[slot], vbuf.at[slot], sem.at[slot, 1]).wait()

    n_keys = TOP_N * SEL_BLOCK
    rows = dec_seq * TOKEN_ROWS
    key_slot = lax.broadcasted_iota(I32, (LANES, n_keys), 1) // SEL_BLOCK
    expand = (lax.broadcasted_iota(I32, (LANES, n_keys), 0) == key_slot).astype(BF16)
    new_col = lax.broadcasted_iota(I32, (1, TOKEN_ROWS), 1)
    t_row = lax.broadcasted_iota(I32, (rows, 1), 0) // TOKEN_ROWS
    n_win = kwin_ref.shape[1]
    win_old_vis = lax.broadcasted_iota(I32, (1, n_win), 1) > t_row
    win_new_vis = new_col <= t_row
    for k in range(N_KV):
        qk = q_ref[0, k]
        pool_ok = _dot((idx_v_ref[0, k] < n_pool_blk).astype(BF16), expand)
        ksn, vsn = ksn_ref[0, k], vsn_ref[0, k]
        s_old = jnp.where(win_old_vis, _dot_nt(qk, kwin_ref[0, :, k, :].astype(BF16)), NEG)
        s_new = jnp.where(win_new_vis, _dot_nt(qk, kwn_ref[0, k]), NEG)
        m = jnp.maximum(jnp.max(s_old, axis=-1, keepdims=True), jnp.max(s_new, axis=-1, keepdims=True))
        p_old, p_new = jnp.exp2(s_old - m), jnp.exp2(s_new - m)
        l = jnp.sum(p_old, axis=-1, keepdims=True) + jnp.sum(p_new, axis=-1, keepdims=True)
        o_win = (_dot(p_old.astype(BF16), vwin_ref[0, :, k, :].astype(BF16)) + _dot(p_new.astype(BF16), vwn_ref[0, k])) / l
        gates = gates_ref[0, k]
        for t in range(dec_seq):
            ts = slice(t * TOKEN_ROWS, (t + 1) * TOKEN_ROWS)
            kt = k * dec_seq + t
            qt = qk[ts]
            s_pool = jnp.where(pool_ok[t:t + 1, :] > 0.5, _dot_nt(qt, kbuf[slot, kt].astype(BF16)), NEG)
            s_cur = jnp.where(new_col <= t, _dot_nt(qt, ksn), NEG)
            m = jnp.maximum(jnp.max(s_pool, axis=-1, keepdims=True), jnp.max(s_cur, axis=-1, keepdims=True))
            p_pool, p_cur = jnp.exp2(s_pool - m), jnp.exp2(s_cur - m)
            l = jnp.sum(p_pool, axis=-1, keepdims=True) + jnp.sum(p_cur, axis=-1, keepdims=True)
            o_sel = (_dot(p_pool.astype(BF16), vbuf[slot, kt].astype(BF16)) + _dot(p_cur.astype(BF16), vsn)) / l
            gt = gates[ts]
            o_ref[0, k, ts, :] = (gt[:, 0:1] * oc_ref[0, k, ts, :] + gt[:, 1:2] * o_sel + gt[:, 2:3] * o_win[ts])


def _attn_sample(page_table, idx_flat, q_tm, idx_pad, ksn, vsn, kwn, vwn, k_win, v_win, oc_tm, gates_tm,
                 k_pool, v_pool, dec_seq, past_len):
    nb = q_tm.shape[0]
    rows = dec_seq * TOKEN_ROWS
    n_win = k_win.shape[1]
    spec = lambda r, w: pl.BlockSpec((1, N_KV, r, w), lambda b, *_: (b, 0, 0, 0))
    win = pl.BlockSpec((1, n_win, N_KV, DH), lambda b, *_: (b, 0, 0, 0))
    any_spec = pl.BlockSpec(memory_space=pl.ANY)
    grid_spec = pltpu.PrefetchScalarGridSpec(
        num_scalar_prefetch=2,
        grid=(nb,),
        in_specs=[spec(rows, DH), spec(TOKEN_ROWS, LANES), spec(TOKEN_ROWS, DH), spec(TOKEN_ROWS, DH),
                  spec(TOKEN_ROWS, DH), spec(TOKEN_ROWS, DH), win, win, spec(rows, DH), spec(rows, LANES),
                  any_spec, any_spec],
        out_specs=spec(rows, DH),
        scratch_shapes=[pltpu.VMEM((2, N_KV * dec_seq, TOP_N * SEL_BLOCK, DH), F32),
                        pltpu.VMEM((2, N_KV * dec_seq, TOP_N * SEL_BLOCK, DH), F32),
                        pltpu.SemaphoreType.DMA((2, 2))],
    )
    return pl.pallas_call(
        functools.partial(_attn_sample_body, dec_seq=dec_seq, n_pool_blk=past_len // SEL_BLOCK),
        grid_spec=grid_spec,
        out_shape=jax.ShapeDtypeStruct((nb, N_KV, rows, DH), F32),
        compiler_params=_params("arbitrary"),
        name="attn_sample",
    )(page_table, idx_flat, q_tm, idx_pad, ksn, vsn, kwn, vwn, k_win, v_win, oc_tm, gates_tm, k_pool, v_pool)


def _mix_out_sample_body(oa_ref, u_ref, st_ref, x_ref, cw_ref, cb_ref, lng_ref, lnb_ref, ga_ref, gc_ref,
                         woa_ref, woc_ref, gm_ref, wmq_ref, o_ref, qm_ref, conv_scr, *, nb, dec_seq):
    hist = CONV_W - 1

    def ext(j):
        if j < hist:
            return st_ref[j * nb:(j + 1) * nb, :]
        return u_ref[(j - hist) * nb:(j - hist + 1) * nb, :]

    for t in range(dec_seq):
        acc = jnp.zeros((nb, CONV_CH), F32) + cb_ref[...]
        for k in range(CONV_W):
            acc = acc + cw_ref[k:k + 1, :] * ext(t + k)
        conv_scr[t * nb:(t + 1) * nb, :] = acc
    conv_n = _ln_silu_rms(conv_scr[...], lng_ref, lnb_ref, gc_ref)
    attn_n = (_rms(oa_ref[...]) * ga_ref[...]).astype(BF16)
    x1 = x_ref[...] + _dot(attn_n, woa_ref[...]) + _dot(conv_n, woc_ref[...])
    o_ref[...] = x1
    qm_ref[...] = _dot((_rms(x1) * gm_ref[...]).astype(BF16), wmq_ref[...])


def _mix_out_sample(o_attn, u, conv_state, x, mw, g_mem, w_mq, nb, dec_seq):
    m = x.shape[0]
    args = (o_attn, u, conv_state, x) + tuple(mw) + (g_mem, w_mq)
    return pl.pallas_call(
        functools.partial(_mix_out_sample_body, nb=nb, dec_seq=dec_seq),
        grid=(1,),
        in_specs=[_const_spec(a.shape) for a in args],
        out_specs=[_const_spec((m, D_MODEL)), _const_spec((m, MEM_W))],
        out_shape=[jax.ShapeDtypeStruct((m, D_MODEL), F32), jax.ShapeDtypeStruct((m, MEM_W), F32)],
        scratch_shapes=[pltpu.VMEM((m, CONV_CH), F32)],
        compiler_params=_params("arbitrary"),
        name="mix_out_sample",
    )(*args)


def _mem_attn_sample_body(q_ref, mk_ref, mv_ref, o_ref):
    head = lambda ref: lambda h: ref[0, :, h, :].astype(BF16)
    o_ref[0] = _mem_attn_core(q_ref[0], head(mk_ref), head(mv_ref))


def _mem_attn_sample(q_pad, mk, mv):
    nb, rows, _ = q_pad.shape
    n_mem = mk.shape[1]
    q_spec = pl.BlockSpec((1, rows, MEM_W), lambda b: (b, 0, 0))
    mem = pl.BlockSpec((1, n_mem, MEM_HEADS, MEM_DH), lambda b: (b, 0, 0, 0))
    return pl.pallas_call(
        _mem_attn_sample_body,
        grid=(nb,),
        in_specs=[q_spec, mem, mem],
        out_specs=q_spec,
        out_shape=jax.ShapeDtypeStruct((nb, rows, MEM_W), BF16),
        compiler_params=_params("arbitrary"),
        name="mem_attn_sample",
    )(q_pad, mk, mv)


def _ffn_sample_body(x_ref, a_ref, wo_ref, g_ref, wg_ref, wu_ref, cw_ref, cb_ref, wd_ref, gf_ref, st_ref,
                     o_ref, sto_ref, x2_scr, h_scr, acc_scr, gate_scr, conv_scr, *, nb, dec_seq):
    f = pl.program_id(0)
    hist = FFN_CONV_W - 1

    @pl.when(f == 0)
    def _():
        x2 = x_ref[...] + _dot(a_ref[...], wo_ref[...])
        x2_scr[...] = x2
        h_scr[...] = (_rms(x2) * g_ref[...]).astype(BF16)
        acc_scr[...] = jnp.zeros(acc_scr.shape, F32)

    hb = h_scr[...]
    gate_scr[...] = _dot(hb, wg_ref[...])

    def ext(j):
        if j < hist:
            return st_ref[j * nb:(j + 1) * nb, :]
        return gate_scr[(j - hist) * nb:(j - hist + 1) * nb, :]

    for t in range(dec_seq):
        acc = cb_ref[...] + cw_ref[0:1, :] * ext(t)
        for k in range(1, FFN_CONV_W):
            acc = acc + cw_ref[k:k + 1, :] * ext(t + k)
        conv_scr[t * nb:(t + 1) * nb, :] = acc
    for j in range(hist):
        sto_ref[j * nb:(j + 1) * nb, :] = ext(dec_seq + j)
    conv = conv_scr[...]
    a = conv * jax.nn.sigmoid(conv) * _dot(hb, wu_ref[...])
    acc_scr[...] += _dot(a.astype(BF16), wd_ref[...])
    _ffn_tail(f, pl.num_programs(0), x2_scr, acc_scr, gf_ref, o_ref)


def _ffn_sample(x1, a, w_mo, fw, ffn_state, nb, dec_seq, tf=512):
    g, wg, wu, cw, cb, wd, gf = fw
    m = x1.shape[0]
    nf = D_FF // tf
    hist = FFN_CONV_W - 1
    col = lambda r: pl.BlockSpec((r, tf), lambda f: (0, f))
    full = lambda shape: pl.BlockSpec(shape, lambda f: (0,) * len(shape))
    return pl.pallas_call(
        functools.partial(_ffn_sample_body, nb=nb, dec_seq=dec_seq),
        grid=(nf,),
        in_specs=[full(x1.shape), full(a.shape), full(w_mo.shape), full(g.shape), col(D_MODEL), col(D_MODEL),
                  col(cw.shape[0]), col(1), pl.BlockSpec((tf, D_MODEL), lambda f: (f, 0)), full(gf.shape),
                  col(nb * hist)],
        out_specs=[full((m, D_MODEL)), col(nb * hist)],
        out_shape=[jax.ShapeDtypeStruct((m, D_MODEL), F32), jax.ShapeDtypeStruct((nb * hist, D_FF), F32)],
        scratch_shapes=[pltpu.VMEM((m, D_MODEL), F32), pltpu.VMEM((m, D_MODEL), BF16), pltpu.VMEM((m, D_MODEL), F32),
                        pltpu.VMEM((m, tf), F32), pltpu.VMEM((m, tf), F32)],
        compiler_params=_params("arbitrary"),
        name="ffn_sample",
    )(x1, a, w_mo, g, wg, wu, cw, cb, wd, gf, ffn_state)


def _prepare_weights(norm_mix_g, w_in, b_gate, cmp_k, cmp_v, conv_w, conv_b, conv_ln_g, conv_ln_b,
                     grp_norm_attn_g, grp_norm_conv_g, w_out, norm_mem_g, mem_norm_g, w_mq, w_mk, w_mv, w_mo,
                     norm_ffn_g, w_ffn_gate, w_ffn_up, ffn_conv_w, ffn_conv_b, w_ffn_down, norm_final_g):
    vec = lambda v: v.reshape(1, -1)
    kv_end = ATTN_W + 6 * KV_W
    n_gate_cols = N_KV * GQA * N_GATE
    per_kv = GQA * N_GATE
    wg = w_in[:, kv_end:kv_end + n_gate_cols].reshape(D_MODEL, N_KV, per_kv)
    wg = jnp.pad(wg, ((0, 0), (0, 0), (0, LANES - per_kv))).reshape(D_MODEL, N_KV * LANES)
    bg = jnp.pad(b_gate.reshape(N_KV, per_kv), ((0, 0), (0, LANES - per_kv))).reshape(1, N_KV * LANES)
    return dict(
        in_proj=(vec(norm_mix_g), w_in[:, :ATTN_W].astype(BF16), w_in[:, ATTN_W:kv_end].astype(BF16),
                 wg.astype(BF16), bg, w_in[:, kv_end + n_gate_cols:].astype(BF16)),
        cmp_k=_compress_weights(*cmp_k),
        cmp_v=_compress_weights(*cmp_v),
        mix=(jnp.pad(conv_w, ((0, 32 - CONV_W), (0, 0))), vec(conv_b), vec(conv_ln_g), vec(conv_ln_b),
             vec(grp_norm_attn_g), vec(grp_norm_conv_g), w_out[:ATTN_W].astype(BF16), w_out[ATTN_W:].astype(BF16)),
        mem_kv=(vec(mem_norm_g), w_mk.astype(BF16), w_mv.astype(BF16)),
        mem=(vec(norm_mem_g), w_mq.astype(BF16), w_mo.astype(BF16)),
        ffn=(vec(norm_ffn_g), w_ffn_gate.astype(BF16), w_ffn_up.astype(BF16),
             jnp.pad(ffn_conv_w, ((0, 8 - FFN_CONV_W), (0, 0))), vec(ffn_conv_b), w_ffn_down.astype(BF16),
             vec(norm_final_g)),
    )


def _prompt_forward(x_prompt, mem_prompt, w):
    batch, t, _ = x_prompt.shape
    x = x_prompt.reshape(batch * t, D_MODEL)
    (q, kc, vc, ks, vs, kw, vw, ksb, vsb, kwb, vwb, gates, u) = _in_proj(x, *w["in_proj"], tm=256)
    kcc = _compress_prompt(kc, w["cmp_k"], batch)
    vcc = _compress_prompt(vc, w["cmp_v"], batch)
    gates_t = gates.reshape(batch * t, N_KV, LANES)[:, :, :GATE_ROWS].transpose(1, 2, 0)
    o_attn = _nsa_prompt(q.T, kcc, vcc.transpose(0, 1, 3, 2), ksb, _values_with_ones(vsb), kwb, _values_with_ones(vwb),
                         gates_t.reshape(N_KV * GATE_ROWS, batch * t), batch)
    x1 = _mix_out_prompt(o_attn, u, x, w["mix"], batch)
    n_mem = mem_prompt.shape[1]
    mk, mv = _mem_kv(mem_prompt.reshape(batch * n_mem, D_MODEL), *w["mem_kv"])
    g_mem, w_mq, w_mo = w["mem"]
    x2 = _mem_attn_prompt(x1, g_mem, w_mq, mk.reshape(batch, n_mem, MEM_W), mv.reshape(batch, n_mem, MEM_W),
                          w_mo, batch)
    y, ffn_tail = _ffn_prompt(x2, *w["ffn"], batch=batch)
    kv5 = lambda a: a.reshape(1, batch, t, N_KV, DH)
    win = lambda a: a.reshape(batch, t, N_KV, DH)[None, :, t - min(WINDOW, t):]
    tiles = ffn_tail.shape[0] // batch
    new_ffn = ffn_tail.reshape(batch, tiles, ffn_tail.shape[1], D_FF)[:, -1, -(FFN_CONV_W - 1):]
    new_conv = u.reshape(batch, t, CONV_CH)[:, t - (CONV_W - 1):]
    mem5 = lambda a: a.reshape(1, batch, n_mem, MEM_HEADS, MEM_DH)
    return (y.reshape(batch, t, D_MODEL), kv5(kc), kv5(vc), kv5(ks), kv5(vs), win(kw), win(vw),
            new_conv[None], new_ffn[None], mem5(mk), mem5(mv))


def _pad_axis(a, axis, size):
    pads = [(0, 0)] * a.ndim
    pads[axis] = (0, size - a.shape[axis])
    return jnp.pad(a, pads)


def _sample_forward(x_sample, pools, k_win, v_win, conv_state, ffn_state, mem_k, mem_v, page_table, w):
    nb, dec_seq, _ = x_sample.shape
    m = nb * dec_seq
    past_len = page_table.shape[1] * PAGE
    assert dec_seq <= HEAD_ROWS and k_win.shape[1] == WINDOW
    x = x_sample.reshape(m, D_MODEL)
    (q, kc, vc, ks, vs, kw, vw, ksb, vsb, kwb, vwb, gates, u) = _in_proj(x, *w["in_proj"], tm=m)
    pool_kc, pool_vc, pool_ks, pool_vs = pools
    kcc = _compress_sample(pool_kc, page_table, w["cmp_k"])
    vcc = _compress_sample(pool_vc, page_table, w["cmp_v"])

    q5 = q.reshape(nb, dec_seq, N_KV, GQA, DH)
    q_hm = _pad_axis(q5.transpose(0, 2, 3, 1, 4), 3, HEAD_ROWS).reshape(nb, N_KV, GQA * HEAD_ROWS, DH)
    q_tm = _pad_axis(q5.transpose(0, 2, 1, 3, 4), 3, TOKEN_ROWS).reshape(nb, N_KV, dec_seq * TOKEN_ROWS, DH)
    oc_hm, idx = _cmp_select_sample(q_hm, kcc, vcc, past_len)
    oc_tm = oc_hm.reshape(nb, N_KV, GQA, HEAD_ROWS, DH)[:, :, :, :dec_seq].transpose(0, 1, 3, 2, 4)
    oc_tm = _pad_axis(oc_tm, 3, TOKEN_ROWS).reshape(nb, N_KV, dec_seq * TOKEN_ROWS, DH)
    idx_flat = idx[:, :, :dec_seq, :TOP_N].reshape(nb, N_KV * dec_seq * TOP_N)
    idx_pad = _pad_axis(idx, 2, TOKEN_ROWS)
    new_rows = lambda a: _pad_axis(a.reshape(nb, dec_seq, N_KV, DH).transpose(0, 2, 1, 3), 2, TOKEN_ROWS)
    gates_tm = gates.reshape(nb, dec_seq, N_KV, LANES)[..., :GQA * N_GATE].reshape(nb, dec_seq, N_KV, GQA, N_GATE)
    gates_tm = _pad_axis(_pad_axis(gates_tm.transpose(0, 2, 1, 3, 4), 3, TOKEN_ROWS), 4, LANES)
    gates_tm = gates_tm.reshape(nb, N_KV, dec_seq * TOKEN_ROWS, LANES)
    o_tm = _attn_sample(page_table, idx_flat, q_tm, idx_pad, new_rows(ksb), new_rows(vsb), new_rows(kwb),
                        new_rows(vwb), k_win, v_win,
                        oc_tm, gates_tm, pool_ks, pool_vs, dec_seq, past_len)
    o_attn = o_tm.reshape(nb, N_KV, dec_seq, TOKEN_ROWS, DH)[:, :, :, :GQA].transpose(2, 0, 1, 3, 4).reshape(m, ATTN_W)
    step_major = lambda a: a.reshape(nb, -1, a.shape[-1]).transpose(1, 0, 2).reshape(-1, a.shape[-1])
    batch_major = lambda a: a.reshape(-1, nb, a.shape[-1]).transpose(1, 0, 2)

    g_mem, w_mq, w_mo = w["mem"]
    x1, qm = _mix_out_sample(o_attn, step_major(u), step_major(conv_state), step_major(x), w["mix"], g_mem, w_mq,
                             nb, dec_seq)
    n_mem = mem_k.shape[1]
    q_pad = _pad_axis(batch_major(qm), 1, TOKEN_ROWS).astype(BF16)
    a = _mem_attn_sample(q_pad, mem_k, mem_v)
    y, new_ffn = _ffn_sample(x1, step_major(a[:, :dec_seq]), w_mo, w["ffn"], step_major(ffn_state), nb, dec_seq)

    kv5 = lambda a: a.reshape(1, nb, dec_seq, N_KV, DH)
    shift = lambda buf, new: jnp.concatenate([buf[:, dec_seq:], new.reshape((nb, dec_seq) + buf.shape[2:])], axis=1)[None]
    return (batch_major(y), kv5(kc), kv5(vc), kv5(ks), kv5(vs), shift(k_win, kw), shift(v_win, vw),
            shift(conv_state, u), batch_major(new_ffn)[None])


def kernel(x_prompt, x_sample, cache_k_cmp, cache_v_cmp, cache_k_sel, cache_v_sel, cache_k_win, cache_v_win,
           state_conv, state_ffn_conv, cache_mem_k, cache_mem_v, page_table, mem_prompt,
           norm_mix_g, w_in, b_gate, cmp_k_pe, cmp_k_w1, cmp_k_b1, cmp_k_w2, cmp_v_pe, cmp_v_w1, cmp_v_b1, cmp_v_w2,
           conv_w, conv_b, conv_ln_g, conv_ln_b, grp_norm_attn_g, grp_norm_conv_g, w_out,
           norm_mem_g, mem_norm_g, w_mq, w_mk, w_mv, w_mo,
           norm_ffn_g, w_ffn_gate, w_ffn_up, ffn_conv_w, ffn_conv_b, w_ffn_down, norm_final_g):
    assert w_in.shape[0] == 1, "single-layer step"
    w = _prepare_weights(norm_mix_g[0], w_in[0], b_gate[0],
                         (cmp_k_pe[0], cmp_k_w1[0], cmp_k_b1[0], cmp_k_w2[0]),
                         (cmp_v_pe[0], cmp_v_w1[0], cmp_v_b1[0], cmp_v_w2[0]),
                         conv_w[0], conv_b[0], conv_ln_g[0], conv_ln_b[0], grp_norm_attn_g[0], grp_norm_conv_g[0],
                         w_out[0], norm_mem_g[0], mem_norm_g[0], w_mq[0], w_mk[0], w_mv[0], w_mo[0],
                         norm_ffn_g[0], w_ffn_gate[0], w_ffn_up[0], ffn_conv_w[0], ffn_conv_b[0], w_ffn_down[0],
                         norm_final_g)
    p = _prompt_forward(x_prompt, mem_prompt, w)
    s = _sample_forward(x_sample, (cache_k_cmp[0], cache_v_cmp[0], cache_k_sel[0], cache_v_sel[0]),
                        cache_k_win[0], cache_v_win[0], state_conv[0], state_ffn_conv[0],
                        cache_mem_k[0], cache_mem_v[0], page_table, w)
    return (p[0], s[0]) + p[1:] + s[1:]
```

```python
import functools

import jax
import jax.numpy as jnp
from jax import lax
from jax.experimental import pallas as pl
from jax.experimental.pallas import tpu as pltpu

F32 = jnp.float32
BF16 = jnp.bfloat16
I32 = jnp.int32

D_MODEL = 2048
N_KV = 2
GQA = 4
DH = 128
ATTN_W = N_KV * GQA * DH
KV_W = N_KV * DH
N_GATE = 3
BLOCK_CMP = 32
STRIDE_CMP = 16
CMP_HID = 256
SEL_BLOCK = 64
TOP_N = 16
WINDOW = 512
CONV_CH = D_MODEL - ATTN_W
CONV_W = 31
D_FF = 5632
FFN_CONV_W = 3
MEM_HEADS = 4
MEM_DH = 128
MEM_W = MEM_HEADS * MEM_DH
ATTN_SCALE = DH ** -0.5
Q_PRESCALE = ATTN_SCALE * 1.4426950408889634
MEM_SCALE = MEM_DH ** -0.5
EPS = 1e-6
NEG = -1e30
BIG = 1e30
LANES = 128
VMEM_LIMIT = 56 * 1024 * 1024


def _dot(a, b):
    return jnp.dot(a, b, preferred_element_type=F32)


def _dot_nt(a, b):
    return lax.dot_general(a, b, (((1,), (1,)), ((), ())), preferred_element_type=F32)


def _rms(x):
    return x * lax.rsqrt(jnp.mean(x * x, axis=-1, keepdims=True) + EPS)


def _const_spec(shape):
    return pl.BlockSpec(shape, lambda *_: (0,) * len(shape), pipeline_mode=pl.Buffered(1))


def _params(*sem):
    return pltpu.CompilerParams(dimension_semantics=sem, vmem_limit_bytes=VMEM_LIMIT)


def _in_proj_body(x_ref, g_ref, wq_ref, wkv_ref, wg_ref, bg_ref, wglu_ref,
                  q_ref, kc_ref, vc_ref, ks_ref, vs_ref, kw_ref, vw_ref,
                  ksb_ref, vsb_ref, kwb_ref, vwb_ref, gates_ref, u_ref):
    hb = (_rms(x_ref[...]) * g_ref[...]).astype(BF16)
    half = ATTN_W // 2
    for c in range(2):
        q_ref[:, c * half:(c + 1) * half] = (_dot(hb, wq_ref[:, c * half:(c + 1) * half]) * Q_PRESCALE).astype(BF16)
    f32_outs = (kc_ref, vc_ref, ks_ref, vs_ref, kw_ref, vw_ref)
    bf_outs = (None, None, ksb_ref, vsb_ref, kwb_ref, vwb_ref)
    for c in range(6):
        r = _dot(hb, wkv_ref[:, c * KV_W:(c + 1) * KV_W])
        for k in range(N_KV):
            f32_outs[c][:, k, :] = r[:, k * DH:(k + 1) * DH]
        if bf_outs[c] is not None:
            bf_outs[c][...] = r.astype(BF16)
    gates_ref[...] = jax.nn.sigmoid(_dot(hb, wg_ref[...]) + bg_ref[...])
    cw = 256
    for c in range(CONV_CH // cw):
        a = _dot(hb, wglu_ref[:, c * cw:(c + 1) * cw])
        gt = _dot(hb, wglu_ref[:, CONV_CH + c * cw:CONV_CH + (c + 1) * cw])
        u_ref[:, c * cw:(c + 1) * cw] = a * jax.nn.sigmoid(gt)


def _in_proj(x, g, wq, wkv, wg, bg, wglu, tm):
    m = x.shape[0]
    row = lambda w: pl.BlockSpec((tm, w), lambda i: (i, 0))
    out_shape = ([jax.ShapeDtypeStruct((m, ATTN_W), BF16)]
                 + [jax.ShapeDtypeStruct((m, N_KV, DH), F32)] * 6
                 + [jax.ShapeDtypeStruct((m, KV_W), BF16)] * 4
                 + [jax.ShapeDtypeStruct((m, N_KV * LANES), F32),
                    jax.ShapeDtypeStruct((m, CONV_CH), F32)])
    state = pl.BlockSpec((tm, N_KV, DH), lambda i: (i, 0, 0))
    out_specs = ([row(ATTN_W)] + [state] * 6 + [row(KV_W)] * 4 + [row(N_KV * LANES), row(CONV_CH)])
    return pl.pallas_call(
        _in_proj_body,
        grid=(m // tm,),
        in_specs=[row(D_MODEL), _const_spec(g.shape), _const_spec(wq.shape), _const_spec(wkv.shape),
                  _const_spec(wg.shape), _const_spec(bg.shape), _const_spec(wglu.shape)],
        out_specs=out_specs,
        out_shape=out_shape,
        compiler_params=_params("arbitrary"),
        name="in_proj",
    )(x, g, wq, wkv, wg, bg, wglu)


def _compress_rows(get_lanes, n, pe_ref, w1l_ref, w1t_ref, b1_ref, w2_ref):
    xk = jnp.concatenate([get_lanes(l) for l in range(STRIDE_CMP)], axis=1)
    lead = _dot((xk + pe_ref[0:1, :]).astype(BF16), w1l_ref[...])
    trail = _dot((xk + pe_ref[1:2, :]).astype(BF16), w1t_ref[...])
    trail_next = pltpu.roll(trail, n - 1, axis=0)
    hid = jax.nn.gelu(lead + trail_next + b1_ref[...])
    out = _dot(hid.astype(BF16), w2_ref[...])
    rows = lax.broadcasted_iota(I32, (n, 1), 0)
    return jnp.where(rows < n - 1, out, 0.0)


def _compress_prompt_body(x_ref, pe_ref, w1l_ref, w1t_ref, b1_ref, w2_ref, o_ref):
    n = x_ref.shape[0] // STRIDE_CMP
    for k in range(N_KV):
        get = lambda l, k=k: x_ref[pl.ds(l, n, stride=STRIDE_CMP), k, :]
        o_ref[0, k] = _compress_rows(get, n, pe_ref, w1l_ref, w1t_ref, b1_ref, w2_ref)


def _compress_weights(pe, w1, b1, w2):
    half = STRIDE_CMP * DH
    pe2 = pe.reshape(2, half)
    return pe2, w1[:half].astype(BF16), w1[half:].astype(BF16), b1.reshape(1, CMP_HID), w2.astype(BF16)


def _compress_prompt(rows, cw, batch):
    t = rows.shape[0] // batch
    n = t // STRIDE_CMP
    return pl.pallas_call(
        _compress_prompt_body,
        grid=(batch,),
        in_specs=[pl.BlockSpec((t, N_KV, DH), lambda b: (b, 0, 0))] + [_const_spec(w.shape) for w in cw],
        out_specs=pl.BlockSpec((1, N_KV, n, DH), lambda b: (b, 0, 0, 0)),
        out_shape=jax.ShapeDtypeStruct((batch, N_KV, n, DH), F32),
        compiler_params=_params("arbitrary"),
        name="compress_prompt",
    )(rows, *cw)


def _overlap_matrix(n_cmp, n_blk=LANES):
    i = lax.broadcasted_iota(I32, (n_cmp, n_blk), 0) * STRIDE_CMP
    j = lax.broadcasted_iota(I32, (n_cmp, n_blk), 1) * SEL_BLOCK
    ov = jnp.maximum(jnp.minimum(i + BLOCK_CMP, j + SEL_BLOCK) - jnp.maximum(i, j), 0)
    return (ov.astype(F32) * (1.0 / BLOCK_CMP)).astype(BF16)


def _importance(p_sum, ov):
    hi = p_sum.astype(BF16)
    r1 = p_sum - hi.astype(F32)
    mid = r1.astype(BF16)
    lo = (r1 - mid.astype(F32)).astype(BF16)
    return _dot(hi, ov) + _dot(mid, ov) + _dot(lo, ov)


def _softmax_av(s, v):
    p = jnp.exp(s - jnp.max(s, axis=-1, keepdims=True))
    return _dot(p.astype(BF16), v) / jnp.sum(p, axis=-1, keepdims=True)


GATE_ROWS = 16
VROWS = DH + 16


def _nsa_prompt_body(qt_ref, kcc_ref, vcct_ref, ks_ref, vst_ref, kw_ref, vwt_ref, gt_ref, o_ref,
                     acc_scr, bias_scr, *, tq, tk, n_sel):
    i = pl.program_id(2)
    t0 = i * tq
    nq = GQA * tq
    heads = lambda a: jnp.concatenate([a] * GQA, axis=1)
    q_pos = t0 + lax.broadcasted_iota(I32, (1, tq), 1)
    q_pos4 = heads(q_pos)
    qt = jnp.concatenate([qt_ref[g * DH:(g + 1) * DH, :] for g in range(GQA)], axis=1)

    n_cmp = kcc_ref.shape[2]
    cmp_end = lax.broadcasted_iota(I32, (n_cmp, 1), 0) * STRIDE_CMP + (BLOCK_CMP - 1)
    cvis = cmp_end <= q_pos4
    s = jnp.where(cvis, _dot(kcc_ref[0, 0].astype(BF16), qt), NEG)
    e = jnp.where(cvis, jnp.exp2(s - jnp.max(s, axis=0, keepdims=True)), 0.0)
    p = e / jnp.maximum(jnp.sum(e, axis=0, keepdims=True), 1e-30)
    o_cmp = _dot(vcct_ref[0, 0].astype(BF16), p.astype(BF16))
    p_sum = p[:, 0:tq]
    for g in range(1, GQA):
        p_sum = p_sum + p[:, g * tq:(g + 1) * tq]

    ov_i = lax.broadcasted_iota(I32, (n_sel, n_cmp), 1) * STRIDE_CMP
    ov_j = lax.broadcasted_iota(I32, (n_sel, n_cmp), 0) * SEL_BLOCK
    ov = jnp.maximum(jnp.minimum(ov_i + BLOCK_CMP, ov_j + SEL_BLOCK) - jnp.maximum(ov_i, ov_j), 0)
    ov = (ov.astype(F32) * (1.0 / BLOCK_CMP)).astype(BF16)
    hi = p_sum.astype(BF16)
    r1 = p_sum - hi.astype(F32)
    mid = r1.astype(BF16)
    lo = (r1 - mid.astype(F32)).astype(BF16)
    imp = _dot(ov, hi) + _dot(ov, mid) + _dot(ov, lo)
    blk = lax.broadcasted_iota(I32, (n_sel, tq), 0)
    cur = q_pos // SEL_BLOCK
    forced = (blk == 0) | (blk == cur) | (blk == cur - 1)
    s_t = jnp.where(blk * SEL_BLOCK <= q_pos, jnp.where(forced, BIG, imp), -BIG)
    rank = jnp.zeros((n_sel, tq), F32)
    for ib in range(n_sel):
        row = s_t[ib:ib + 1, :]
        beats = (row > s_t) | ((row == s_t) & (blk > ib))
        rank = rank + jnp.where(beats, 1.0, 0.0)
    bias_scr[...] = jnp.where(rank < TOP_N, 0.0, NEG)

    acc_scr[...] = jnp.zeros(acc_scr.shape, F32)
    blocks_per_tile = tk // SEL_BLOCK
    last_tile = ks_ref.shape[0] // tk - 1

    def key_tile(j, state, m, causal):
        jd = jnp.minimum(j, last_tile)
        k0 = pl.multiple_of(jd * tk, tk)
        bias = jnp.concatenate(
            [jnp.broadcast_to(bias_scr[pl.ds(jd * blocks_per_tile + c, 1), :], (SEL_BLOCK, tq))
             for c in range(blocks_per_tile)], axis=0)
        if causal:
            bias = jnp.where(j * tk + lax.broadcasted_iota(I32, (tk, 1), 0) <= q_pos, bias, NEG)
        s = _dot(ks_ref[pl.ds(k0, tk), :], qt) + heads(bias)
        m_new = jnp.maximum(m, jnp.max(s, axis=0, keepdims=True))
        p = jnp.exp2(s - m_new).astype(BF16)
        acc_scr[state] = jnp.exp2(m - m_new) * acc_scr[state] + _dot(vst_ref[:, pl.ds(k0, tk)], p)
        return m_new

    def tile_pair(jp, c):
        return key_tile(2 * jp, 0, c[0], False), key_tile(2 * jp + 1, 1, c[1], False)

    n_pairs = t0 // (2 * tk)
    empty = jnp.full((1, nq), NEG, F32)
    c = lax.fori_loop(0, n_pairs, tile_pair, (empty, empty))
    m0 = key_tile(2 * n_pairs, 0, c[0], True)
    second_is_live = (t0 // tk) % 2 == 1
    m1 = lax.cond(second_is_live, lambda: key_tile(2 * n_pairs + 1, 1, c[1], True), lambda: c[1])
    m = jnp.maximum(m0, m1)
    merged = jnp.exp2(m0 - m) * acc_scr[0] + jnp.exp2(m1 - m) * acc_scr[1]
    o_sel = merged[0:DH] / merged[DH:DH + 1]

    span = WINDOW + tq
    win0 = pl.multiple_of(jnp.maximum(t0 - WINDOW, 0), tq)
    kw_pos = win0 + lax.broadcasted_iota(I32, (span, 1), 0)
    wvis = (kw_pos <= q_pos4) & (q_pos4 - kw_pos < WINDOW)
    s = jnp.where(wvis, _dot(kw_ref[pl.ds(win0, span), :], qt), NEG)
    p = jnp.exp2(s - jnp.max(s, axis=0, keepdims=True)).astype(BF16)
    r = _dot(vwt_ref[:, pl.ds(win0, span)], p)
    o_win = r[0:DH] / r[DH:DH + 1]

    gt = gt_ref[...]
    gate = lambda c: jnp.concatenate([gt[g * N_GATE + c:g * N_GATE + c + 1, :] for g in range(GQA)], axis=1)
    o = gate(0) * o_cmp + gate(1) * o_sel + gate(2) * o_win
    for g in range(GQA):
        o_ref[:, g * DH:(g + 1) * DH] = o[:, g * tq:(g + 1) * tq].T


def _values_with_ones(v):
    m = v.shape[0]
    vt = v.T.reshape(N_KV, DH, m)
    return jnp.concatenate([vt, jnp.ones((N_KV, VROWS - DH, m), v.dtype)], axis=1).reshape(N_KV * VROWS, m)


def _nsa_prompt(qt, kcc, vcct, ksb, vst, kwb, vwt, gates_t, batch, tq=512, tk=512):
    m = qt.shape[1]
    t = m // batch
    tk = min(tk, t)
    nt = t // tq
    n_cmp = kcc.shape[2]
    n_sel = max(t // SEL_BLOCK, 8)
    assert t % tk == 0 and tk % tq == 0 and t >= WINDOW + tq and tq % LANES == 0
    rows = lambda: pl.BlockSpec((t, DH), lambda b, k, i: (b, k))
    cols = lambda: pl.BlockSpec((VROWS, t), lambda b, k, i: (k, b))
    return pl.pallas_call(
        functools.partial(_nsa_prompt_body, tq=tq, tk=tk, n_sel=n_sel),
        grid=(batch, N_KV, nt),
        in_specs=[pl.BlockSpec((GQA * DH, tq), lambda b, k, i: (k, b * nt + i)),
                  pl.BlockSpec((1, 1, n_cmp, DH), lambda b, k, i: (b, k, 0, 0)),
                  pl.BlockSpec((1, 1, DH, n_cmp), lambda b, k, i: (b, k, 0, 0)),
                  rows(), cols(), rows(), cols(),
                  pl.BlockSpec((GATE_ROWS, tq), lambda b, k, i: (k, b * nt + i))],
        out_specs=pl.BlockSpec((tq, GQA * DH), lambda b, k, i: (b * nt + i, k)),
        out_shape=jax.ShapeDtypeStruct((m, ATTN_W), F32),
        scratch_shapes=[pltpu.VMEM((2, VROWS, GQA * tq), F32), pltpu.VMEM((n_sel, tq), F32)],
        compiler_params=_params("arbitrary", "arbitrary", "arbitrary"),
        name="nsa_prompt",
    )(qt, kcc, vcct, ksb, vst, kwb, vwt, gates_t)


def _ln_silu_rms(y, lng_ref, lnb_ref, gn_ref):
    mu = jnp.mean(y, axis=-1, keepdims=True)
    var = jnp.mean(jnp.square(y - mu), axis=-1, keepdims=True)
    y = (y - mu) * lax.rsqrt(var + EPS) * lng_ref[...] + lnb_ref[...]
    y = y * jax.nn.sigmoid(y)
    return (_rms(y) * gn_ref[...]).astype(BF16)


def _mix_out_prompt_body(oa_ref, u_ref, halo_ref, x_ref, cw_ref, cb_ref, lng_ref, lnb_ref, ga_ref, gc_ref,
                         woa_ref, woc_ref, o_ref, ext_scr, conv_scr, *, tm, tiles_per_seq, rc):
    i = pl.program_id(0)
    pad = halo_ref.shape[0]
    first = (i % tiles_per_seq) == 0
    ext_scr[0:pad, :] = jnp.where(first, 0.0, halo_ref[...])
    ext_scr[pad:, :] = u_ref[...]
    off = pad - (CONV_W - 1)
    span = rc + pad

    def chunk(r, carry):
        base = pl.multiple_of(r * rc, rc)
        window = ext_scr[pl.ds(base, span), :]
        acc = jnp.zeros((rc, CONV_CH), F32) + cb_ref[...]
        for res in range(8):
            shifted = window if res == 0 else pltpu.roll(window, span - res, axis=0)
            for k in range(CONV_W):
                if (k + off) % 8 == res:
                    a8 = k + off - res
                    acc = acc + cw_ref[k:k + 1, :] * shifted[a8:a8 + rc]
        conv_scr[pl.ds(base, rc), :] = acc
        return carry

    lax.fori_loop(0, tm // rc, chunk, 0)
    conv_n = _ln_silu_rms(conv_scr[...], lng_ref, lnb_ref, gc_ref)
    attn_n = (_rms(oa_ref[...]) * ga_ref[...]).astype(BF16)
    half = D_MODEL // 2
    for c in range(2):
        cs = slice(c * half, (c + 1) * half)
        o_ref[:, cs] = x_ref[:, cs] + _dot(attn_n, woa_ref[:, cs]) + _dot(conv_n, woc_ref[:, cs])


def _mix_out_prompt(o_attn, u, x, mw, batch, tm=256, rc=32):
    m = x.shape[0]
    t = m // batch
    tm = min(tm, t)
    pad = 32
    row = lambda w: pl.BlockSpec((tm, w), lambda i: (i, 0))
    halo = pl.BlockSpec((pad, CONV_CH), lambda i: (jnp.maximum(i * (tm // pad) - 1, 0), 0))
    return pl.pallas_call(
        functools.partial(_mix_out_prompt_body, tm=tm, tiles_per_seq=t // tm, rc=rc),
        grid=(m // tm,),
        in_specs=[row(ATTN_W), row(CONV_CH), halo, row(D_MODEL)] + [_const_spec(w.shape) for w in mw],
        out_specs=row(D_MODEL),
        out_shape=jax.ShapeDtypeStruct((m, D_MODEL), F32),
        scratch_shapes=[pltpu.VMEM((tm + pad, CONV_CH), F32), pltpu.VMEM((tm, CONV_CH), F32)],
        compiler_params=_params("arbitrary"),
        name="mix_out_prompt",
    )(o_attn, u, u, x, *mw)


def _mem_kv_body(mem_ref, g_ref, wk_ref, wv_ref, k_ref, v_ref):
    mb = (_rms(mem_ref[...]) * g_ref[...]).astype(BF16)
    k_ref[...] = _dot(mb, wk_ref[...])
    v_ref[...] = _dot(mb, wv_ref[...])


def _mem_kv(mem, g, wk, wv, tm=256):
    m = mem.shape[0]
    row = lambda w: pl.BlockSpec((tm, w), lambda i: (i, 0))
    return pl.pallas_call(
        _mem_kv_body,
        grid=(m // tm,),
        in_specs=[row(D_MODEL), _const_spec(g.shape), _const_spec(wk.shape), _const_spec(wv.shape)],
        out_specs=[row(MEM_W), row(MEM_W)],
        out_shape=[jax.ShapeDtypeStruct((m, MEM_W), F32)] * 2,
        compiler_params=_params("arbitrary"),
        name="mem_kv",
    )(mem, g, wk, wv)


def _mem_attn_core(q, mem_k, mem_v):
    outs = []
    for h in range(MEM_HEADS):
        s = _dot_nt(q[:, h * MEM_DH:(h + 1) * MEM_DH].astype(BF16), mem_k(h)) * MEM_SCALE
        outs.append(_softmax_av(s, mem_v(h)))
    return jnp.concatenate(outs, axis=1).astype(BF16)


def _mem_attn_prompt_body(x_ref, g_ref, wq_ref, mk_ref, mv_ref, wo_ref, o_ref):
    x = x_ref[...]
    hb = (_rms(x) * g_ref[...]).astype(BF16)
    head = lambda ref: lambda h: ref[0, :, h * MEM_DH:(h + 1) * MEM_DH].astype(BF16)
    a = _mem_attn_core(_dot(hb, wq_ref[...]), head(mk_ref), head(mv_ref))
    o_ref[...] = x + _dot(a, wo_ref[...])


def _mem_attn_prompt(x, g, wq, mk, mv, wo, batch, tm=512):
    m = x.shape[0]
    tiles_per_seq = m // batch // tm
    n_mem = mk.shape[1]
    row = pl.BlockSpec((tm, D_MODEL), lambda i: (i, 0))
    mem = pl.BlockSpec((1, n_mem, MEM_W), lambda i: (i // tiles_per_seq, 0, 0))
    return pl.pallas_call(
        _mem_attn_prompt_body,
        grid=(m // tm,),
        in_specs=[row, _const_spec(g.shape), _const_spec(wq.shape), mem, mem, _const_spec(wo.shape)],
        out_specs=row,
        out_shape=jax.ShapeDtypeStruct((m, D_MODEL), F32),
        compiler_params=_params("arbitrary"),
        name="mem_attn_prompt",
    )(x, g, wq, mk, mv, wo)


def _ffn_tail(f, nf, x_ref, acc_scr, gf_ref, o_ref):
    @pl.when(f == nf - 1)
    def _():
        o_ref[...] = _rms(x_ref[...] + acc_scr[...]) * gf_ref[...]


def _ffn_prompt_body(x_ref, g_ref, wg0_ref, wu0_ref, cw0_ref, cb0_ref, wd0_ref, wg1_ref, wu1_ref, cw1_ref, cb1_ref,
                     wd1_ref, gf_ref, o_ref, st0_ref, st1_ref, h_scr, acc_scr, carry_scr, gext_scr,
                     *, tm, tiles_per_seq, n_tiles):
    i = pl.program_id(0)
    f = pl.program_id(1)
    hist = carry_scr.shape[1]

    @pl.when(f == 0)
    def _():
        h_scr[...] = (_rms(x_ref[...]) * g_ref[...]).astype(BF16)
        acc_scr[...] = jnp.zeros(acc_scr.shape, F32)

    hb = h_scr[...]
    first = (i % tiles_per_seq) == 0

    def hidden(slot, tile, wg_ref, wu_ref, cw_ref, cb_ref, st_ref):
        gate = _dot(hb, wg_ref[...])
        gext_scr[slot, 0:hist, :] = jnp.where(first, 0.0, carry_scr[tile])
        gext_scr[slot, hist:, :] = gate
        tail = gate[tm - hist:, :]
        carry_scr[tile] = tail
        st_ref[0, 0] = tail
        conv = (cw_ref[0:1, :] * gext_scr[slot, hist - 2:hist - 2 + tm, :]
                + cw_ref[1:2, :] * gext_scr[slot, hist - 1:hist - 1 + tm, :] + cw_ref[2:3, :] * gate + cb_ref[...])
        return conv * jax.nn.sigmoid(conv) * _dot(hb, wu_ref[...])

    def down(slot, tile, refs):
        wg_ref, wu_ref, cw_ref, cb_ref, wd_ref, st_ref = refs
        return _dot(hidden(slot, tile, wg_ref, wu_ref, cw_ref, cb_ref, st_ref).astype(BF16), wd_ref[...])

    tile0 = (wg0_ref, wu0_ref, cw0_ref, cb0_ref, wd0_ref, st0_ref)
    tile1 = (wg1_ref, wu1_ref, cw1_ref, cb1_ref, wd1_ref, st1_ref)
    pair = 2 * f + 1 < n_tiles

    @pl.when(pair)
    def _():
        acc_scr[...] += down(0, 2 * f, tile0) + down(1, 2 * f + 1, tile1)

    @pl.when(jnp.logical_not(pair))
    def _():
        acc_scr[...] += down(0, 2 * f, tile0)
        st1_ref[...] = jnp.zeros(st1_ref.shape, F32)

    _ffn_tail(f, pl.num_programs(1), x_ref, acc_scr, gf_ref, o_ref)


def _ffn_prompt(x, g, wg, wu, cw, cb, wd, gf, batch, tm=512, tf=512):
    m = x.shape[0]
    t = m // batch
    tm = min(tm, t)
    n_tiles = D_FF // tf
    n_steps = -(-n_tiles // 2)
    hist = 8
    row = pl.BlockSpec((tm, D_MODEL), lambda i, f: (i, 0))
    tile_of = (lambda f: 2 * f, lambda f: jnp.minimum(2 * f + 1, n_tiles - 1))
    col = lambda r, s: pl.BlockSpec((r, tf), lambda i, f: (0, tile_of[s](f)))
    weights = lambda s: [col(D_MODEL, s), col(D_MODEL, s), col(cw.shape[0], s), col(1, s),
                         pl.BlockSpec((tf, D_MODEL), lambda i, f: (tile_of[s](f), 0))]
    st_spec = pl.BlockSpec((1, 1, hist, tf), lambda i, f: (i, f, 0, 0))
    st_shape = jax.ShapeDtypeStruct((m // tm, n_steps, hist, tf), F32)
    y, st0, st1 = pl.pallas_call(
        functools.partial(_ffn_prompt_body, tm=tm, tiles_per_seq=t // tm, n_tiles=n_tiles),
        grid=(m // tm, n_steps),
        in_specs=[row, _const_spec(g.shape)] + weights(0) + weights(1) + [_const_spec(gf.shape)],
        out_specs=[row, st_spec, st_spec],
        out_shape=[jax.ShapeDtypeStruct((m, D_MODEL), F32), st_shape, st_shape],
        scratch_shapes=[pltpu.VMEM((tm, D_MODEL), BF16), pltpu.VMEM((tm, D_MODEL), F32),
                        pltpu.VMEM((n_tiles, hist, tf), F32), pltpu.VMEM((2, tm + hist, tf), F32)],
        compiler_params=_params("arbitrary", "arbitrary"),
        name="ffn_prompt",
    )(x, g, wg, wu, cw, cb, wd, wg, wu, cw, cb, wd, gf)
    st = jnp.stack([st0, st1], axis=2).transpose(0, 3, 1, 2, 4).reshape(m // tm, hist, 2 * n_steps * tf)
    return y, st[:, :, :D_FF]


PAGE = 128
CHUNKS_PER_PAGE = PAGE // STRIDE_CMP


def _compress_sample_body(pt_ref, pool_ref, pe_ref, w1l_ref, w1t_ref, b1_ref, w2_ref, o_ref, buf, sem, *, n_pages):
    b = pl.program_id(0)
    slot = b % 2

    def fetch(bb, sl):
        def one(j, carry):
            page = pt_ref[bb, j]
            for k in range(N_KV):
                pltpu.make_async_copy(pool_ref.at[page, :, k, :],
                                      buf.at[sl, k, pl.ds(pl.multiple_of(j * PAGE, PAGE), PAGE), :], sem.at[sl]).start()
            return carry
        lax.fori_loop(0, n_pages, one, 0, unroll=8)

    @pl.when(b == 0)
    def _():
        fetch(0, 0)

    @pl.when(b + 1 < pl.num_programs(0))
    def _():
        fetch(b + 1, 1 - slot)

    pltpu.make_async_copy(buf.at[slot], buf.at[slot], sem.at[slot]).wait()
    n = n_pages * CHUNKS_PER_PAGE
    for k in range(N_KV):
        get = lambda l, k=k: buf[slot, k, pl.ds(l, n, stride=STRIDE_CMP), :]
        o_ref[0, k] = _compress_rows(get, n, pe_ref, w1l_ref, w1t_ref, b1_ref, w2_ref)


def _compress_sample(pool, page_table, cw):
    nb, n_pages = page_table.shape
    n = n_pages * CHUNKS_PER_PAGE
    grid_spec = pltpu.PrefetchScalarGridSpec(
        num_scalar_prefetch=1,
        grid=(nb,),
        in_specs=[pl.BlockSpec(memory_space=pl.ANY)] + [_const_spec(w.shape) for w in cw],
        out_specs=pl.BlockSpec((1, N_KV, n, DH), lambda b, pt: (b, 0, 0, 0)),
        scratch_shapes=[pltpu.VMEM((2, N_KV, n_pages * PAGE, DH), F32), pltpu.SemaphoreType.DMA((2,))],
    )
    return pl.pallas_call(
        functools.partial(_compress_sample_body, n_pages=n_pages),
        grid_spec=grid_spec,
        out_shape=jax.ShapeDtypeStruct((nb, N_KV, n, DH), F32),
        compiler_params=_params("arbitrary"),
        name="compress_sample",
    )(page_table, pool, *cw)


HEAD_ROWS = 8
TOKEN_ROWS = 16


def _cmp_select_sample_body(q_ref, kcc_ref, vcc_ref, oc_ref, idx_ref, *, past_len, n_blk):
    bb = q_ref.shape[0]
    n_cmp = kcc_ref.shape[2]
    rows = GQA * HEAD_ROWS
    q_pos = past_len + lax.broadcasted_iota(I32, (rows, 1), 0) % HEAD_ROWS
    cmp_end = lax.broadcasted_iota(I32, (1, n_cmp), 1) * STRIDE_CMP + (BLOCK_CMP - 1)
    cmask = cmp_end <= q_pos
    p_sums = []
    for bi in range(bb):
        for k in range(N_KV):
            s = jnp.where(cmask, _dot_nt(q_ref[bi, k], kcc_ref[bi, k].astype(BF16)), NEG)
            e = jnp.where(cmask, jnp.exp2(s - jnp.max(s, axis=-1, keepdims=True)), 0.0)
            p = e / jnp.maximum(jnp.sum(e, axis=-1, keepdims=True), 1e-30)
            oc_ref[bi, k] = _dot(p.astype(BF16), vcc_ref[bi, k].astype(BF16))
            p_sum = p[0:HEAD_ROWS]
            for g in range(1, GQA):
                p_sum = p_sum + p[g * HEAD_ROWS:(g + 1) * HEAD_ROWS]
            p_sums.append(p_sum)
    n_rows = bb * N_KV * HEAD_ROWS
    imp = _importance(jnp.concatenate(p_sums, axis=0), _overlap_matrix(n_cmp, n_blk))
    q_pos_r = past_len + lax.broadcasted_iota(I32, (n_rows, 1), 0) % HEAD_ROWS
    blk = lax.broadcasted_iota(I32, (1, n_blk), 1)
    cur = q_pos_r // SEL_BLOCK
    forced = (blk == 0) | (blk == cur) | (blk == cur - 1)
    score = jnp.where(blk * SEL_BLOCK <= q_pos_r, jnp.where(forced, BIG, imp), -BIG)
    lane = lax.broadcasted_iota(I32, (n_rows, n_blk), 1).astype(F32)
    out_lane = lax.broadcasted_iota(I32, (n_rows, LANES), 1)
    picks = jnp.zeros((n_rows, LANES), F32)
    for n in range(TOP_N):
        best = jnp.max(score, axis=-1, keepdims=True)
        pick = jnp.min(jnp.where(score == best, lane, float(n_blk)), axis=-1, keepdims=True)
        picks = jnp.where(out_lane == n, pick, picks)
        score = jnp.where(lane == pick, -3e38, score)
    picks = picks.astype(I32)
    for r in range(bb * N_KV):
        idx_ref[r // N_KV, r % N_KV] = picks[r * HEAD_ROWS:(r + 1) * HEAD_ROWS]


def _cmp_select_sample(q_hm, kcc, vcc, past_len, bb=8):
    nb = q_hm.shape[0]
    bb = min(bb, nb)
    assert nb % bb == 0
    n_cmp = kcc.shape[2]
    n_blk = -(-(past_len // SEL_BLOCK + 1) // LANES) * LANES
    rows = GQA * HEAD_ROWS
    spec = lambda r, w: pl.BlockSpec((bb, N_KV, r, w), lambda b: (b, 0, 0, 0))
    return pl.pallas_call(
        functools.partial(_cmp_select_sample_body, past_len=past_len, n_blk=n_blk),
        grid=(nb // bb,),
        in_specs=[spec(rows, DH), spec(n_cmp, DH), spec(n_cmp, DH)],
        out_specs=[spec(rows, DH), spec(HEAD_ROWS, LANES)],
        out_shape=[jax.ShapeDtypeStruct((nb, N_KV, rows, DH), F32),
                   jax.ShapeDtypeStruct((nb, N_KV, HEAD_ROWS, LANES), I32)],
        compiler_params=_params("arbitrary"),
        name="cmp_select_sample",
    )(q_hm, kcc, vcc)


def _attn_sample_body(pt_ref, idx_s_ref, q_ref, idx_v_ref, ksn_ref, vsn_ref, kwn_ref, vwn_ref, kwin_ref, vwin_ref,
                      oc_ref, gates_ref, kpool_ref, vpool_ref, o_ref, kbuf, vbuf, sem, *, dec_seq, n_pool_blk):
    step = pl.program_id(0)
    slot = step % 2
    bb = q_ref.shape[0]
    n_kt = N_KV * dec_seq
    assert PAGE == 2 * SEL_BLOCK

    def gather(st, sl):
        for row in range(bb * n_kt):
            bg = st * bb + row // n_kt
            kt = row % n_kt
            k = kt // dec_seq
            for n in range(TOP_N):
                blk = idx_s_ref[bg, kt * TOP_N + n]
                src = jnp.where(blk < n_pool_blk, blk, 0)
                page = pt_ref[bg, lax.shift_right_logical(src, 1)]
                rows = pl.ds(pl.multiple_of((src & 1) * SEL_BLOCK, SEL_BLOCK), SEL_BLOCK)
                dst = pl.ds(n * SEL_BLOCK, SEL_BLOCK)
                pltpu.make_async_copy(kpool_ref.at[page, rows, k], kbuf.at[sl, row, dst], sem.at[sl, 0]).start()
                pltpu.make_async_copy(vpool_ref.at[page, rows, k], vbuf.at[sl, row, dst], sem.at[sl, 1]).start()

    @pl.when(step == 0)
    def _():
        gather(0, 0)

    @pl.when(step + 1 < pl.num_programs(0))
    def _():
        gather(step + 1, 1 - slot)

    pltpu.make_async_copy(kbuf.at[slot], kbuf.at[slot], sem.at[slot, 0]).wait()
    pltpu.make_async_copy(vbuf.at[slot], vbuf.at[slot], sem.at[slot, 1]).wait()

    n_keys = TOP_N * SEL_BLOCK
    rows = dec_seq * TOKEN_ROWS
    key_slot = lax.broadcasted_iota(I32, (LANES, n_keys), 1) // SEL_BLOCK
    expand = (lax.broadcasted_iota(I32, (LANES, n_keys), 0) == key_slot).astype(BF16)
    new_col = lax.broadcasted_iota(I32, (1, TOKEN_ROWS), 1)
    t_row = lax.broadcasted_iota(I32, (rows, 1), 0) // TOKEN_ROWS
    n_win = kwin_ref.shape[1]
    win_old_vis = lax.broadcasted_iota(I32, (1, n_win), 1) > t_row
    win_new_vis = new_col <= t_row
    for bi, k in [(bi, k) for bi in range(bb) for k in range(N_KV)]:
        qk = q_ref[bi, k]
        pool_ok = _dot((idx_v_ref[bi, k] < n_pool_blk).astype(BF16), expand)
        ksn, vsn = ksn_ref[bi, k], vsn_ref[bi, k]
        s_old = jnp.where(win_old_vis, _dot_nt(qk, kwin_ref[bi, :, k, :].astype(BF16)), NEG)
        s_new = jnp.where(win_new_vis, _dot_nt(qk, kwn_ref[bi, k]), NEG)
        m = jnp.maximum(jnp.max(s_old, axis=-1, keepdims=True), jnp.max(s_new, axis=-1, keepdims=True))
        p_old, p_new = jnp.exp2(s_old - m), jnp.exp2(s_new - m)
        l = jnp.sum(p_old, axis=-1, keepdims=True) + jnp.sum(p_new, axis=-1, keepdims=True)
        o_win = (_dot(p_old.astype(BF16), vwin_ref[bi, :, k, :].astype(BF16)) + _dot(p_new.astype(BF16), vwn_ref[bi, k])) / l
        gates = gates_ref[bi, k]
        for t in range(dec_seq):
            ts = slice(t * TOKEN_ROWS, (t + 1) * TOKEN_ROWS)
            kt = bi * n_kt + k * dec_seq + t
            qt = qk[ts]
            s_pool = jnp.where(pool_ok[t:t + 1, :] > 0.5, _dot_nt(qt, kbuf[slot, kt].astype(BF16)), NEG)
            s_cur = jnp.where(new_col <= t, _dot_nt(qt, ksn), NEG)
            m = jnp.maximum(jnp.max(s_pool, axis=-1, keepdims=True), jnp.max(s_cur, axis=-1, keepdims=True))
            p_pool, p_cur = jnp.exp2(s_pool - m), jnp.exp2(s_cur - m)
            l = jnp.sum(p_pool, axis=-1, keepdims=True) + jnp.sum(p_cur, axis=-1, keepdims=True)
            o_sel = (_dot(p_pool.astype(BF16), vbuf[slot, kt].astype(BF16)) + _dot(p_cur.astype(BF16), vsn)) / l
            gt = gates[ts]
            o_ref[bi, k, ts, :] = (gt[:, 0:1] * oc_ref[bi, k, ts, :] + gt[:, 1:2] * o_sel + gt[:, 2:3] * o_win[ts])


def _attn_sample(page_table, idx_flat, q_tm, idx_pad, ksn, vsn, kwn, vwn, k_win, v_win, oc_tm, gates_tm,
                 k_pool, v_pool, dec_seq, past_len, bb=1):
    nb = q_tm.shape[0]
    bb = bb if nb % bb == 0 else 1
    rows = dec_seq * TOKEN_ROWS
    n_win = k_win.shape[1]
    spec = lambda r, w: pl.BlockSpec((bb, N_KV, r, w), lambda b, *_: (b, 0, 0, 0))
    win = pl.BlockSpec((bb, n_win, N_KV, DH), lambda b, *_: (b, 0, 0, 0))
    any_spec = pl.BlockSpec(memory_space=pl.ANY)
    grid_spec = pltpu.PrefetchScalarGridSpec(
        num_scalar_prefetch=2,
        grid=(nb // bb,),
        in_specs=[spec(rows, DH), spec(TOKEN_ROWS, LANES), spec(TOKEN_ROWS, DH), spec(TOKEN_ROWS, DH),
                  spec(TOKEN_ROWS, DH), spec(TOKEN_ROWS, DH), win, win, spec(rows, DH), spec(rows, LANES),
                  any_spec, any_spec],
        out_specs=spec(rows, DH),
        scratch_shapes=[pltpu.VMEM((2, bb * N_KV * dec_seq, TOP_N * SEL_BLOCK, DH), F32),
                        pltpu.VMEM((2, bb * N_KV * dec_seq, TOP_N * SEL_BLOCK, DH), F32),
                        pltpu.SemaphoreType.DMA((2, 2))],
    )
    return pl.pallas_call(
        functools.partial(_attn_sample_body, dec_seq=dec_seq, n_pool_blk=past_len // SEL_BLOCK),
        grid_spec=grid_spec,
        out_shape=jax.ShapeDtypeStruct((nb, N_KV, rows, DH), F32),
        compiler_params=_params("arbitrary"),
        name="attn_sample",
    )(page_table, idx_flat, q_tm, idx_pad, ksn, vsn, kwn, vwn, k_win, v_win, oc_tm, gates_tm, k_pool, v_pool)


def _mix_out_sample_body(oa_ref, u_ref, st_ref, x_ref, cw_ref, cb_ref, lng_ref, lnb_ref, ga_ref, gc_ref,
                         woa_ref, woc_ref, gm_ref, wmq_ref, o_ref, qm_ref, conv_scr, *, nb, dec_seq):
    hist = CONV_W - 1

    def ext(j):
        if j < hist:
            return st_ref[j * nb:(j + 1) * nb, :]
        return u_ref[(j - hist) * nb:(j - hist + 1) * nb, :]

    for t in range(dec_seq):
        acc = jnp.zeros((nb, CONV_CH), F32) + cb_ref[...]
        for k in range(CONV_W):
            acc = acc + cw_ref[k:k + 1, :] * ext(t + k)
        conv_scr[t * nb:(t + 1) * nb, :] = acc
    conv_n = _ln_silu_rms(conv_scr[...], lng_ref, lnb_ref, gc_ref)
    attn_n = (_rms(oa_ref[...]) * ga_ref[...]).astype(BF16)
    x1 = x_ref[...] + _dot(attn_n, woa_ref[...]) + _dot(conv_n, woc_ref[...])
    o_ref[...] = x1
    qm_ref[...] = _dot((_rms(x1) * gm_ref[...]).astype(BF16), wmq_ref[...])


def _mix_out_sample(o_attn, u, conv_state, x, mw, g_mem, w_mq, nb, dec_seq):
    m = x.shape[0]
    args = (o_attn, u, conv_state, x) + tuple(mw) + (g_mem, w_mq)
    return pl.pallas_call(
        functools.partial(_mix_out_sample_body, nb=nb, dec_seq=dec_seq),
        grid=(1,),
        in_specs=[_const_spec(a.shape) for a in args],
        out_specs=[_const_spec((m, D_MODEL)), _const_spec((m, MEM_W))],
        out_shape=[jax.ShapeDtypeStruct((m, D_MODEL), F32), jax.ShapeDtypeStruct((m, MEM_W), F32)],
        scratch_shapes=[pltpu.VMEM((m, CONV_CH), F32)],
        compiler_params=_params("arbitrary"),
        name="mix_out_sample",
    )(*args)


def _mem_attn_sample_body(q_ref, mk_ref, mv_ref, o_ref):
    head = lambda ref: lambda h: ref[0, :, h, :].astype(BF16)
    o_ref[0] = _mem_attn_core(q_ref[0], head(mk_ref), head(mv_ref))


def _mem_attn_sample(q_pad, mk, mv):
    nb, rows, _ = q_pad.shape
    n_mem = mk.shape[1]
    q_spec = pl.BlockSpec((1, rows, MEM_W), lambda b: (b, 0, 0))
    mem = pl.BlockSpec((1, n_mem, MEM_HEADS, MEM_DH), lambda b: (b, 0, 0, 0))
    return pl.pallas_call(
        _mem_attn_sample_body,
        grid=(nb,),
        in_specs=[q_spec, mem, mem],
        out_specs=q_spec,
        out_shape=jax.ShapeDtypeStruct((nb, rows, MEM_W), BF16),
        compiler_params=_params("arbitrary"),
        name="mem_attn_sample",
    )(q_pad, mk, mv)


def _ffn_sample_body(x_ref, a_ref, wo_ref, g_ref, wg_ref, wu_ref, cw_ref, cb_ref, wd_ref, gf_ref, st_ref,
                     o_ref, sto_ref, x2_scr, h_scr, acc_scr, gate_scr, conv_scr, *, nb, dec_seq):
    f = pl.program_id(0)
    hist = FFN_CONV_W - 1

    @pl.when(f == 0)
    def _():
        x2 = x_ref[...] + _dot(a_ref[...], wo_ref[...])
        x2_scr[...] = x2
        h_scr[...] = (_rms(x2) * g_ref[...]).astype(BF16)
        acc_scr[...] = jnp.zeros(acc_scr.shape, F32)

    hb = h_scr[...]
    gate_scr[...] = _dot(hb, wg_ref[...])

    def ext(j):
        if j < hist:
            return st_ref[j * nb:(j + 1) * nb, :]
        return gate_scr[(j - hist) * nb:(j - hist + 1) * nb, :]

    for t in range(dec_seq):
        acc = cb_ref[...] + cw_ref[0:1, :] * ext(t)
        for k in range(1, FFN_CONV_W):
            acc = acc + cw_ref[k:k + 1, :] * ext(t + k)
        conv_scr[t * nb:(t + 1) * nb, :] = acc
    for j in range(hist):
        sto_ref[j * nb:(j + 1) * nb, :] = ext(dec_seq + j)
    conv = conv_scr[...]
    a = conv * jax.nn.sigmoid(conv) * _dot(hb, wu_ref[...])
    acc_scr[...] += _dot(a.astype(BF16), wd_ref[...])
    _ffn_tail(f, pl.num_programs(0), x2_scr, acc_scr, gf_ref, o_ref)


def _ffn_sample(x1, a, w_mo, fw, ffn_state, nb, dec_seq, tf=512):
    g, wg, wu, cw, cb, wd, gf = fw
    m = x1.shape[0]
    nf = D_FF // tf
    hist = FFN_CONV_W - 1
    col = lambda r: pl.BlockSpec((r, tf), lambda f: (0, f))
    full = lambda shape: pl.BlockSpec(shape, lambda f: (0,) * len(shape))
    return pl.pallas_call(
        functools.partial(_ffn_sample_body, nb=nb, dec_seq=dec_seq),
        grid=(nf,),
        in_specs=[full(x1.shape), full(a.shape), full(w_mo.shape), full(g.shape), col(D_MODEL), col(D_MODEL),
                  col(cw.shape[0]), col(1), pl.BlockSpec((tf, D_MODEL), lambda f: (f, 0)), full(gf.shape),
                  col(nb * hist)],
        out_specs=[full((m, D_MODEL)), col(nb * hist)],
        out_shape=[jax.ShapeDtypeStruct((m, D_MODEL), F32), jax.ShapeDtypeStruct((nb * hist, D_FF), F32)],
        scratch_shapes=[pltpu.VMEM((m, D_MODEL), F32), pltpu.VMEM((m, D_MODEL), BF16), pltpu.VMEM((m, D_MODEL), F32),
                        pltpu.VMEM((m, tf), F32), pltpu.VMEM((m, tf), F32)],
        compiler_params=_params("arbitrary"),
        name="ffn_sample",
    )(x1, a, w_mo, g, wg, wu, cw, cb, wd, gf, ffn_state)


def _prepare_weights(norm_mix_g, w_in, b_gate, cmp_k, cmp_v, conv_w, conv_b, conv_ln_g, conv_ln_b,
                     grp_norm_attn_g, grp_norm_conv_g, w_out, norm_mem_g, mem_norm_g, w_mq, w_mk, w_mv, w_mo,
                     norm_ffn_g, w_ffn_gate, w_ffn_up, ffn_conv_w, ffn_conv_b, w_ffn_down, norm_final_g):
    vec = lambda v: v.reshape(1, -1)
    kv_end = ATTN_W + 6 * KV_W
    n_gate_cols = N_KV * GQA * N_GATE
    per_kv = GQA * N_GATE
    wg = w_in[:, kv_end:kv_end + n_gate_cols].reshape(D_MODEL, N_KV, per_kv)
    wg = jnp.pad(wg, ((0, 0), (0, 0), (0, LANES - per_kv))).reshape(D_MODEL, N_KV * LANES)
    bg = jnp.pad(b_gate.reshape(N_KV, per_kv), ((0, 0), (0, LANES - per_kv))).reshape(1, N_KV * LANES)
    return dict(
        in_proj=(vec(norm_mix_g), w_in[:, :ATTN_W].astype(BF16), w_in[:, ATTN_W:kv_end].astype(BF16),
                 wg.astype(BF16), bg, w_in[:, kv_end + n_gate_cols:].astype(BF16)),
        cmp_k=_compress_weights(*cmp_k),
        cmp_v=_compress_weights(*cmp_v),
        mix=(jnp.pad(conv_w, ((0, 32 - CONV_W), (0, 0))), vec(conv_b), vec(conv_ln_g), vec(conv_ln_b),
             vec(grp_norm_attn_g), vec(grp_norm_conv_g), w_out[:ATTN_W].astype(BF16), w_out[ATTN_W:].astype(BF16)),
        mem_kv=(vec(mem_norm_g), w_mk.astype(BF16), w_mv.astype(BF16)),
        mem=(vec(norm_mem_g), w_mq.astype(BF16), w_mo.astype(BF16)),
        ffn=(vec(norm_ffn_g), w_ffn_gate.astype(BF16), w_ffn_up.astype(BF16),
             jnp.pad(ffn_conv_w, ((0, 8 - FFN_CONV_W), (0, 0))), vec(ffn_conv_b), w_ffn_down.astype(BF16),
             vec(norm_final_g)),
    )


def _prompt_forward(x_prompt, mem_prompt, w):
    batch, t, _ = x_prompt.shape
    x = x_prompt.reshape(batch * t, D_MODEL)
    (q, kc, vc, ks, vs, kw, vw, ksb, vsb, kwb, vwb, gates, u) = _in_proj(x, *w["in_proj"], tm=512)
    kcc = _compress_prompt(kc, w["cmp_k"], batch)
    vcc = _compress_prompt(vc, w["cmp_v"], batch)
    gates_t = gates.reshape(batch * t, N_KV, LANES)[:, :, :GATE_ROWS].transpose(1, 2, 0)
    o_attn = _nsa_prompt(q.T, kcc, vcc.transpose(0, 1, 3, 2), ksb, _values_with_ones(vsb), kwb, _values_with_ones(vwb),
                         gates_t.reshape(N_KV * GATE_ROWS, batch * t), batch)
    x1 = _mix_out_prompt(o_attn, u, x, w["mix"], batch)
    n_mem = mem_prompt.shape[1]
    mk, mv = _mem_kv(mem_prompt.reshape(batch * n_mem, D_MODEL), *w["mem_kv"])
    g_mem, w_mq, w_mo = w["mem"]
    x2 = _mem_attn_prompt(x1, g_mem, w_mq, mk.reshape(batch, n_mem, MEM_W), mv.reshape(batch, n_mem, MEM_W),
                          w_mo, batch)
    y, ffn_tail = _ffn_prompt(x2, *w["ffn"], batch=batch)
    kv5 = lambda a: a.reshape(1, batch, t, N_KV, DH)
    win = lambda a: a.reshape(batch, t, N_KV, DH)[None, :, t - min(WINDOW, t):]
    tiles = ffn_tail.shape[0] // batch
    new_ffn = ffn_tail.reshape(batch, tiles, ffn_tail.shape[1], D_FF)[:, -1, -(FFN_CONV_W - 1):]
    new_conv = u.reshape(batch, t, CONV_CH)[:, t - (CONV_W - 1):]
    mem5 = lambda a: a.reshape(1, batch, n_mem, MEM_HEADS, MEM_DH)
    return (y.reshape(batch, t, D_MODEL), kv5(kc), kv5(vc), kv5(ks), kv5(vs), win(kw), win(vw),
            new_conv[None], new_ffn[None], mem5(mk), mem5(mv))


def _pad_axis(a, axis, size):
    pads = [(0, 0)] * a.ndim
    pads[axis] = (0, size - a.shape[axis])
    return jnp.pad(a, pads)


def _sample_forward(x_sample, pools, k_win, v_win, conv_state, ffn_state, mem_k, mem_v, page_table, w):
    nb, dec_seq, _ = x_sample.shape
    m = nb * dec_seq
    past_len = page_table.shape[1] * PAGE
    assert dec_seq <= HEAD_ROWS and k_win.shape[1] == WINDOW
    x = x_sample.reshape(m, D_MODEL)
    (q, kc, vc, ks, vs, kw, vw, ksb, vsb, kwb, vwb, gates, u) = _in_proj(x, *w["in_proj"], tm=m)
    pool_kc, pool_vc, pool_ks, pool_vs = pools
    kcc = _compress_sample(pool_kc, page_table, w["cmp_k"])
    vcc = _compress_sample(pool_vc, page_table, w["cmp_v"])

    q5 = q.reshape(nb, dec_seq, N_KV, GQA, DH)
    q_hm = _pad_axis(q5.transpose(0, 2, 3, 1, 4), 3, HEAD_ROWS).reshape(nb, N_KV, GQA * HEAD_ROWS, DH)
    q_tm = _pad_axis(q5.transpose(0, 2, 1, 3, 4), 3, TOKEN_ROWS).reshape(nb, N_KV, dec_seq * TOKEN_ROWS, DH)
    oc_hm, idx = _cmp_select_sample(q_hm, kcc, vcc, past_len)
    oc_tm = oc_hm.reshape(nb, N_KV, GQA, HEAD_ROWS, DH)[:, :, :, :dec_seq].transpose(0, 1, 3, 2, 4)
    oc_tm = _pad_axis(oc_tm, 3, TOKEN_ROWS).reshape(nb, N_KV, dec_seq * TOKEN_ROWS, DH)
    idx_flat = idx[:, :, :dec_seq, :TOP_N].reshape(nb, N_KV * dec_seq * TOP_N)
    idx_pad = _pad_axis(idx, 2, TOKEN_ROWS)
    new_rows = lambda a: _pad_axis(a.reshape(nb, dec_seq, N_KV, DH).transpose(0, 2, 1, 3), 2, TOKEN_ROWS)
    gates_tm = gates.reshape(nb, dec_seq, N_KV, LANES)[..., :GQA * N_GATE].reshape(nb, dec_seq, N_KV, GQA, N_GATE)
    gates_tm = _pad_axis(_pad_axis(gates_tm.transpose(0, 2, 1, 3, 4), 3, TOKEN_ROWS), 4, LANES)
    gates_tm = gates_tm.reshape(nb, N_KV, dec_seq * TOKEN_ROWS, LANES)
    o_tm = _attn_sample(page_table, idx_flat, q_tm, idx_pad, new_rows(ksb), new_rows(vsb), new_rows(kwb),
                        new_rows(vwb), k_win, v_win,
                        oc_tm, gates_tm, pool_ks, pool_vs, dec_seq, past_len)
    o_attn = o_tm.reshape(nb, N_KV, dec_seq, TOKEN_ROWS, DH)[:, :, :, :GQA].transpose(2, 0, 1, 3, 4).reshape(m, ATTN_W)
    step_major = lambda a: a.reshape(nb, -1, a.shape[-1]).transpose(1, 0, 2).reshape(-1, a.shape[-1])
    batch_major = lambda a: a.reshape(-1, nb, a.shape[-1]).transpose(1, 0, 2)

    g_mem, w_mq, w_mo = w["mem"]
    x1, qm = _mix_out_sample(o_attn, step_major(u), step_major(conv_state), step_major(x), w["mix"], g_mem, w_mq,
                             nb, dec_seq)
    n_mem = mem_k.shape[1]
    q_pad = _pad_axis(batch_major(qm), 1, TOKEN_ROWS).astype(BF16)
    a = _mem_attn_sample(q_pad, mem_k, mem_v)
    y, new_ffn = _ffn_sample(x1, step_major(a[:, :dec_seq]), w_mo, w["ffn"], step_major(ffn_state), nb, dec_seq)

    kv5 = lambda a: a.reshape(1, nb, dec_seq, N_KV, DH)
    shift = lambda buf, new: jnp.concatenate([buf[:, dec_seq:], new.reshape((nb, dec_seq) + buf.shape[2:])], axis=1)[None]
    return (batch_major(y), kv5(kc), kv5(vc), kv5(ks), kv5(vs), shift(k_win, kw), shift(v_win, vw),
            shift(conv_state, u), batch_major(new_ffn)[None])


def kernel(x_prompt, x_sample, cache_k_cmp, cache_v_cmp, cache_k_sel, cache_v_sel, cache_k_win, cache_v_win,
           state_conv, state_ffn_conv, cache_mem_k, cache_mem_v, page_table, mem_prompt,
           norm_mix_g, w_in, b_gate, cmp_k_pe, cmp_k_w1, cmp_k_b1, cmp_k_w2, cmp_v_pe, cmp_v_w1, cmp_v_b1, cmp_v_w2,
           conv_w, conv_b, conv_ln_g, conv_ln_b, grp_norm_attn_g, grp_norm_conv_g, w_out,
           norm_mem_g, mem_norm_g, w_mq, w_mk, w_mv, w_mo,
           norm_ffn_g, w_ffn_gate, w_ffn_up, ffn_conv_w, ffn_conv_b, w_ffn_down, norm_final_g):
    assert w_in.shape[0] == 1, "single-layer step"
    w = _prepare_weights(norm_mix_g[0], w_in[0], b_gate[0],
                         (cmp_k_pe[0], cmp_k_w1[0], cmp_k_b1[0], cmp_k_w2[0]),
                         (cmp_v_pe[0], cmp_v_w1[0], cmp_v_b1[0], cmp_v_w2[0]),
                         conv_w[0], conv_b[0], conv_ln_g[0], conv_ln_b[0], grp_norm_attn_g[0], grp_norm_conv_g[0],
                         w_out[0], norm_mem_g[0], mem_norm_g[0], w_mq[0], w_mk[0], w_mv[0], w_mo[0],
                         norm_ffn_g[0], w_ffn_gate[0], w_ffn_up[0], ffn_conv_w[0], ffn_conv_b[0], w_ffn_down[0],
                         norm_final_g)
    p = _prompt_forward(x_prompt, mem_prompt, w)
    s = _sample_forward(x_sample, (cache_k_cmp[0], cache_v_cmp[0], cache_k_sel[0], cache_v_sel[0]),
                        cache_k_win[0], cache_v_win[0], state_conv[0], state_ffn_conv[0],
                        cache_mem_k[0], cache_mem_v[0], page_table, w)
    return (p[0], s[0]) + p[1:] + s[1:]
```

```python
import functools

import jax
import jax.numpy as jnp
from jax import lax
from jax.experimental import pallas as pl
from jax.experimental.pallas import tpu as pltpu

F32 = jnp.float32
BF16 = jnp.bfloat16
I32 = jnp.int32

D_MODEL = 2048
N_KV = 2
GQA = 4
DH = 128
ATTN_W = N_KV * GQA * DH
KV_W = N_KV * DH
N_GATE = 3
BLOCK_CMP = 32
STRIDE_CMP = 16
CMP_HID = 256
SEL_BLOCK = 64
TOP_N = 16
WINDOW = 512
CONV_CH = D_MODEL - ATTN_W
CONV_W = 31
D_FF = 5632
FFN_CONV_W = 3
MEM_HEADS = 4
MEM_DH = 128
MEM_W = MEM_HEADS * MEM_DH
ATTN_SCALE = DH ** -0.5
Q_PRESCALE = ATTN_SCALE * 1.4426950408889634
MEM_SCALE = MEM_DH ** -0.5
EPS = 1e-6
NEG = -1e30
BIG = 1e30
LANES = 128
VMEM_LIMIT = 56 * 1024 * 1024


def _dot(a, b):
    return jnp.dot(a, b, preferred_element_type=F32)


def _dot_nt(a, b):
    return lax.dot_general(a, b, (((1,), (1,)), ((), ())), preferred_element_type=F32)


def _rms(x):
    return x * lax.rsqrt(jnp.mean(x * x, axis=-1, keepdims=True) + EPS)


def _const_spec(shape):
    return pl.BlockSpec(shape, lambda *_: (0,) * len(shape), pipeline_mode=pl.Buffered(1))


def _params(*sem):
    return pltpu.CompilerParams(dimension_semantics=sem, vmem_limit_bytes=VMEM_LIMIT)


def _in_proj_body(x_ref, g_ref, wq_ref, wkv_ref, wg_ref, bg_ref, wglu_ref,
                  q_ref, kc_ref, vc_ref, ks_ref, vs_ref, kw_ref, vw_ref,
                  ksb_ref, vsb_ref, kwb_ref, vwb_ref, gates_ref, u_ref):
    hb = (_rms(x_ref[...]) * g_ref[...]).astype(BF16)
    half = ATTN_W // 2
    for c in range(2):
        q_ref[:, c * half:(c + 1) * half] = (_dot(hb, wq_ref[:, c * half:(c + 1) * half]) * Q_PRESCALE).astype(BF16)
    f32_outs = (kc_ref, vc_ref, ks_ref, vs_ref, kw_ref, vw_ref)
    bf_outs = (None, None, ksb_ref, vsb_ref, kwb_ref, vwb_ref)
    for c in range(6):
        r = _dot(hb, wkv_ref[:, c * KV_W:(c + 1) * KV_W])
        for k in range(N_KV):
            f32_outs[c][:, k, :] = r[:, k * DH:(k + 1) * DH]
        if bf_outs[c] is not None:
            bf_outs[c][...] = r.astype(BF16)
    gates_ref[...] = jax.nn.sigmoid(_dot(hb, wg_ref[...]) + bg_ref[...])
    cw = 256
    for c in range(CONV_CH // cw):
        a = _dot(hb, wglu_ref[:, c * cw:(c + 1) * cw])
        gt = _dot(hb, wglu_ref[:, CONV_CH + c * cw:CONV_CH + (c + 1) * cw])
        u_ref[:, c * cw:(c + 1) * cw] = a * jax.nn.sigmoid(gt)


def _in_proj(x, g, wq, wkv, wg, bg, wglu, tm):
    m = x.shape[0]
    row = lambda w: pl.BlockSpec((tm, w), lambda i: (i, 0))
    out_shape = ([jax.ShapeDtypeStruct((m, ATTN_W), BF16)]
                 + [jax.ShapeDtypeStruct((m, N_KV, DH), F32)] * 6
                 + [jax.ShapeDtypeStruct((m, KV_W), BF16)] * 4
                 + [jax.ShapeDtypeStruct((m, N_KV * LANES), F32),
                    jax.ShapeDtypeStruct((m, CONV_CH), F32)])
    state = pl.BlockSpec((tm, N_KV, DH), lambda i: (i, 0, 0))
    out_specs = ([row(ATTN_W)] + [state] * 6 + [row(KV_W)] * 4 + [row(N_KV * LANES), row(CONV_CH)])
    return pl.pallas_call(
        _in_proj_body,
        grid=(m // tm,),
        in_specs=[row(D_MODEL), _const_spec(g.shape), _const_spec(wq.shape), _const_spec(wkv.shape),
                  _const_spec(wg.shape), _const_spec(bg.shape), _const_spec(wglu.shape)],
        out_specs=out_specs,
        out_shape=out_shape,
        compiler_params=_params("arbitrary"),
        name="in_proj",
    )(x, g, wq, wkv, wg, bg, wglu)


def _compress_rows(get_lanes, n, pe_ref, w1l_ref, w1t_ref, b1_ref, w2_ref):
    xk = jnp.concatenate([get_lanes(l) for l in range(STRIDE_CMP)], axis=1)
    lead = _dot((xk + pe_ref[0:1, :]).astype(BF16), w1l_ref[...])
    trail = _dot((xk + pe_ref[1:2, :]).astype(BF16), w1t_ref[...])
    trail_next = pltpu.roll(trail, n - 1, axis=0)
    hid = jax.nn.gelu(lead + trail_next + b1_ref[...])
    out = _dot(hid.astype(BF16), w2_ref[...])
    rows = lax.broadcasted_iota(I32, (n, 1), 0)
    return jnp.where(rows < n - 1, out, 0.0)


def _compress_prompt_body(x_ref, pe_ref, w1l_ref, w1t_ref, b1_ref, w2_ref, o_ref):
    n = x_ref.shape[0] // STRIDE_CMP
    for k in range(N_KV):
        get = lambda l, k=k: x_ref[pl.ds(l, n, stride=STRIDE_CMP), k, :]
        o_ref[0, k] = _compress_rows(get, n, pe_ref, w1l_ref, w1t_ref, b1_ref, w2_ref)


def _compress_weights(pe, w1, b1, w2):
    half = STRIDE_CMP * DH
    pe2 = pe.reshape(2, half)
    return pe2, w1[:half].astype(BF16), w1[half:].astype(BF16), b1.reshape(1, CMP_HID), w2.astype(BF16)


def _compress_prompt(rows, cw, batch):
    t = rows.shape[0] // batch
    n = t // STRIDE_CMP
    return pl.pallas_call(
        _compress_prompt_body,
        grid=(batch,),
        in_specs=[pl.BlockSpec((t, N_KV, DH), lambda b: (b, 0, 0))] + [_const_spec(w.shape) for w in cw],
        out_specs=pl.BlockSpec((1, N_KV, n, DH), lambda b: (b, 0, 0, 0)),
        out_shape=jax.ShapeDtypeStruct((batch, N_KV, n, DH), F32),
        compiler_params=_params("arbitrary"),
        name="compress_prompt",
    )(rows, *cw)


def _overlap_matrix(n_cmp, n_blk=LANES):
    i = lax.broadcasted_iota(I32, (n_cmp, n_blk), 0) * STRIDE_CMP
    j = lax.broadcasted_iota(I32, (n_cmp, n_blk), 1) * SEL_BLOCK
    ov = jnp.maximum(jnp.minimum(i + BLOCK_CMP, j + SEL_BLOCK) - jnp.maximum(i, j), 0)
    return (ov.astype(F32) * (1.0 / BLOCK_CMP)).astype(BF16)


def _importance(p_sum, ov):
    hi = p_sum.astype(BF16)
    r1 = p_sum - hi.astype(F32)
    mid = r1.astype(BF16)
    lo = (r1 - mid.astype(F32)).astype(BF16)
    return _dot(hi, ov) + _dot(mid, ov) + _dot(lo, ov)


def _softmax_av(s, v):
    p = jnp.exp(s - jnp.max(s, axis=-1, keepdims=True))
    return _dot(p.astype(BF16), v) / jnp.sum(p, axis=-1, keepdims=True)


GATE_ROWS = 16
VROWS = DH + 16


def _nsa_prompt_body(qt_ref, kcc_ref, vcct_ref, ks_ref, vst_ref, kw_ref, vwt_ref, gt_ref, o_ref,
                     acc_scr, bias_scr, *, tq, tk, n_sel):
    i = pl.program_id(2)
    t0 = i * tq
    nq = GQA * tq
    heads = lambda a: jnp.concatenate([a] * GQA, axis=1)
    q_pos = t0 + lax.broadcasted_iota(I32, (1, tq), 1)
    q_pos4 = heads(q_pos)
    qt = jnp.concatenate([qt_ref[g * DH:(g + 1) * DH, :] for g in range(GQA)], axis=1)

    n_cmp = kcc_ref.shape[2]
    cmp_end = lax.broadcasted_iota(I32, (n_cmp, 1), 0) * STRIDE_CMP + (BLOCK_CMP - 1)
    cvis = cmp_end <= q_pos4
    s = jnp.where(cvis, _dot(kcc_ref[0, 0].astype(BF16), qt), NEG)
    e = jnp.where(cvis, jnp.exp2(s - jnp.max(s, axis=0, keepdims=True)), 0.0)
    p = e / jnp.maximum(jnp.sum(e, axis=0, keepdims=True), 1e-30)
    o_cmp = _dot(vcct_ref[0, 0].astype(BF16), p.astype(BF16))
    p_sum = p[:, 0:tq]
    for g in range(1, GQA):
        p_sum = p_sum + p[:, g * tq:(g + 1) * tq]

    ov_i = lax.broadcasted_iota(I32, (n_sel, n_cmp), 1) * STRIDE_CMP
    ov_j = lax.broadcasted_iota(I32, (n_sel, n_cmp), 0) * SEL_BLOCK
    ov = jnp.maximum(jnp.minimum(ov_i + BLOCK_CMP, ov_j + SEL_BLOCK) - jnp.maximum(ov_i, ov_j), 0)
    ov = (ov.astype(F32) * (1.0 / BLOCK_CMP)).astype(BF16)
    hi = p_sum.astype(BF16)
    r1 = p_sum - hi.astype(F32)
    mid = r1.astype(BF16)
    lo = (r1 - mid.astype(F32)).astype(BF16)
    imp = _dot(ov, hi) + _dot(ov, mid) + _dot(ov, lo)
    blk = lax.broadcasted_iota(I32, (n_sel, tq), 0)
    cur = q_pos // SEL_BLOCK
    forced = (blk == 0) | (blk == cur) | (blk == cur - 1)
    s_t = jnp.where(blk * SEL_BLOCK <= q_pos, jnp.where(forced, BIG, imp), -BIG)
    rank = jnp.zeros((n_sel, tq), F32)
    for ib in range(n_sel):
        row = s_t[ib:ib + 1, :]
        beats = (row > s_t) | ((row == s_t) & (blk > ib))
        rank = rank + jnp.where(beats, 1.0, 0.0)
    bias_scr[...] = jnp.where(rank < TOP_N, 0.0, NEG)

    acc_scr[...] = jnp.zeros(acc_scr.shape, F32)
    blocks_per_tile = tk // SEL_BLOCK
    last_tile = ks_ref.shape[0] // tk - 1

    def key_tile(j, state, m, causal):
        jd = jnp.minimum(j, last_tile)
        k0 = pl.multiple_of(jd * tk, tk)
        bias = jnp.concatenate(
            [jnp.broadcast_to(bias_scr[pl.ds(jd * blocks_per_tile + c, 1), :], (SEL_BLOCK, tq))
             for c in range(blocks_per_tile)], axis=0)
        if causal:
            bias = jnp.where(j * tk + lax.broadcasted_iota(I32, (tk, 1), 0) <= q_pos, bias, NEG)
        s = _dot(ks_ref[pl.ds(k0, tk), :], qt) + heads(bias)
        m_new = jnp.maximum(m, jnp.max(s, axis=0, keepdims=True))
        p = jnp.exp2(s - m_new).astype(BF16)
        acc_scr[state] = jnp.exp2(m - m_new) * acc_scr[state] + _dot(vst_ref[:, pl.ds(k0, tk)], p)
        return m_new

    def tile_pair(jp, c):
        return key_tile(2 * jp, 0, c[0], False), key_tile(2 * jp + 1, 1, c[1], False)

    n_pairs = t0 // (2 * tk)
    empty = jnp.full((1, nq), NEG, F32)
    c = lax.fori_loop(0, n_pairs, tile_pair, (empty, empty))
    m0 = key_tile(2 * n_pairs, 0, c[0], True)
    second_is_live = (t0 // tk) % 2 == 1
    m1 = lax.cond(second_is_live, lambda: key_tile(2 * n_pairs + 1, 1, c[1], True), lambda: c[1])
    m = jnp.maximum(m0, m1)
    merged = jnp.exp2(m0 - m) * acc_scr[0] + jnp.exp2(m1 - m) * acc_scr[1]
    o_sel = merged[0:DH] / merged[DH:DH + 1]

    span = WINDOW + tq
    win0 = pl.multiple_of(jnp.maximum(t0 - WINDOW, 0), tq)
    kw_pos = win0 + lax.broadcasted_iota(I32, (span, 1), 0)
    wvis = (kw_pos <= q_pos4) & (q_pos4 - kw_pos < WINDOW)
    s = jnp.where(wvis, _dot(kw_ref[pl.ds(win0, span), :], qt), NEG)
    p = jnp.exp2(s - jnp.max(s, axis=0, keepdims=True)).astype(BF16)
    r = _dot(vwt_ref[:, pl.ds(win0, span)], p)
    o_win = r[0:DH] / r[DH:DH + 1]

    gt = gt_ref[...]
    gate = lambda c: jnp.concatenate([gt[g * N_GATE + c:g * N_GATE + c + 1, :] for g in range(GQA)], axis=1)
    o = gate(0) * o_cmp + gate(1) * o_sel + gate(2) * o_win
    for g in range(GQA):
        o_ref[:, g * DH:(g + 1) * DH] = o[:, g * tq:(g + 1) * tq].T


def _values_with_ones(v):
    m = v.shape[0]
    vt = v.T.reshape(N_KV, DH, m)
    return jnp.concatenate([vt, jnp.ones((N_KV, VROWS - DH, m), v.dtype)], axis=1).reshape(N_KV * VROWS, m)


def _nsa_prompt(qt, kcc, vcct, ksb, vst, kwb, vwt, gates_t, batch, tq=512, tk=512):
    m = qt.shape[1]
    t = m // batch
    tk = min(tk, t)
    nt = t // tq
    n_cmp = kcc.shape[2]
    n_sel = max(t // SEL_BLOCK, 8)
    assert t % tk == 0 and tk % tq == 0 and t >= WINDOW + tq and tq % LANES == 0
    rows = lambda: pl.BlockSpec((t, DH), lambda b, k, i: (b, k))
    cols = lambda: pl.BlockSpec((VROWS, t), lambda b, k, i: (k, b))
    return pl.pallas_call(
        functools.partial(_nsa_prompt_body, tq=tq, tk=tk, n_sel=n_sel),
        grid=(batch, N_KV, nt),
        in_specs=[pl.BlockSpec((GQA * DH, tq), lambda b, k, i: (k, b * nt + i)),
                  pl.BlockSpec((1, 1, n_cmp, DH), lambda b, k, i: (b, k, 0, 0)),
                  pl.BlockSpec((1, 1, DH, n_cmp), lambda b, k, i: (b, k, 0, 0)),
                  rows(), cols(), rows(), cols(),
                  pl.BlockSpec((GATE_ROWS, tq), lambda b, k, i: (k, b * nt + i))],
        out_specs=pl.BlockSpec((tq, GQA * DH), lambda b, k, i: (b * nt + i, k)),
        out_shape=jax.ShapeDtypeStruct((m, ATTN_W), F32),
        scratch_shapes=[pltpu.VMEM((2, VROWS, GQA * tq), F32), pltpu.VMEM((n_sel, tq), F32)],
        compiler_params=_params("arbitrary", "arbitrary", "arbitrary"),
        name="nsa_prompt",
    )(qt, kcc, vcct, ksb, vst, kwb, vwt, gates_t)


def _ln_silu_rms(y, lng_ref, lnb_ref, gn_ref):
    mu = jnp.mean(y, axis=-1, keepdims=True)
    var = jnp.mean(jnp.square(y - mu), axis=-1, keepdims=True)
    y = (y - mu) * lax.rsqrt(var + EPS) * lng_ref[...] + lnb_ref[...]
    y = y * jax.nn.sigmoid(y)
    return (_rms(y) * gn_ref[...]).astype(BF16)


def _mix_out_prompt_body(oa_ref, u_ref, halo_ref, x_ref, cw_ref, cb_ref, lng_ref, lnb_ref, ga_ref, gc_ref,
                         woa_ref, woc_ref, o_ref, ext_scr, conv_scr, *, tm, tiles_per_seq, rc):
    i = pl.program_id(0)
    pad = halo_ref.shape[0]
    first = (i % tiles_per_seq) == 0
    ext_scr[0:pad, :] = jnp.where(first, 0.0, halo_ref[...])
    ext_scr[pad:, :] = u_ref[...]
    off = pad - (CONV_W - 1)
    span = rc + pad

    def chunk(r, carry):
        base = pl.multiple_of(r * rc, rc)
        window = ext_scr[pl.ds(base, span), :]
        acc = jnp.zeros((rc, CONV_CH), F32) + cb_ref[...]
        for res in range(8):
            shifted = window if res == 0 else pltpu.roll(window, span - res, axis=0)
            for k in range(CONV_W):
                if (k + off) % 8 == res:
                    a8 = k + off - res
                    acc = acc + cw_ref[k:k + 1, :] * shifted[a8:a8 + rc]
        conv_scr[pl.ds(base, rc), :] = acc
        return carry

    lax.fori_loop(0, tm // rc, chunk, 0)
    conv_n = _ln_silu_rms(conv_scr[...], lng_ref, lnb_ref, gc_ref)
    attn_n = (_rms(oa_ref[...]) * ga_ref[...]).astype(BF16)
    half = D_MODEL // 2
    for c in range(2):
        cs = slice(c * half, (c + 1) * half)
        o_ref[:, cs] = x_ref[:, cs] + _dot(attn_n, woa_ref[:, cs]) + _dot(conv_n, woc_ref[:, cs])


def _mix_out_prompt(o_attn, u, x, mw, batch, tm=256, rc=32):
    m = x.shape[0]
    t = m // batch
    tm = min(tm, t)
    pad = 32
    row = lambda w: pl.BlockSpec((tm, w), lambda i: (i, 0))
    halo = pl.BlockSpec((pad, CONV_CH), lambda i: (jnp.maximum(i * (tm // pad) - 1, 0), 0))
    return pl.pallas_call(
        functools.partial(_mix_out_prompt_body, tm=tm, tiles_per_seq=t // tm, rc=rc),
        grid=(m // tm,),
        in_specs=[row(ATTN_W), row(CONV_CH), halo, row(D_MODEL)] + [_const_spec(w.shape) for w in mw],
        out_specs=row(D_MODEL),
        out_shape=jax.ShapeDtypeStruct((m, D_MODEL), F32),
        scratch_shapes=[pltpu.VMEM((tm + pad, CONV_CH), F32), pltpu.VMEM((tm, CONV_CH), F32)],
        compiler_params=_params("arbitrary"),
        name="mix_out_prompt",
    )(o_attn, u, u, x, *mw)


def _mem_kv_body(mem_ref, g_ref, wk_ref, wv_ref, k_ref, v_ref):
    mb = (_rms(mem_ref[...]) * g_ref[...]).astype(BF16)
    k_ref[...] = _dot(mb, wk_ref[...])
    v_ref[...] = _dot(mb, wv_ref[...])


def _mem_kv(mem, g, wk, wv, tm=256):
    m = mem.shape[0]
    row = lambda w: pl.BlockSpec((tm, w), lambda i: (i, 0))
    return pl.pallas_call(
        _mem_kv_body,
        grid=(m // tm,),
        in_specs=[row(D_MODEL), _const_spec(g.shape), _const_spec(wk.shape), _const_spec(wv.shape)],
        out_specs=[row(MEM_W), row(MEM_W)],
        out_shape=[jax.ShapeDtypeStruct((m, MEM_W), F32)] * 2,
        compiler_params=_params("arbitrary"),
        name="mem_kv",
    )(mem, g, wk, wv)


def _mem_attn_core(q, mem_k, mem_v):
    outs = []
    for h in range(MEM_HEADS):
        s = _dot_nt(q[:, h * MEM_DH:(h + 1) * MEM_DH].astype(BF16), mem_k(h)) * MEM_SCALE
        outs.append(_softmax_av(s, mem_v(h)))
    return jnp.concatenate(outs, axis=1).astype(BF16)


def _mem_attn_prompt_body(x_ref, g_ref, wq_ref, mk_ref, mv_ref, wo_ref, o_ref):
    x = x_ref[...]
    hb = (_rms(x) * g_ref[...]).astype(BF16)
    head = lambda ref: lambda h: ref[0, :, h * MEM_DH:(h + 1) * MEM_DH].astype(BF16)
    a = _mem_attn_core(_dot(hb, wq_ref[...]), head(mk_ref), head(mv_ref))
    o_ref[...] = x + _dot(a, wo_ref[...])


def _mem_attn_prompt(x, g, wq, mk, mv, wo, batch, tm=512):
    m = x.shape[0]
    tiles_per_seq = m // batch // tm
    n_mem = mk.shape[1]
    row = pl.BlockSpec((tm, D_MODEL), lambda i: (i, 0))
    mem = pl.BlockSpec((1, n_mem, MEM_W), lambda i: (i // tiles_per_seq, 0, 0))
    return pl.pallas_call(
        _mem_attn_prompt_body,
        grid=(m // tm,),
        in_specs=[row, _const_spec(g.shape), _const_spec(wq.shape), mem, mem, _const_spec(wo.shape)],
        out_specs=row,
        out_shape=jax.ShapeDtypeStruct((m, D_MODEL), F32),
        compiler_params=_params("arbitrary"),
        name="mem_attn_prompt",
    )(x, g, wq, mk, mv, wo)


def _ffn_tail(f, nf, x_ref, acc_scr, gf_ref, o_ref):
    @pl.when(f == nf - 1)
    def _():
        o_ref[...] = _rms(x_ref[...] + acc_scr[...]) * gf_ref[...]


def _ffn_prompt_body(x_ref, g_ref, wg0_ref, wu0_ref, cw0_ref, cb0_ref, wd0_ref, wg1_ref, wu1_ref, cw1_ref, cb1_ref,
                     wd1_ref, gf_ref, o_ref, st0_ref, st1_ref, h_scr, acc_scr, carry_scr, gext_scr,
                     *, tm, tiles_per_seq, n_tiles):
    i = pl.program_id(0)
    f = pl.program_id(1)
    hist = carry_scr.shape[1]

    @pl.when(f == 0)
    def _():
        h_scr[...] = (_rms(x_ref[...]) * g_ref[...]).astype(BF16)
        acc_scr[...] = jnp.zeros(acc_scr.shape, F32)

    hb = h_scr[...]
    first = (i % tiles_per_seq) == 0

    def hidden(slot, tile, wg_ref, wu_ref, cw_ref, cb_ref, st_ref):
        gate = _dot(hb, wg_ref[...])
        gext_scr[slot, 0:hist, :] = jnp.where(first, 0.0, carry_scr[tile])
        gext_scr[slot, hist:, :] = gate
        tail = gate[tm - hist:, :]
        carry_scr[tile] = tail
        st_ref[0, 0] = tail
        conv = (cw_ref[0:1, :] * gext_scr[slot, hist - 2:hist - 2 + tm, :]
                + cw_ref[1:2, :] * gext_scr[slot, hist - 1:hist - 1 + tm, :] + cw_ref[2:3, :] * gate + cb_ref[...])
        return conv * jax.nn.sigmoid(conv) * _dot(hb, wu_ref[...])

    def down(slot, tile, refs):
        wg_ref, wu_ref, cw_ref, cb_ref, wd_ref, st_ref = refs
        return _dot(hidden(slot, tile, wg_ref, wu_ref, cw_ref, cb_ref, st_ref).astype(BF16), wd_ref[...])

    tile0 = (wg0_ref, wu0_ref, cw0_ref, cb0_ref, wd0_ref, st0_ref)
    tile1 = (wg1_ref, wu1_ref, cw1_ref, cb1_ref, wd1_ref, st1_ref)
    pair = 2 * f + 1 < n_tiles

    @pl.when(pair)
    def _():
        acc_scr[...] += down(0, 2 * f, tile0) + down(1, 2 * f + 1, tile1)

    @pl.when(jnp.logical_not(pair))
    def _():
        acc_scr[...] += down(0, 2 * f, tile0)
        st1_ref[...] = jnp.zeros(st1_ref.shape, F32)

    _ffn_tail(f, pl.num_programs(1), x_ref, acc_scr, gf_ref, o_ref)


def _ffn_prompt(x, g, wg, wu, cw, cb, wd, gf, batch, tm=512, tf=512):
    m = x.shape[0]
    t = m // batch
    tm = min(tm, t)
    n_tiles = D_FF // tf
    n_steps = -(-n_tiles // 2)
    hist = 8
    row = pl.BlockSpec((tm, D_MODEL), lambda i, f: (i, 0))
    tile_of = (lambda f: 2 * f, lambda f: jnp.minimum(2 * f + 1, n_tiles - 1))
    col = lambda r, s: pl.BlockSpec((r, tf), lambda i, f: (0, tile_of[s](f)))
    weights = lambda s: [col(D_MODEL, s), col(D_MODEL, s), col(cw.shape[0], s), col(1, s),
                         pl.BlockSpec((tf, D_MODEL), lambda i, f: (tile_of[s](f), 0))]
    st_spec = pl.BlockSpec((1, 1, hist, tf), lambda i, f: (i, f, 0, 0))
    st_shape = jax.ShapeDtypeStruct((m // tm, n_steps, hist, tf), F32)
    y, st0, st1 = pl.pallas_call(
        functools.partial(_ffn_prompt_body, tm=tm, tiles_per_seq=t // tm, n_tiles=n_tiles),
        grid=(m // tm, n_steps),
        in_specs=[row, _const_spec(g.shape)] + weights(0) + weights(1) + [_const_spec(gf.shape)],
        out_specs=[row, st_spec, st_spec],
        out_shape=[jax.ShapeDtypeStruct((m, D_MODEL), F32), st_shape, st_shape],
        scratch_shapes=[pltpu.VMEM((tm, D_MODEL), BF16), pltpu.VMEM((tm, D_MODEL), F32),
                        pltpu.VMEM((n_tiles, hist, tf), F32), pltpu.VMEM((2, tm + hist, tf), F32)],
        compiler_params=_params("arbitrary", "arbitrary"),
        name="ffn_prompt",
    )(x, g, wg, wu, cw, cb, wd, wg, wu, cw, cb, wd, gf)
    st = jnp.stack([st0, st1], axis=2).transpose(0, 3, 1, 2, 4).reshape(m // tm, hist, 2 * n_steps * tf)
    return y, st[:, :, :D_FF]


PAGE = 128
CHUNKS_PER_PAGE = PAGE // STRIDE_CMP


def _compress_sample_body(pt_ref, pool_ref, pe_ref, w1l_ref, w1t_ref, b1_ref, w2_ref, o_ref, buf, sem, *, n_pages):
    b = pl.program_id(0)
    slot = b % 2

    def fetch(bb, sl):
        def one(j, carry):
            page = pt_ref[bb, j]
            for k in range(N_KV):
                pltpu.make_async_copy(pool_ref.at[page, :, k, :],
                                      buf.at[sl, k, pl.ds(pl.multiple_of(j * PAGE, PAGE), PAGE), :], sem.at[sl]).start()
            return carry
        lax.fori_loop(0, n_pages, one, 0, unroll=8)

    @pl.when(b == 0)
    def _():
        fetch(0, 0)

    @pl.when(b + 1 < pl.num_programs(0))
    def _():
        fetch(b + 1, 1 - slot)

    pltpu.make_async_copy(buf.at[slot], buf.at[slot], sem.at[slot]).wait()
    n = n_pages * CHUNKS_PER_PAGE
    for k in range(N_KV):
        get = lambda l, k=k: buf[slot, k, pl.ds(l, n, stride=STRIDE_CMP), :]
        o_ref[0, k] = _compress_rows(get, n, pe_ref, w1l_ref, w1t_ref, b1_ref, w2_ref)


def _compress_sample(pool, page_table, cw):
    nb, n_pages = page_table.shape
    n = n_pages * CHUNKS_PER_PAGE
    grid_spec = pltpu.PrefetchScalarGridSpec(
        num_scalar_prefetch=1,
        grid=(nb,),
        in_specs=[pl.BlockSpec(memory_space=pl.ANY)] + [_const_spec(w.shape) for w in cw],
        out_specs=pl.BlockSpec((1, N_KV, n, DH), lambda b, pt: (b, 0, 0, 0)),
        scratch_shapes=[pltpu.VMEM((2, N_KV, n_pages * PAGE, DH), F32), pltpu.SemaphoreType.DMA((2,))],
    )
    return pl.pallas_call(
        functools.partial(_compress_sample_body, n_pages=n_pages),
        grid_spec=grid_spec,
        out_shape=jax.ShapeDtypeStruct((nb, N_KV, n, DH), F32),
        compiler_params=_params("arbitrary"),
        name="compress_sample",
    )(page_table, pool, *cw)


HEAD_ROWS = 8
TOKEN_ROWS = 16


def _cmp_select_sample_body(q_ref, kcc_ref, vcc_ref, oc_ref, idx_ref, *, past_len, n_blk):
    bb = q_ref.shape[0]
    n_cmp = kcc_ref.shape[2]
    rows = GQA * HEAD_ROWS
    q_pos = past_len + lax.broadcasted_iota(I32, (rows, 1), 0) % HEAD_ROWS
    cmp_end = lax.broadcasted_iota(I32, (1, n_cmp), 1) * STRIDE_CMP + (BLOCK_CMP - 1)
    cmask = cmp_end <= q_pos
    jobs = [(bi, k) for bi in range(bb) for k in range(N_KV)]
    scores = [jnp.where(cmask, _dot_nt(q_ref[bi, k], kcc_ref[bi, k].astype(BF16)), NEG) for bi, k in jobs]
    p_sums = []
    for s, (bi, k) in zip(scores, jobs):
        e = jnp.where(cmask, jnp.exp2(s - jnp.max(s, axis=-1, keepdims=True)), 0.0)
        p = e / jnp.maximum(jnp.sum(e, axis=-1, keepdims=True), 1e-30)
        oc_ref[bi, k] = _dot(p.astype(BF16), vcc_ref[bi, k].astype(BF16))
        p_sum = p[0:HEAD_ROWS]
        for g in range(1, GQA):
            p_sum = p_sum + p[g * HEAD_ROWS:(g + 1) * HEAD_ROWS]
        p_sums.append(p_sum)
    n_rows = bb * N_KV * HEAD_ROWS
    imp = _importance(jnp.concatenate(p_sums, axis=0), _overlap_matrix(n_cmp, n_blk))
    q_pos_r = past_len + lax.broadcasted_iota(I32, (n_rows, 1), 0) % HEAD_ROWS
    blk = lax.broadcasted_iota(I32, (1, n_blk), 1)
    cur = q_pos_r // SEL_BLOCK
    forced = (blk == 0) | (blk == cur) | (blk == cur - 1)
    score = jnp.where(blk * SEL_BLOCK <= q_pos_r, jnp.where(forced, BIG, imp), -BIG)
    lane = lax.broadcasted_iota(I32, (n_rows, n_blk), 1).astype(F32)
    out_lane = lax.broadcasted_iota(I32, (n_rows, LANES), 1)
    picks = jnp.zeros((n_rows, LANES), F32)
    for n in range(TOP_N):
        best = jnp.max(score, axis=-1, keepdims=True)
        pick = jnp.min(jnp.where(score == best, lane, float(n_blk)), axis=-1, keepdims=True)
        picks = jnp.where(out_lane == n, pick, picks)
        score = jnp.where(lane == pick, -3e38, score)
    picks = picks.astype(I32)
    for r in range(bb * N_KV):
        idx_ref[r // N_KV, r % N_KV] = picks[r * HEAD_ROWS:(r + 1) * HEAD_ROWS]


def _cmp_select_sample(q_hm, kcc, vcc, past_len, bb=8):
    nb = q_hm.shape[0]
    bb = min(bb, nb)
    assert nb % bb == 0
    n_cmp = kcc.shape[2]
    n_blk = -(-(past_len // SEL_BLOCK + 1) // LANES) * LANES
    rows = GQA * HEAD_ROWS
    spec = lambda r, w: pl.BlockSpec((bb, N_KV, r, w), lambda b: (b, 0, 0, 0))
    return pl.pallas_call(
        functools.partial(_cmp_select_sample_body, past_len=past_len, n_blk=n_blk),
        grid=(nb // bb,),
        in_specs=[spec(rows, DH), spec(n_cmp, DH), spec(n_cmp, DH)],
        out_specs=[spec(rows, DH), spec(HEAD_ROWS, LANES)],
        out_shape=[jax.ShapeDtypeStruct((nb, N_KV, rows, DH), F32),
                   jax.ShapeDtypeStruct((nb, N_KV, HEAD_ROWS, LANES), I32)],
        compiler_params=_params("arbitrary"),
        name="cmp_select_sample",
    )(q_hm, kcc, vcc)


def _attn_sample_body(pt_ref, idx_s_ref, q_ref, idx_v_ref, ksn_ref, vsn_ref, kwn_ref, vwn_ref, kwin_ref, vwin_ref,
                      oc_ref, gates_ref, kpool_ref, vpool_ref, o_ref, kbuf, vbuf, sem, *, dec_seq, n_pool_blk):
    step = pl.program_id(0)
    slot = step % 2
    bb = q_ref.shape[0]
    n_kt = N_KV * dec_seq
    assert PAGE == 2 * SEL_BLOCK

    def gather(st, sl):
        for row in range(bb * n_kt):
            bg = st * bb + row // n_kt
            kt = row % n_kt
            k = kt // dec_seq
            for n in range(TOP_N):
                blk = idx_s_ref[bg, kt * TOP_N + n]
                src = jnp.where(blk < n_pool_blk, blk, 0)
                page = pt_ref[bg, lax.shift_right_logical(src, 1)]
                rows = pl.ds(pl.multiple_of((src & 1) * SEL_BLOCK, SEL_BLOCK), SEL_BLOCK)
                dst = pl.ds(n * SEL_BLOCK, SEL_BLOCK)
                pltpu.make_async_copy(kpool_ref.at[page, rows, k], kbuf.at[sl, row, dst], sem.at[sl, 0]).start()
                pltpu.make_async_copy(vpool_ref.at[page, rows, k], vbuf.at[sl, row, dst], sem.at[sl, 1]).start()

    @pl.when(step == 0)
    def _():
        gather(0, 0)

    @pl.when(step + 1 < pl.num_programs(0))
    def _():
        gather(step + 1, 1 - slot)

    pltpu.make_async_copy(kbuf.at[slot], kbuf.at[slot], sem.at[slot, 0]).wait()
    pltpu.make_async_copy(vbuf.at[slot], vbuf.at[slot], sem.at[slot, 1]).wait()

    n_keys = TOP_N * SEL_BLOCK
    rows = dec_seq * TOKEN_ROWS
    key_slot = lax.broadcasted_iota(I32, (LANES, n_keys), 1) // SEL_BLOCK
    expand = (lax.broadcasted_iota(I32, (LANES, n_keys), 0) == key_slot).astype(BF16)
    new_col = lax.broadcasted_iota(I32, (1, TOKEN_ROWS), 1)
    t_row = lax.broadcasted_iota(I32, (rows, 1), 0) // TOKEN_ROWS
    n_win = kwin_ref.shape[1]
    win_old_vis = lax.broadcasted_iota(I32, (1, n_win), 1) > t_row
    win_new_vis = new_col <= t_row
    heads = [(bi, k) for bi in range(bb) for k in range(N_KV)]
    jobs = []
    for bi, k in heads:
        qk = q_ref[bi, k]
        pool_ok = _dot((idx_v_ref[bi, k] < n_pool_blk).astype(BF16), expand)
        jobs.append((jnp.where(win_old_vis, _dot_nt(qk, kwin_ref[bi, :, k, :].astype(BF16)), NEG),
                     jnp.where(win_new_vis, _dot_nt(qk, kwn_ref[bi, k]), NEG),
                     lambda bi=bi, k=k: vwin_ref[bi, :, k, :].astype(BF16), vwn_ref[bi, k]))
        for t in range(dec_seq):
            qt = qk[t * TOKEN_ROWS:(t + 1) * TOKEN_ROWS]
            kt = bi * n_kt + k * dec_seq + t
            jobs.append((jnp.where(pool_ok[t:t + 1, :] > 0.5, _dot_nt(qt, kbuf[slot, kt].astype(BF16)), NEG),
                         jnp.where(new_col <= t, _dot_nt(qt, ksn_ref[bi, k]), NEG),
                         lambda kt=kt: vbuf[slot, kt].astype(BF16), vsn_ref[bi, k]))
    probs = []
    for s_a, s_b, _, _ in jobs:
        m = jnp.maximum(jnp.max(s_a, axis=-1, keepdims=True), jnp.max(s_b, axis=-1, keepdims=True))
        p_a, p_b = jnp.exp2(s_a - m), jnp.exp2(s_b - m)
        probs.append((p_a.astype(BF16), p_b.astype(BF16),
                      jnp.sum(p_a, axis=-1, keepdims=True) + jnp.sum(p_b, axis=-1, keepdims=True)))
    outs = [(_dot(p_a, v_a()) + _dot(p_b, v_b)) / l for (p_a, p_b, l), (_, _, v_a, v_b) in zip(probs, jobs)]
    for h, (bi, k) in enumerate(heads):
        o_win = outs[h * (dec_seq + 1)]
        gates = gates_ref[bi, k]
        for t in range(dec_seq):
            ts = slice(t * TOKEN_ROWS, (t + 1) * TOKEN_ROWS)
            gt = gates[ts]
            o_ref[bi, k, ts, :] = (gt[:, 0:1] * oc_ref[bi, k, ts, :] + gt[:, 1:2] * outs[h * (dec_seq + 1) + 1 + t]
                                   + gt[:, 2:3] * o_win[ts])


def _attn_sample(page_table, idx_flat, q_tm, idx_pad, ksn, vsn, kwn, vwn, k_win, v_win, oc_tm, gates_tm,
                 k_pool, v_pool, dec_seq, past_len, bb=1):
    nb = q_tm.shape[0]
    bb = bb if nb % bb == 0 else 1
    rows = dec_seq * TOKEN_ROWS
    n_win = k_win.shape[1]
    spec = lambda r, w: pl.BlockSpec((bb, N_KV, r, w), lambda b, *_: (b, 0, 0, 0))
    win = pl.BlockSpec((bb, n_win, N_KV, DH), lambda b, *_: (b, 0, 0, 0))
    any_spec = pl.BlockSpec(memory_space=pl.ANY)
    grid_spec = pltpu.PrefetchScalarGridSpec(
        num_scalar_prefetch=2,
        grid=(nb // bb,),
        in_specs=[spec(rows, DH), spec(TOKEN_ROWS, LANES), spec(TOKEN_ROWS, DH), spec(TOKEN_ROWS, DH),
                  spec(TOKEN_ROWS, DH), spec(TOKEN_ROWS, DH), win, win, spec(rows, DH), spec(rows, LANES),
                  any_spec, any_spec],
        out_specs=spec(rows, DH),
        scratch_shapes=[pltpu.VMEM((2, bb * N_KV * dec_seq, TOP_N * SEL_BLOCK, DH), F32),
                        pltpu.VMEM((2, bb * N_KV * dec_seq, TOP_N * SEL_BLOCK, DH), F32),
                        pltpu.SemaphoreType.DMA((2, 2))],
    )
    return pl.pallas_call(
        functools.partial(_attn_sample_body, dec_seq=dec_seq, n_pool_blk=past_len // SEL_BLOCK),
        grid_spec=grid_spec,
        out_shape=jax.ShapeDtypeStruct((nb, N_KV, rows, DH), F32),
        compiler_params=_params("arbitrary"),
        name="attn_sample",
    )(page_table, idx_flat, q_tm, idx_pad, ksn, vsn, kwn, vwn, k_win, v_win, oc_tm, gates_tm, k_pool, v_pool)


def _mix_out_sample_body(oa_ref, u_ref, st_ref, x_ref, cw_ref, cb_ref, lng_ref, lnb_ref, ga_ref, gc_ref,
                         woa_ref, woc_ref, gm_ref, wmq_ref, o_ref, qm_ref, conv_scr, *, nb, dec_seq):
    hist = CONV_W - 1

    def ext(j):
        if j < hist:
            return st_ref[j * nb:(j + 1) * nb, :]
        return u_ref[(j - hist) * nb:(j - hist + 1) * nb, :]

    for t in range(dec_seq):
        acc = jnp.zeros((nb, CONV_CH), F32) + cb_ref[...]
        for k in range(CONV_W):
            acc = acc + cw_ref[k:k + 1, :] * ext(t + k)
        conv_scr[t * nb:(t + 1) * nb, :] = acc
    conv_n = _ln_silu_rms(conv_scr[...], lng_ref, lnb_ref, gc_ref)
    attn_n = (_rms(oa_ref[...]) * ga_ref[...]).astype(BF16)
    x1 = x_ref[...] + _dot(attn_n, woa_ref[...]) + _dot(conv_n, woc_ref[...])
    o_ref[...] = x1
    qm_ref[...] = _dot((_rms(x1) * gm_ref[...]).astype(BF16), wmq_ref[...])


def _mix_out_sample(o_attn, u, conv_state, x, mw, g_mem, w_mq, nb, dec_seq):
    m = x.shape[0]
    args = (o_attn, u, conv_state, x) + tuple(mw) + (g_mem, w_mq)
    return pl.pallas_call(
        functools.partial(_mix_out_sample_body, nb=nb, dec_seq=dec_seq),
        grid=(1,),
        in_specs=[_const_spec(a.shape) for a in args],
        out_specs=[_const_spec((m, D_MODEL)), _const_spec((m, MEM_W))],
        out_shape=[jax.ShapeDtypeStruct((m, D_MODEL), F32), jax.ShapeDtypeStruct((m, MEM_W), F32)],
        scratch_shapes=[pltpu.VMEM((m, CONV_CH), F32)],
        compiler_params=_params("arbitrary"),
        name="mix_out_sample",
    )(*args)


def _mem_attn_sample_body(q_ref, mk_ref, mv_ref, o_ref):
    jobs = [(bi, h) for bi in range(q_ref.shape[0]) for h in range(MEM_HEADS)]
    scores = [_dot_nt(q_ref[bi, :, h * MEM_DH:(h + 1) * MEM_DH], mk_ref[bi, :, h, :].astype(BF16)) * MEM_SCALE
              for bi, h in jobs]
    probs = [jnp.exp(s - jnp.max(s, axis=-1, keepdims=True)) for s in scores]
    outs = [_dot(p.astype(BF16), mv_ref[bi, :, h, :].astype(BF16)) / jnp.sum(p, axis=-1, keepdims=True)
            for p, (bi, h) in zip(probs, jobs)]
    for bi in range(q_ref.shape[0]):
        o_ref[bi] = jnp.concatenate(outs[bi * MEM_HEADS:(bi + 1) * MEM_HEADS], axis=1).astype(BF16)


def _mem_attn_sample(q_pad, mk, mv, bb=4):
    nb, rows, _ = q_pad.shape
    bb = bb if nb % bb == 0 else 1
    n_mem = mk.shape[1]
    q_spec = pl.BlockSpec((bb, rows, MEM_W), lambda b: (b, 0, 0))
    mem = pl.BlockSpec((bb, n_mem, MEM_HEADS, MEM_DH), lambda b: (b, 0, 0, 0))
    return pl.pallas_call(
        _mem_attn_sample_body,
        grid=(nb // bb,),
        in_specs=[q_spec, mem, mem],
        out_specs=q_spec,
        out_shape=jax.ShapeDtypeStruct((nb, rows, MEM_W), BF16),
        compiler_params=_params("arbitrary"),
        name="mem_attn_sample",
    )(q_pad, mk, mv)


def _ffn_sample_body(x_ref, a_ref, wo_ref, g_ref, wg_ref, wu_ref, cw_ref, cb_ref, wd_ref, gf_ref, st_ref,
                     o_ref, sto_ref, x2_scr, h_scr, acc_scr, gate_scr, conv_scr, *, nb, dec_seq):
    f = pl.program_id(0)
    hist = FFN_CONV_W - 1

    @pl.when(f == 0)
    def _():
        x2 = x_ref[...] + _dot(a_ref[...], wo_ref[...])
        x2_scr[...] = x2
        h_scr[...] = (_rms(x2) * g_ref[...]).astype(BF16)
        acc_scr[...] = jnp.zeros(acc_scr.shape, F32)

    hb = h_scr[...]
    gate_scr[...] = _dot(hb, wg_ref[...])

    def ext(j):
        if j < hist:
            return st_ref[j * nb:(j + 1) * nb, :]
        return gate_scr[(j - hist) * nb:(j - hist + 1) * nb, :]

    for t in range(dec_seq):
        acc = cb_ref[...] + cw_ref[0:1, :] * ext(t)
        for k in range(1, FFN_CONV_W):
            acc = acc + cw_ref[k:k + 1, :] * ext(t + k)
        conv_scr[t * nb:(t + 1) * nb, :] = acc
    for j in range(hist):
        sto_ref[j * nb:(j + 1) * nb, :] = ext(dec_seq + j)
    conv = conv_scr[...]
    a = conv * jax.nn.sigmoid(conv) * _dot(hb, wu_ref[...])
    acc_scr[...] += _dot(a.astype(BF16), wd_ref[...])
    _ffn_tail(f, pl.num_programs(0), x2_scr, acc_scr, gf_ref, o_ref)


def _ffn_sample(x1, a, w_mo, fw, ffn_state, nb, dec_seq, tf=512):
    g, wg, wu, cw, cb, wd, gf = fw
    m = x1.shape[0]
    nf = D_FF // tf
    hist = FFN_CONV_W - 1
    col = lambda r: pl.BlockSpec((r, tf), lambda f: (0, f))
    full = lambda shape: pl.BlockSpec(shape, lambda f: (0,) * len(shape))
    return pl.pallas_call(
        functools.partial(_ffn_sample_body, nb=nb, dec_seq=dec_seq),
        grid=(nf,),
        in_specs=[full(x1.shape), full(a.shape), full(w_mo.shape), full(g.shape), col(D_MODEL), col(D_MODEL),
                  col(cw.shape[0]), col(1), pl.BlockSpec((tf, D_MODEL), lambda f: (f, 0)), full(gf.shape),
                  col(nb * hist)],
        out_specs=[full((m, D_MODEL)), col(nb * hist)],
        out_shape=[jax.ShapeDtypeStruct((m, D_MODEL), F32), jax.ShapeDtypeStruct((nb * hist, D_FF), F32)],
        scratch_shapes=[pltpu.VMEM((m, D_MODEL), F32), pltpu.VMEM((m, D_MODEL), BF16), pltpu.VMEM((m, D_MODEL), F32),
                        pltpu.VMEM((m, tf), F32), pltpu.VMEM((m, tf), F32)],
        compiler_params=_params("arbitrary"),
        name="ffn_sample",
    )(x1, a, w_mo, g, wg, wu, cw, cb, wd, gf, ffn_state)


def _prepare_weights(norm_mix_g, w_in, b_gate, cmp_k, cmp_v, conv_w, conv_b, conv_ln_g, conv_ln_b,
                     grp_norm_attn_g, grp_norm_conv_g, w_out, norm_mem_g, mem_norm_g, w_mq, w_mk, w_mv, w_mo,
                     norm_ffn_g, w_ffn_gate, w_ffn_up, ffn_conv_w, ffn_conv_b, w_ffn_down, norm_final_g):
    vec = lambda v: v.reshape(1, -1)
    kv_end = ATTN_W + 6 * KV_W
    n_gate_cols = N_KV * GQA * N_GATE
    per_kv = GQA * N_GATE
    wg = w_in[:, kv_end:kv_end + n_gate_cols].reshape(D_MODEL, N_KV, per_kv)
    wg = jnp.pad(wg, ((0, 0), (0, 0), (0, LANES - per_kv))).reshape(D_MODEL, N_KV * LANES)
    bg = jnp.pad(b_gate.reshape(N_KV, per_kv), ((0, 0), (0, LANES - per_kv))).reshape(1, N_KV * LANES)
    return dict(
        in_proj=(vec(norm_mix_g), w_in[:, :ATTN_W].astype(BF16), w_in[:, ATTN_W:kv_end].astype(BF16),
                 wg.astype(BF16), bg, w_in[:, kv_end + n_gate_cols:].astype(BF16)),
        cmp_k=_compress_weights(*cmp_k),
        cmp_v=_compress_weights(*cmp_v),
        mix=(jnp.pad(conv_w, ((0, 32 - CONV_W), (0, 0))), vec(conv_b), vec(conv_ln_g), vec(conv_ln_b),
             vec(grp_norm_attn_g), vec(grp_norm_conv_g), w_out[:ATTN_W].astype(BF16), w_out[ATTN_W:].astype(BF16)),
        mem_kv=(vec(mem_norm_g), w_mk.astype(BF16), w_mv.astype(BF16)),
        mem=(vec(norm_mem_g), w_mq.astype(BF16), w_mo.astype(BF16)),
        ffn=(vec(norm_ffn_g), w_ffn_gate.astype(BF16), w_ffn_up.astype(BF16),
             jnp.pad(ffn_conv_w, ((0, 8 - FFN_CONV_W), (0, 0))), vec(ffn_conv_b), w_ffn_down.astype(BF16),
             vec(norm_final_g)),
    )


def _prompt_forward(x_prompt, mem_prompt, w):
    batch, t, _ = x_prompt.shape
    x = x_prompt.reshape(batch * t, D_MODEL)
    (q, kc, vc, ks, vs, kw, vw, ksb, vsb, kwb, vwb, gates, u) = _in_proj(x, *w["in_proj"], tm=512)
    kcc = _compress_prompt(kc, w["cmp_k"], batch)
    vcc = _compress_prompt(vc, w["cmp_v"], batch)
    gates_t = gates.reshape(batch * t, N_KV, LANES)[:, :, :GATE_ROWS].transpose(1, 2, 0)
    o_attn = _nsa_prompt(q.T, kcc, vcc.transpose(0, 1, 3, 2), ksb, _values_with_ones(vsb), kwb, _values_with_ones(vwb),
                         gates_t.reshape(N_KV * GATE_ROWS, batch * t), batch)
    x1 = _mix_out_prompt(o_attn, u, x, w["mix"], batch)
    n_mem = mem_prompt.shape[1]
    mk, mv = _mem_kv(mem_prompt.reshape(batch * n_mem, D_MODEL), *w["mem_kv"])
    g_mem, w_mq, w_mo = w["mem"]
    x2 = _mem_attn_prompt(x1, g_mem, w_mq, mk.reshape(batch, n_mem, MEM_W), mv.reshape(batch, n_mem, MEM_W),
                          w_mo, batch)
    y, ffn_tail = _ffn_prompt(x2, *w["ffn"], batch=batch)
    kv5 = lambda a: a.reshape(1, batch, t, N_KV, DH)
    win = lambda a: a.reshape(batch, t, N_KV, DH)[None, :, t - min(WINDOW, t):]
    tiles = ffn_tail.shape[0] // batch
    new_ffn = ffn_tail.reshape(batch, tiles, ffn_tail.shape[1], D_FF)[:, -1, -(FFN_CONV_W - 1):]
    new_conv = u.reshape(batch, t, CONV_CH)[:, t - (CONV_W - 1):]
    mem5 = lambda a: a.reshape(1, batch, n_mem, MEM_HEADS, MEM_DH)
    return (y.reshape(batch, t, D_MODEL), kv5(kc), kv5(vc), kv5(ks), kv5(vs), win(kw), win(vw),
            new_conv[None], new_ffn[None], mem5(mk), mem5(mv))


def _pad_axis(a, axis, size):
    pads = [(0, 0)] * a.ndim
    pads[axis] = (0, size - a.shape[axis])
    return jnp.pad(a, pads)


def _sample_forward(x_sample, pools, k_win, v_win, conv_state, ffn_state, mem_k, mem_v, page_table, w):
    nb, dec_seq, _ = x_sample.shape
    m = nb * dec_seq
    past_len = page_table.shape[1] * PAGE
    assert dec_seq <= HEAD_ROWS and k_win.shape[1] == WINDOW
    x = x_sample.reshape(m, D_MODEL)
    (q, kc, vc, ks, vs, kw, vw, ksb, vsb, kwb, vwb, gates, u) = _in_proj(x, *w["in_proj"], tm=m)
    pool_kc, pool_vc, pool_ks, pool_vs = pools
    kcc = _compress_sample(pool_kc, page_table, w["cmp_k"])
    vcc = _compress_sample(pool_vc, page_table, w["cmp_v"])

    q5 = q.reshape(nb, dec_seq, N_KV, GQA, DH)
    q_hm = _pad_axis(q5.transpose(0, 2, 3, 1, 4), 3, HEAD_ROWS).reshape(nb, N_KV, GQA * HEAD_ROWS, DH)
    q_tm = _pad_axis(q5.transpose(0, 2, 1, 3, 4), 3, TOKEN_ROWS).reshape(nb, N_KV, dec_seq * TOKEN_ROWS, DH)
    oc_hm, idx = _cmp_select_sample(q_hm, kcc, vcc, past_len)
    oc_tm = oc_hm.reshape(nb, N_KV, GQA, HEAD_ROWS, DH)[:, :, :, :dec_seq].transpose(0, 1, 3, 2, 4)
    oc_tm = _pad_axis(oc_tm, 3, TOKEN_ROWS).reshape(nb, N_KV, dec_seq * TOKEN_ROWS, DH)
    idx_flat = idx[:, :, :dec_seq, :TOP_N].reshape(nb, N_KV * dec_seq * TOP_N)
    idx_pad = _pad_axis(idx, 2, TOKEN_ROWS)
    new_rows = lambda a: _pad_axis(a.reshape(nb, dec_seq, N_KV, DH).transpose(0, 2, 1, 3), 2, TOKEN_ROWS)
    gates_tm = gates.reshape(nb, dec_seq, N_KV, LANES)[..., :GQA * N_GATE].reshape(nb, dec_seq, N_KV, GQA, N_GATE)
    gates_tm = _pad_axis(_pad_axis(gates_tm.transpose(0, 2, 1, 3, 4), 3, TOKEN_ROWS), 4, LANES)
    gates_tm = gates_tm.reshape(nb, N_KV, dec_seq * TOKEN_ROWS, LANES)
    o_tm = _attn_sample(page_table, idx_flat, q_tm, idx_pad, new_rows(ksb), new_rows(vsb), new_rows(kwb),
                        new_rows(vwb), k_win, v_win,
                        oc_tm, gates_tm, pool_ks, pool_vs, dec_seq, past_len)
    o_attn = o_tm.reshape(nb, N_KV, dec_seq, TOKEN_ROWS, DH)[:, :, :, :GQA].transpose(2, 0, 1, 3, 4).reshape(m, ATTN_W)
    step_major = lambda a: a.reshape(nb, -1, a.shape[-1]).transpose(1, 0, 2).reshape(-1, a.shape[-1])
    batch_major = lambda a: a.reshape(-1, nb, a.shape[-1]).transpose(1, 0, 2)

    g_mem, w_mq, w_mo = w["mem"]
    x1, qm = _mix_out_sample(o_attn, step_major(u), step_major(conv_state), step_major(x), w["mix"], g_mem, w_mq,
                             nb, dec_seq)
    n_mem = mem_k.shape[1]
    q_pad = _pad_axis(batch_major(qm), 1, TOKEN_ROWS).astype(BF16)
    a = _mem_attn_sample(q_pad, mem_k, mem_v)
    y, new_ffn = _ffn_sample(x1, step_major(a[:, :dec_seq]), w_mo, w["ffn"], step_major(ffn_state), nb, dec_seq)

    kv5 = lambda a: a.reshape(1, nb, dec_seq, N_KV, DH)
    shift = lambda buf, new: jnp.concatenate([buf[:, dec_seq:], new.reshape((nb, dec_seq) + buf.shape[2:])], axis=1)[None]
    return (batch_major(y), kv5(kc), kv5(vc), kv5(ks), kv5(vs), shift(k_win, kw), shift(v_win, vw),
            shift(conv_state, u), batch_major(new_ffn)[None])


def kernel(x_prompt, x_sample, cache_k_cmp, cache_v_cmp, cache_k_sel, cache_v_sel, cache_k_win, cache_v_win,
           state_conv, state_ffn_conv, cache_mem_k, cache_mem_v, page_table, mem_prompt,
           norm_mix_g, w_in, b_gate, cmp_k_pe, cmp_k_w1, cmp_k_b1, cmp_k_w2, cmp_v_pe, cmp_v_w1, cmp_v_b1, cmp_v_w2,
           conv_w, conv_b, conv_ln_g, conv_ln_b, grp_norm_attn_g, grp_norm_conv_g, w_out,
           norm_mem_g, mem_norm_g, w_mq, w_mk, w_mv, w_mo,
           norm_ffn_g, w_ffn_gate, w_ffn_up, ffn_conv_w, ffn_conv_b, w_ffn_down, norm_final_g):
    assert w_in.shape[0] == 1, "single-layer step"
    w = _prepare_weights(norm_mix_g[0], w_in[0], b_gate[0],
                         (cmp_k_pe[0], cmp_k_w1[0], cmp_k_b1[0], cmp_k_w2[0]),
                         (cmp_v_pe[0], cmp_v_w1[0], cmp_v_b1[0], cmp_v_w2[0]),
                         conv_w[0], conv_b[0], conv_ln_g[0], conv_ln_b[0], grp_norm_attn_g[0], grp_norm_conv_g[0],
                         w_out[0], norm_mem_g[0], mem_norm_g[0], w_mq[0], w_mk[0], w_mv[0], w_mo[0],
                         norm_ffn_g[0], w_ffn_gate[0], w_ffn_up[0], ffn_conv_w[0], ffn_conv_b[0], w_ffn_down[0],
                         norm_final_g)
    p = _prompt_forward(x_prompt, mem_prompt, w)
    s = _sample_forward(x_sample, (cache_k_cmp[0], cache_v_cmp[0], cache_k_sel[0], cache_v_sel[0]),
                        cache_k_win[0], cache_v_win[0], state_conv[0], state_ffn_conv[0],
                        cache_mem_k[0], cache_mem_v[0], page_table, w)
    return (p[0], s[0]) + p[1:] + s[1:]
```

```python
import functools

import jax
import jax.numpy as jnp
from jax import lax
from jax.experimental import pallas as pl
from jax.experimental.pallas import tpu as pltpu

F32 = jnp.float32
BF16 = jnp.bfloat16
I32 = jnp.int32

D_MODEL = 2048
N_KV = 2
GQA = 4
DH = 128
ATTN_W = N_KV * GQA * DH
KV_W = N_KV * DH
N_GATE = 3
BLOCK_CMP = 32
STRIDE_CMP = 16
CMP_HID = 256
SEL_BLOCK = 64
TOP_N = 16
WINDOW = 512
CONV_CH = D_MODEL - ATTN_W
CONV_W = 31
D_FF = 5632
FFN_CONV_W = 3
MEM_HEADS = 4
MEM_DH = 128
MEM_W = MEM_HEADS * MEM_DH
ATTN_SCALE = DH ** -0.5
Q_PRESCALE = ATTN_SCALE * 1.4426950408889634
MEM_SCALE = MEM_DH ** -0.5
EPS = 1e-6
NEG = -1e30
BIG = 1e30
LANES = 128
VMEM_LIMIT = 56 * 1024 * 1024


def _dot(a, b):
    return jnp.dot(a, b, preferred_element_type=F32)


def _dot_nt(a, b):
    return lax.dot_general(a, b, (((1,), (1,)), ((), ())), preferred_element_type=F32)


def _rms(x):
    return x * lax.rsqrt(jnp.mean(x * x, axis=-1, keepdims=True) + EPS)


def _const_spec(shape):
    return pl.BlockSpec(shape, lambda *_: (0,) * len(shape), pipeline_mode=pl.Buffered(1))


def _params(*sem):
    return pltpu.CompilerParams(dimension_semantics=sem, vmem_limit_bytes=VMEM_LIMIT)


def _in_proj_body(x_ref, g_ref, wq_ref, wkv_ref, wg_ref, bg_ref, wglu_ref,
                  q_ref, kc_ref, vc_ref, ks_ref, vs_ref, kw_ref, vw_ref,
                  ksb_ref, vsb_ref, kwb_ref, vwb_ref, gates_ref, u_ref):
    hb = (_rms(x_ref[...]) * g_ref[...]).astype(BF16)
    half = ATTN_W // 2
    for c in range(2):
        q_ref[:, c * half:(c + 1) * half] = (_dot(hb, wq_ref[:, c * half:(c + 1) * half]) * Q_PRESCALE).astype(BF16)
    f32_outs = (kc_ref, vc_ref, ks_ref, vs_ref, kw_ref, vw_ref)
    bf_outs = (None, None, ksb_ref, vsb_ref, kwb_ref, vwb_ref)
    for c in range(6):
        r = _dot(hb, wkv_ref[:, c * KV_W:(c + 1) * KV_W])
        for k in range(N_KV):
            f32_outs[c][:, k, :] = r[:, k * DH:(k + 1) * DH]
        if bf_outs[c] is not None:
            bf_outs[c][...] = r.astype(BF16)
    gates_ref[...] = jax.nn.sigmoid(_dot(hb, wg_ref[...]) + bg_ref[...])
    cw = 256
    for c in range(CONV_CH // cw):
        a = _dot(hb, wglu_ref[:, c * cw:(c + 1) * cw])
        gt = _dot(hb, wglu_ref[:, CONV_CH + c * cw:CONV_CH + (c + 1) * cw])
        u_ref[:, c * cw:(c + 1) * cw] = a * jax.nn.sigmoid(gt)


def _in_proj(x, g, wq, wkv, wg, bg, wglu, tm):
    m = x.shape[0]
    row = lambda w: pl.BlockSpec((tm, w), lambda i: (i, 0))
    out_shape = ([jax.ShapeDtypeStruct((m, ATTN_W), BF16)]
                 + [jax.ShapeDtypeStruct((m, N_KV, DH), F32)] * 6
                 + [jax.ShapeDtypeStruct((m, KV_W), BF16)] * 4
                 + [jax.ShapeDtypeStruct((m, N_KV * LANES), F32),
                    jax.ShapeDtypeStruct((m, CONV_CH), F32)])
    state = pl.BlockSpec((tm, N_KV, DH), lambda i: (i, 0, 0))
    out_specs = ([row(ATTN_W)] + [state] * 6 + [row(KV_W)] * 4 + [row(N_KV * LANES), row(CONV_CH)])
    return pl.pallas_call(
        _in_proj_body,
        grid=(m // tm,),
        in_specs=[row(D_MODEL), _const_spec(g.shape), _const_spec(wq.shape), _const_spec(wkv.shape),
                  _const_spec(wg.shape), _const_spec(bg.shape), _const_spec(wglu.shape)],
        out_specs=out_specs,
        out_shape=out_shape,
        compiler_params=_params("arbitrary"),
        name="in_proj",
    )(x, g, wq, wkv, wg, bg, wglu)


def _compress_rows(get_lanes, n, pe_ref, w1l_ref, w1t_ref, b1_ref, w2_ref):
    xk = jnp.concatenate([get_lanes(l) for l in range(STRIDE_CMP)], axis=1)
    lead = _dot((xk + pe_ref[0:1, :]).astype(BF16), w1l_ref[...])
    trail = _dot((xk + pe_ref[1:2, :]).astype(BF16), w1t_ref[...])
    trail_next = pltpu.roll(trail, n - 1, axis=0)
    hid = jax.nn.gelu(lead + trail_next + b1_ref[...])
    out = _dot(hid.astype(BF16), w2_ref[...])
    rows = lax.broadcasted_iota(I32, (n, 1), 0)
    return jnp.where(rows < n - 1, out, 0.0)


def _compress_prompt_body(x_ref, pe_ref, w1l_ref, w1t_ref, b1_ref, w2_ref, o_ref):
    n = x_ref.shape[0] // STRIDE_CMP
    for k in range(N_KV):
        get = lambda l, k=k: x_ref[pl.ds(l, n, stride=STRIDE_CMP), k, :]
        o_ref[0, k] = _compress_rows(get, n, pe_ref, w1l_ref, w1t_ref, b1_ref, w2_ref)


def _compress_weights(pe, w1, b1, w2):
    half = STRIDE_CMP * DH
    pe2 = pe.reshape(2, half)
    return pe2, w1[:half].astype(BF16), w1[half:].astype(BF16), b1.reshape(1, CMP_HID), w2.astype(BF16)


def _compress_prompt(rows, cw, batch):
    t = rows.shape[0] // batch
    n = t // STRIDE_CMP
    return pl.pallas_call(
        _compress_prompt_body,
        grid=(batch,),
        in_specs=[pl.BlockSpec((t, N_KV, DH), lambda b: (b, 0, 0))] + [_const_spec(w.shape) for w in cw],
        out_specs=pl.BlockSpec((1, N_KV, n, DH), lambda b: (b, 0, 0, 0)),
        out_shape=jax.ShapeDtypeStruct((batch, N_KV, n, DH), F32),
        compiler_params=_params("arbitrary"),
        name="compress_prompt",
    )(rows, *cw)


def _overlap_matrix(n_cmp, n_blk):
    i = lax.broadcasted_iota(I32, (n_cmp, n_blk), 0) * STRIDE_CMP
    j = lax.broadcasted_iota(I32, (n_cmp, n_blk), 1) * SEL_BLOCK
    ov = jnp.maximum(jnp.minimum(i + BLOCK_CMP, j + SEL_BLOCK) - jnp.maximum(i, j), 0)
    return (ov.astype(F32) * (1.0 / BLOCK_CMP)).astype(BF16)


def _importance(p_sum, ov):
    hi = p_sum.astype(BF16)
    r1 = p_sum - hi.astype(F32)
    mid = r1.astype(BF16)
    lo = (r1 - mid.astype(F32)).astype(BF16)
    return _dot(hi, ov) + _dot(mid, ov) + _dot(lo, ov)


def _softmax_av(s, v):
    p = jnp.exp(s - jnp.max(s, axis=-1, keepdims=True))
    return _dot(p.astype(BF16), v) / jnp.sum(p, axis=-1, keepdims=True)


GATE_ROWS = 16
VROWS = DH + 16


def _nsa_prompt_body(qt_ref, kcc_ref, vcct_ref, ks_ref, vst_ref, kw_ref, vwt_ref, gt_ref, o_ref,
                     acc_scr, bias_scr, *, tq, tk, wq, n_sel):
    i = pl.program_id(2)
    t0 = i * tq
    nq = GQA * tq
    heads = lambda a: jnp.concatenate([a] * GQA, axis=1)
    q_pos = t0 + lax.broadcasted_iota(I32, (1, tq), 1)
    q_pos4 = heads(q_pos)
    qt = jnp.concatenate([qt_ref[g * DH:(g + 1) * DH, :] for g in range(GQA)], axis=1)

    n_cmp = kcc_ref.shape[2]
    cmp_end = lax.broadcasted_iota(I32, (n_cmp, 1), 0) * STRIDE_CMP + (BLOCK_CMP - 1)
    cvis = cmp_end <= q_pos4
    s = jnp.where(cvis, _dot(kcc_ref[0, 0].astype(BF16), qt), NEG)
    e = jnp.where(cvis, jnp.exp2(s - jnp.max(s, axis=0, keepdims=True)), 0.0)
    p = e / jnp.maximum(jnp.sum(e, axis=0, keepdims=True), 1e-30)
    o_cmp = _dot(vcct_ref[0, 0].astype(BF16), p.astype(BF16))
    p_sum = p[:, 0:tq]
    for g in range(1, GQA):
        p_sum = p_sum + p[:, g * tq:(g + 1) * tq]

    ov_i = lax.broadcasted_iota(I32, (n_sel, n_cmp), 1) * STRIDE_CMP
    ov_j = lax.broadcasted_iota(I32, (n_sel, n_cmp), 0) * SEL_BLOCK
    ov = jnp.maximum(jnp.minimum(ov_i + BLOCK_CMP, ov_j + SEL_BLOCK) - jnp.maximum(ov_i, ov_j), 0)
    ov = (ov.astype(F32) * (1.0 / BLOCK_CMP)).astype(BF16)
    hi = p_sum.astype(BF16)
    r1 = p_sum - hi.astype(F32)
    mid = r1.astype(BF16)
    lo = (r1 - mid.astype(F32)).astype(BF16)
    imp = _dot(ov, hi) + _dot(ov, mid) + _dot(ov, lo)
    blk = lax.broadcasted_iota(I32, (n_sel, tq), 0)
    cur = q_pos // SEL_BLOCK
    forced = (blk == 0) | (blk == cur) | (blk == cur - 1)
    s_t = jnp.where(blk * SEL_BLOCK <= q_pos, jnp.where(forced, BIG, imp), -BIG)
    rank = jnp.zeros((n_sel, tq), F32)
    for ib in range(n_sel):
        row = s_t[ib:ib + 1, :]
        beats = (row > s_t) | ((row == s_t) & (blk > ib))
        rank = rank + jnp.where(beats, 1.0, 0.0)
    bias_scr[...] = jnp.where(rank < TOP_N, 0.0, NEG)

    acc_scr[...] = jnp.zeros(acc_scr.shape, F32)
    blocks_per_tile = tk // SEL_BLOCK
    last_tile = ks_ref.shape[0] // tk - 1

    def key_tile(j, state, m, causal):
        jd = jnp.minimum(j, last_tile)
        k0 = pl.multiple_of(jd * tk, tk)
        bias = jnp.concatenate(
            [jnp.broadcast_to(bias_scr[pl.ds(jd * blocks_per_tile + c, 1), :], (SEL_BLOCK, tq))
             for c in range(blocks_per_tile)], axis=0)
        if causal:
            bias = jnp.where(j * tk + lax.broadcasted_iota(I32, (tk, 1), 0) <= q_pos, bias, NEG)
        s = _dot(ks_ref[pl.ds(k0, tk), :], qt) + heads(bias)
        m_new = jnp.maximum(m, jnp.max(s, axis=0, keepdims=True))
        p = jnp.exp2(s - m_new).astype(BF16)
        acc_scr[state] = jnp.exp2(m - m_new) * acc_scr[state] + _dot(vst_ref[:, pl.ds(k0, tk)], p)
        return m_new

    def tile_pair(jp, c):
        return key_tile(2 * jp, 0, c[0], False), key_tile(2 * jp + 1, 1, c[1], False)

    n_pairs = t0 // (2 * tk)
    empty = jnp.full((1, nq), NEG, F32)
    c = lax.fori_loop(0, n_pairs, tile_pair, (empty, empty))
    m0 = key_tile(2 * n_pairs, 0, c[0], True)
    second_is_live = (t0 // tk) % 2 == 1
    m1 = lax.cond(second_is_live, lambda: key_tile(2 * n_pairs + 1, 1, c[1], True), lambda: c[1])
    m = jnp.maximum(m0, m1)
    merged = jnp.exp2(m0 - m) * acc_scr[0] + jnp.exp2(m1 - m) * acc_scr[1]
    o_sel = merged[0:DH] / merged[DH:DH + 1]

    span = WINDOW + wq
    n_sub = tq // wq
    o_sub = []
    for j in range(n_sub):
        sub = lambda a, j=j: jnp.concatenate([a[:, g * tq + j * wq:g * tq + (j + 1) * wq] for g in range(GQA)], axis=1)
        q_sub = sub(q_pos4)
        win0 = pl.multiple_of(jnp.maximum(t0 + j * wq - WINDOW, 0), wq)
        kw_pos = win0 + lax.broadcasted_iota(I32, (span, 1), 0)
        wvis = (kw_pos <= q_sub) & (q_sub - kw_pos < WINDOW)
        s = jnp.where(wvis, _dot(kw_ref[pl.ds(win0, span), :], sub(qt)), NEG)
        p = jnp.exp2(s - jnp.max(s, axis=0, keepdims=True)).astype(BF16)
        r = _dot(vwt_ref[:, pl.ds(win0, span)], p)
        o_sub.append(r[0:DH] / r[DH:DH + 1])
    o_win = jnp.concatenate([o_sub[j][:, g * wq:(g + 1) * wq] for g in range(GQA) for j in range(n_sub)], axis=1)

    gt = gt_ref[...]
    gate = lambda c: jnp.concatenate([gt[g * N_GATE + c:g * N_GATE + c + 1, :] for g in range(GQA)], axis=1)
    o = gate(0) * o_cmp + gate(1) * o_sel + gate(2) * o_win
    for g in range(GQA):
        o_ref[:, g * DH:(g + 1) * DH] = o[:, g * tq:(g + 1) * tq].T


def _values_with_ones(v):
    m = v.shape[0]
    vt = v.T.reshape(N_KV, DH, m)
    return jnp.concatenate([vt, jnp.ones((N_KV, VROWS - DH, m), v.dtype)], axis=1).reshape(N_KV * VROWS, m)


def _nsa_prompt(qt, kcc, vcct, ksb, vst, kwb, vwt, gates_t, batch, tq=512, tk=512, wq=128):
    m = qt.shape[1]
    t = m // batch
    tk = min(tk, t)
    nt = t // tq
    n_cmp = kcc.shape[2]
    n_sel = max(t // SEL_BLOCK, 8)
    assert t % tk == 0 and tk % tq == 0 and tq % wq == 0 and wq % LANES == 0 and t >= WINDOW + wq
    rows = lambda: pl.BlockSpec((t, DH), lambda b, k, i: (b, k))
    cols = lambda: pl.BlockSpec((VROWS, t), lambda b, k, i: (k, b))
    return pl.pallas_call(
        functools.partial(_nsa_prompt_body, tq=tq, tk=tk, wq=wq, n_sel=n_sel),
        grid=(batch, N_KV, nt),
        in_specs=[pl.BlockSpec((GQA * DH, tq), lambda b, k, i: (k, b * nt + i)),
                  pl.BlockSpec((1, 1, n_cmp, DH), lambda b, k, i: (b, k, 0, 0)),
                  pl.BlockSpec((1, 1, DH, n_cmp), lambda b, k, i: (b, k, 0, 0)),
                  rows(), cols(), rows(), cols(),
                  pl.BlockSpec((GATE_ROWS, tq), lambda b, k, i: (k, b * nt + i))],
        out_specs=pl.BlockSpec((tq, GQA * DH), lambda b, k, i: (b * nt + i, k)),
        out_shape=jax.ShapeDtypeStruct((m, ATTN_W), F32),
        scratch_shapes=[pltpu.VMEM((2, VROWS, GQA * tq), F32), pltpu.VMEM((n_sel, tq), F32)],
        compiler_params=_params("arbitrary", "arbitrary", "arbitrary"),
        name="nsa_prompt",
    )(qt, kcc, vcct, ksb, vst, kwb, vwt, gates_t)


def _ln_silu_rms(y, lng_ref, lnb_ref, gn_ref):
    mu = jnp.mean(y, axis=-1, keepdims=True)
    var = jnp.mean(jnp.square(y - mu), axis=-1, keepdims=True)
    y = (y - mu) * lax.rsqrt(var + EPS) * lng_ref[...] + lnb_ref[...]
    y = y * jax.nn.sigmoid(y)
    return (_rms(y) * gn_ref[...]).astype(BF16)


def _mix_out_prompt_body(oa_ref, u_ref, halo_ref, x_ref, cw_ref, cb_ref, lng_ref, lnb_ref, ga_ref, gc_ref,
                         woa_ref, woc_ref, o_ref, ext_scr, conv_scr, *, tm, tiles_per_seq, rc):
    i = pl.program_id(0)
    pad = halo_ref.shape[0]
    first = (i % tiles_per_seq) == 0
    ext_scr[0:pad, :] = jnp.where(first, 0.0, halo_ref[...])
    ext_scr[pad:, :] = u_ref[...]
    off = pad - (CONV_W - 1)
    span = rc + pad

    def chunk(r, carry):
        base = pl.multiple_of(r * rc, rc)
        window = ext_scr[pl.ds(base, span), :]
        acc = jnp.zeros((rc, CONV_CH), F32) + cb_ref[...]
        for res in range(8):
            shifted = window if res == 0 else pltpu.roll(window, span - res, axis=0)
            for k in range(CONV_W):
                if (k + off) % 8 == res:
                    a8 = k + off - res
                    acc = acc + cw_ref[k:k + 1, :] * shifted[a8:a8 + rc]
        conv_scr[pl.ds(base, rc), :] = acc
        return carry

    lax.fori_loop(0, tm // rc, chunk, 0)
    conv_n = _ln_silu_rms(conv_scr[...], lng_ref, lnb_ref, gc_ref)
    attn_n = (_rms(oa_ref[...]) * ga_ref[...]).astype(BF16)
    half = D_MODEL // 2
    for c in range(2):
        cs = slice(c * half, (c + 1) * half)
        o_ref[:, cs] = x_ref[:, cs] + _dot(attn_n, woa_ref[:, cs]) + _dot(conv_n, woc_ref[:, cs])


def _mix_out_prompt(o_attn, u, x, mw, batch, tm=512, rc=32):
    m = x.shape[0]
    t = m // batch
    tm = min(tm, t)
    pad = 32
    row = lambda w: pl.BlockSpec((tm, w), lambda i: (i, 0))
    halo = pl.BlockSpec((pad, CONV_CH), lambda i: (jnp.maximum(i * (tm // pad) - 1, 0), 0))
    return pl.pallas_call(
        functools.partial(_mix_out_prompt_body, tm=tm, tiles_per_seq=t // tm, rc=rc),
        grid=(m // tm,),
        in_specs=[row(ATTN_W), row(CONV_CH), halo, row(D_MODEL)] + [_const_spec(w.shape) for w in mw],
        out_specs=row(D_MODEL),
        out_shape=jax.ShapeDtypeStruct((m, D_MODEL), F32),
        scratch_shapes=[pltpu.VMEM((tm + pad, CONV_CH), F32), pltpu.VMEM((tm, CONV_CH), F32)],
        compiler_params=_params("arbitrary"),
        name="mix_out_prompt",
    )(o_attn, u, u, x, *mw)


def _mem_kv_body(mem_ref, g_ref, wk_ref, wv_ref, k_ref, v_ref):
    mb = (_rms(mem_ref[...]) * g_ref[...]).astype(BF16)
    k_ref[...] = _dot(mb, wk_ref[...])
    v_ref[...] = _dot(mb, wv_ref[...])


def _mem_kv(mem, g, wk, wv, tm=256):
    m = mem.shape[0]
    row = lambda w: pl.BlockSpec((tm, w), lambda i: (i, 0))
    return pl.pallas_call(
        _mem_kv_body,
        grid=(m // tm,),
        in_specs=[row(D_MODEL), _const_spec(g.shape), _const_spec(wk.shape), _const_spec(wv.shape)],
        out_specs=[row(MEM_W), row(MEM_W)],
        out_shape=[jax.ShapeDtypeStruct((m, MEM_W), F32)] * 2,
        compiler_params=_params("arbitrary"),
        name="mem_kv",
    )(mem, g, wk, wv)


def _mem_attn_core(q, mem_k, mem_v):
    outs = []
    for h in range(MEM_HEADS):
        s = _dot_nt(q[:, h * MEM_DH:(h + 1) * MEM_DH].astype(BF16), mem_k(h)) * MEM_SCALE
        outs.append(_softmax_av(s, mem_v(h)))
    return jnp.concatenate(outs, axis=1).astype(BF16)


def _mem_attn_prompt_body(x_ref, g_ref, wq_ref, mk_ref, mv_ref, wo_ref, o_ref):
    x = x_ref[...]
    hb = (_rms(x) * g_ref[...]).astype(BF16)
    head = lambda ref: lambda h: ref[0, :, h * MEM_DH:(h + 1) * MEM_DH].astype(BF16)
    a = _mem_attn_core(_dot(hb, wq_ref[...]), head(mk_ref), head(mv_ref))
    o_ref[...] = x + _dot(a, wo_ref[...])


def _mem_attn_prompt(x, g, wq, mk, mv, wo, batch, tm=512):
    m = x.shape[0]
    tiles_per_seq = m // batch // tm
    n_mem = mk.shape[1]
    row = pl.BlockSpec((tm, D_MODEL), lambda i: (i, 0))
    mem = pl.BlockSpec((1, n_mem, MEM_W), lambda i: (i // tiles_per_seq, 0, 0))
    return pl.pallas_call(
        _mem_attn_prompt_body,
        grid=(m // tm,),
        in_specs=[row, _const_spec(g.shape), _const_spec(wq.shape), mem, mem, _const_spec(wo.shape)],
        out_specs=row,
        out_shape=jax.ShapeDtypeStruct((m, D_MODEL), F32),
        compiler_params=_params("arbitrary"),
        name="mem_attn_prompt",
    )(x, g, wq, mk, mv, wo)


def _ffn_tail(f, nf, x_ref, acc_scr, gf_ref, o_ref):
    @pl.when(f == nf - 1)
    def _():
        o_ref[...] = _rms(x_ref[...] + acc_scr[...]) * gf_ref[...]


def _ffn_prompt_body(x_ref, g_ref, wg0_ref, wu0_ref, cw0_ref, cb0_ref, wd0_ref, wg1_ref, wu1_ref, cw1_ref, cb1_ref,
                     wd1_ref, gf_ref, o_ref, st0_ref, st1_ref, h_scr, acc_scr, carry_scr, gext_scr,
                     *, tm, tiles_per_seq, n_tiles):
    i = pl.program_id(0)
    f = pl.program_id(1)
    hist = carry_scr.shape[1]

    @pl.when(f == 0)
    def _():
        h_scr[...] = (_rms(x_ref[...]) * g_ref[...]).astype(BF16)
        acc_scr[...] = jnp.zeros(acc_scr.shape, F32)

    hb = h_scr[...]
    first = (i % tiles_per_seq) == 0

    def hidden(slot, tile, wg_ref, wu_ref, cw_ref, cb_ref, st_ref):
        gate = _dot(hb, wg_ref[...])
        gext_scr[slot, 0:hist, :] = jnp.where(first, 0.0, carry_scr[tile])
        gext_scr[slot, hist:, :] = gate
        tail = gate[tm - hist:, :]
        carry_scr[tile] = tail
        st_ref[0, 0] = tail
        conv = (cw_ref[0:1, :] * gext_scr[slot, hist - 2:hist - 2 + tm, :]
                + cw_ref[1:2, :] * gext_scr[slot, hist - 1:hist - 1 + tm, :] + cw_ref[2:3, :] * gate + cb_ref[...])
        return conv * jax.nn.sigmoid(conv) * _dot(hb, wu_ref[...])

    def down(slot, tile, refs):
        wg_ref, wu_ref, cw_ref, cb_ref, wd_ref, st_ref = refs
        return _dot(hidden(slot, tile, wg_ref, wu_ref, cw_ref, cb_ref, st_ref).astype(BF16), wd_ref[...])

    tile0 = (wg0_ref, wu0_ref, cw0_ref, cb0_ref, wd0_ref, st0_ref)
    tile1 = (wg1_ref, wu1_ref, cw1_ref, cb1_ref, wd1_ref, st1_ref)
    pair = 2 * f + 1 < n_tiles

    @pl.when(pair)
    def _():
        acc_scr[...] += down(0, 2 * f, tile0) + down(1, 2 * f + 1, tile1)

    @pl.when(jnp.logical_not(pair))
    def _():
        acc_scr[...] += down(0, 2 * f, tile0)
        st1_ref[...] = jnp.zeros(st1_ref.shape, F32)

    _ffn_tail(f, pl.num_programs(1), x_ref, acc_scr, gf_ref, o_ref)


def _ffn_prompt(x, g, wg, wu, cw, cb, wd, gf, batch, tm=512, tf=512):
    m = x.shape[0]
    t = m // batch
    tm = min(tm, t)
    n_tiles = D_FF // tf
    n_steps = -(-n_tiles // 2)
    hist = 8
    row = pl.BlockSpec((tm, D_MODEL), lambda i, f: (i, 0))
    tile_of = (lambda f: 2 * f, lambda f: jnp.minimum(2 * f + 1, n_tiles - 1))
    col = lambda r, s: pl.BlockSpec((r, tf), lambda i, f: (0, tile_of[s](f)))
    weights = lambda s: [col(D_MODEL, s), col(D_MODEL, s), col(cw.shape[0], s), col(1, s),
                         pl.BlockSpec((tf, D_MODEL), lambda i, f: (tile_of[s](f), 0))]
    st_spec = pl.BlockSpec((1, 1, hist, tf), lambda i, f: (i, f, 0, 0))
    st_shape = jax.ShapeDtypeStruct((m // tm, n_steps, hist, tf), F32)
    y, st0, st1 = pl.pallas_call(
        functools.partial(_ffn_prompt_body, tm=tm, tiles_per_seq=t // tm, n_tiles=n_tiles),
        grid=(m // tm, n_steps),
        in_specs=[row, _const_spec(g.shape)] + weights(0) + weights(1) + [_const_spec(gf.shape)],
        out_specs=[row, st_spec, st_spec],
        out_shape=[jax.ShapeDtypeStruct((m, D_MODEL), F32), st_shape, st_shape],
        scratch_shapes=[pltpu.VMEM((tm, D_MODEL), BF16), pltpu.VMEM((tm, D_MODEL), F32),
                        pltpu.VMEM((n_tiles, hist, tf), F32), pltpu.VMEM((2, tm + hist, tf), F32)],
        compiler_params=_params("arbitrary", "arbitrary"),
        name="ffn_prompt",
    )(x, g, wg, wu, cw, cb, wd, wg, wu, cw, cb, wd, gf)
    st = jnp.stack([st0, st1], axis=2).transpose(0, 3, 1, 2, 4).reshape(m // tm, hist, 2 * n_steps * tf)
    return y, st[:, :, :D_FF]


PAGE = 128
CHUNKS_PER_PAGE = PAGE // STRIDE_CMP


def _compress_sample_body(pt_ref, pool_ref, pe_ref, w1l_ref, w1t_ref, b1_ref, w2_ref, o_ref, buf, sem, *, n_pages):
    b = pl.program_id(0)
    slot = b % 2

    def fetch(bb, sl):
        def one(j, carry):
            page = pt_ref[bb, j]
            for k in range(N_KV):
                pltpu.make_async_copy(pool_ref.at[page, :, k, :],
                                      buf.at[sl, k, pl.ds(pl.multiple_of(j * PAGE, PAGE), PAGE), :], sem.at[sl]).start()
            return carry
        lax.fori_loop(0, n_pages, one, 0, unroll=8)

    @pl.when(b == 0)
    def _():
        fetch(0, 0)

    @pl.when(b + 1 < pl.num_programs(0))
    def _():
        fetch(b + 1, 1 - slot)

    pltpu.make_async_copy(buf.at[slot], buf.at[slot], sem.at[slot]).wait()
    n = n_pages * CHUNKS_PER_PAGE
    for k in range(N_KV):
        get = lambda l, k=k: buf[slot, k, pl.ds(l, n, stride=STRIDE_CMP), :]
        o_ref[0, k] = _compress_rows(get, n, pe_ref, w1l_ref, w1t_ref, b1_ref, w2_ref)


def _compress_sample(pool, page_table, cw):
    nb, n_pages = page_table.shape
    n = n_pages * CHUNKS_PER_PAGE
    grid_spec = pltpu.PrefetchScalarGridSpec(
        num_scalar_prefetch=1,
        grid=(nb,),
        in_specs=[pl.BlockSpec(memory_space=pl.ANY)] + [_const_spec(w.shape) for w in cw],
        out_specs=pl.BlockSpec((1, N_KV, n, DH), lambda b, pt: (b, 0, 0, 0)),
        scratch_shapes=[pltpu.VMEM((2, N_KV, n_pages * PAGE, DH), F32), pltpu.SemaphoreType.DMA((2,))],
    )
    return pl.pallas_call(
        functools.partial(_compress_sample_body, n_pages=n_pages),
        grid_spec=grid_spec,
        out_shape=jax.ShapeDtypeStruct((nb, N_KV, n, DH), F32),
        compiler_params=_params("arbitrary"),
        name="compress_sample",
    )(page_table, pool, *cw)


HEAD_ROWS = 8
TOKEN_ROWS = 16


def _cmp_select_sample_body(q_ref, kcc_ref, vcc_ref, oc_ref, idx_ref, *, past_len, n_blk):
    bb = q_ref.shape[0]
    n_cmp = kcc_ref.shape[2]
    rows = GQA * HEAD_ROWS
    q_pos = past_len + lax.broadcasted_iota(I32, (rows, 1), 0) % HEAD_ROWS
    cmp_end = lax.broadcasted_iota(I32, (1, n_cmp), 1) * STRIDE_CMP + (BLOCK_CMP - 1)
    cmask = cmp_end <= q_pos
    jobs = [(bi, k) for bi in range(bb) for k in range(N_KV)]
    scores = [jnp.where(cmask, _dot_nt(q_ref[bi, k], kcc_ref[bi, k].astype(BF16)), NEG) for bi, k in jobs]
    p_sums = []
    for s, (bi, k) in zip(scores, jobs):
        e = jnp.where(cmask, jnp.exp2(s - jnp.max(s, axis=-1, keepdims=True)), 0.0)
        p = e / jnp.maximum(jnp.sum(e, axis=-1, keepdims=True), 1e-30)
        oc_ref[bi, k] = _dot(p.astype(BF16), vcc_ref[bi, k].astype(BF16))
        p_sum = p[0:HEAD_ROWS]
        for g in range(1, GQA):
            p_sum = p_sum + p[g * HEAD_ROWS:(g + 1) * HEAD_ROWS]
        p_sums.append(p_sum)
    n_rows = bb * N_KV * HEAD_ROWS
    imp = _importance(jnp.concatenate(p_sums, axis=0), _overlap_matrix(n_cmp, n_blk))
    q_pos_r = past_len + lax.broadcasted_iota(I32, (n_rows, 1), 0) % HEAD_ROWS
    blk = lax.broadcasted_iota(I32, (1, n_blk), 1)
    cur = q_pos_r // SEL_BLOCK
    forced = (blk == 0) | (blk == cur) | (blk == cur - 1)
    score = jnp.where(blk * SEL_BLOCK <= q_pos_r, jnp.where(forced, BIG, imp), -BIG)
    lane = lax.broadcasted_iota(I32, (n_rows, n_blk), 1).astype(F32)
    out_lane = lax.broadcasted_iota(I32, (n_rows, LANES), 1)
    picks = jnp.zeros((n_rows, LANES), F32)
    for n in range(TOP_N):
        best = jnp.max(score, axis=-1, keepdims=True)
        pick = jnp.min(jnp.where(score == best, lane, float(n_blk)), axis=-1, keepdims=True)
        picks = jnp.where(out_lane == n, pick, picks)
        score = jnp.where(lane == pick, -3e38, score)
    picks = picks.astype(I32)
    for r in range(bb * N_KV):
        idx_ref[r // N_KV, r % N_KV] = picks[r * HEAD_ROWS:(r + 1) * HEAD_ROWS]


def _cmp_select_sample(q_hm, kcc, vcc, past_len, bb=8):
    nb = q_hm.shape[0]
    bb = min(bb, nb)
    assert nb % bb == 0
    n_cmp = kcc.shape[2]
    n_blk = -(-(past_len // SEL_BLOCK + 1) // LANES) * LANES
    rows = GQA * HEAD_ROWS
    spec = lambda r, w: pl.BlockSpec((bb, N_KV, r, w), lambda b: (b, 0, 0, 0))
    return pl.pallas_call(
        functools.partial(_cmp_select_sample_body, past_len=past_len, n_blk=n_blk),
        grid=(nb // bb,),
        in_specs=[spec(rows, DH), spec(n_cmp, DH), spec(n_cmp, DH)],
        out_specs=[spec(rows, DH), spec(HEAD_ROWS, LANES)],
        out_shape=[jax.ShapeDtypeStruct((nb, N_KV, rows, DH), F32),
                   jax.ShapeDtypeStruct((nb, N_KV, HEAD_ROWS, LANES), I32)],
        compiler_params=_params("arbitrary"),
        name="cmp_select_sample",
    )(q_hm, kcc, vcc)


def _attn_sample_body(pt_ref, idx_s_ref, q_ref, idx_v_ref, ksn_ref, vsn_ref, kwn_ref, vwn_ref, kwin_ref, vwin_ref,
                      oc_ref, gates_ref, kpool_ref, vpool_ref, o_ref, kbuf, vbuf, sem, *, dec_seq, n_pool_blk):
    step = pl.program_id(0)
    slot = step % 2
    bb = q_ref.shape[0]
    n_kt = N_KV * dec_seq
    assert PAGE == 2 * SEL_BLOCK

    def gather(st, sl):
        for row in range(bb * n_kt):
            bg = st * bb + row // n_kt
            kt = row % n_kt
            k = kt // dec_seq
            for n in range(TOP_N):
                blk = idx_s_ref[bg, kt * TOP_N + n]
                src = jnp.where(blk < n_pool_blk, blk, 0)
                page = pt_ref[bg, lax.shift_right_logical(src, 1)]
                rows = pl.ds(pl.multiple_of((src & 1) * SEL_BLOCK, SEL_BLOCK), SEL_BLOCK)
                dst = pl.ds(n * SEL_BLOCK, SEL_BLOCK)
                pltpu.make_async_copy(kpool_ref.at[page, rows, k], kbuf.at[sl, row, dst], sem.at[sl, 0]).start()
                pltpu.make_async_copy(vpool_ref.at[page, rows, k], vbuf.at[sl, row, dst], sem.at[sl, 1]).start()

    @pl.when(step == 0)
    def _():
        gather(0, 0)

    @pl.when(step + 1 < pl.num_programs(0))
    def _():
        gather(step + 1, 1 - slot)

    pltpu.make_async_copy(kbuf.at[slot], kbuf.at[slot], sem.at[slot, 0]).wait()
    pltpu.make_async_copy(vbuf.at[slot], vbuf.at[slot], sem.at[slot, 1]).wait()

    n_keys = TOP_N * SEL_BLOCK
    rows = dec_seq * TOKEN_ROWS
    key_slot = lax.broadcasted_iota(I32, (LANES, n_keys), 1) // SEL_BLOCK
    expand = (lax.broadcasted_iota(I32, (LANES, n_keys), 0) == key_slot).astype(BF16)
    new_col = lax.broadcasted_iota(I32, (1, TOKEN_ROWS), 1)
    t_row = lax.broadcasted_iota(I32, (rows, 1), 0) // TOKEN_ROWS
    n_win = kwin_ref.shape[1]
    win_old_vis = lax.broadcasted_iota(I32, (1, n_win), 1) > t_row
    win_new_vis = new_col <= t_row
    heads = [(bi, k) for bi in range(bb) for k in range(N_KV)]
    jobs = []
    for bi, k in heads:
        qk = q_ref[bi, k]
        pool_ok = _dot((idx_v_ref[bi, k] < n_pool_blk).astype(BF16), expand)
        jobs.append((jnp.where(win_old_vis, _dot_nt(qk, kwin_ref[bi, :, k, :].astype(BF16)), NEG),
                     jnp.where(win_new_vis, _dot_nt(qk, kwn_ref[bi, k]), NEG),
                     lambda bi=bi, k=k: vwin_ref[bi, :, k, :].astype(BF16), vwn_ref[bi, k]))
        for t in range(dec_seq):
            qt = qk[t * TOKEN_ROWS:(t + 1) * TOKEN_ROWS]
            kt = bi * n_kt + k * dec_seq + t
            jobs.append((jnp.where(pool_ok[t:t + 1, :] > 0.5, _dot_nt(qt, kbuf[slot, kt].astype(BF16)), NEG),
                         jnp.where(new_col <= t, _dot_nt(qt, ksn_ref[bi, k]), NEG),
                         lambda kt=kt: vbuf[slot, kt].astype(BF16), vsn_ref[bi, k]))
    probs = []
    for s_a, s_b, _, _ in jobs:
        m = jnp.maximum(jnp.max(s_a, axis=-1, keepdims=True), jnp.max(s_b, axis=-1, keepdims=True))
        p_a, p_b = jnp.exp2(s_a - m), jnp.exp2(s_b - m)
        probs.append((p_a.astype(BF16), p_b.astype(BF16),
                      jnp.sum(p_a, axis=-1, keepdims=True) + jnp.sum(p_b, axis=-1, keepdims=True)))
    outs = [(_dot(p_a, v_a()) + _dot(p_b, v_b)) / l for (p_a, p_b, l), (_, _, v_a, v_b) in zip(probs, jobs)]
    for h, (bi, k) in enumerate(heads):
        o_win = outs[h * (dec_seq + 1)]
        gates = gates_ref[bi, k]
        for t in range(dec_seq):
            ts = slice(t * TOKEN_ROWS, (t + 1) * TOKEN_ROWS)
            gt = gates[ts]
            o_ref[bi, k, ts, :] = (gt[:, 0:1] * oc_ref[bi, k, ts, :] + gt[:, 1:2] * outs[h * (dec_seq + 1) + 1 + t]
                                   + gt[:, 2:3] * o_win[ts])


def _attn_sample(page_table, idx_flat, q_tm, idx_pad, ksn, vsn, kwn, vwn, k_win, v_win, oc_tm, gates_tm,
                 k_pool, v_pool, dec_seq, past_len):
    nb = q_tm.shape[0]
    bb = 1
    rows = dec_seq * TOKEN_ROWS
    n_win = k_win.shape[1]
    spec = lambda r, w: pl.BlockSpec((bb, N_KV, r, w), lambda b, *_: (b, 0, 0, 0))
    win = pl.BlockSpec((bb, n_win, N_KV, DH), lambda b, *_: (b, 0, 0, 0))
    any_spec = pl.BlockSpec(memory_space=pl.ANY)
    grid_spec = pltpu.PrefetchScalarGridSpec(
        num_scalar_prefetch=2,
        grid=(nb // bb,),
        in_specs=[spec(rows, DH), spec(TOKEN_ROWS, LANES), spec(TOKEN_ROWS, DH), spec(TOKEN_ROWS, DH),
                  spec(TOKEN_ROWS, DH), spec(TOKEN_ROWS, DH), win, win, spec(rows, DH), spec(rows, LANES),
                  any_spec, any_spec],
        out_specs=spec(rows, DH),
        scratch_shapes=[pltpu.VMEM((2, bb * N_KV * dec_seq, TOP_N * SEL_BLOCK, DH), F32),
                        pltpu.VMEM((2, bb * N_KV * dec_seq, TOP_N * SEL_BLOCK, DH), F32),
                        pltpu.SemaphoreType.DMA((2, 2))],
    )
    return pl.pallas_call(
        functools.partial(_attn_sample_body, dec_seq=dec_seq, n_pool_blk=past_len // SEL_BLOCK),
        grid_spec=grid_spec,
        out_shape=jax.ShapeDtypeStruct((nb, N_KV, rows, DH), F32),
        compiler_params=_params("arbitrary"),
        name="attn_sample",
    )(page_table, idx_flat, q_tm, idx_pad, ksn, vsn, kwn, vwn, k_win, v_win, oc_tm, gates_tm, k_pool, v_pool)


def _mix_out_sample_body(oa_ref, u_ref, st_ref, x_ref, cw_ref, cb_ref, lng_ref, lnb_ref, ga_ref, gc_ref,
                         woa_ref, woc_ref, gm_ref, wmq_ref, o_ref, qm_ref, conv_scr, *, nb, dec_seq):
    hist = CONV_W - 1

    def ext(j):
        if j < hist:
            return st_ref[j * nb:(j + 1) * nb, :]
        return u_ref[(j - hist) * nb:(j - hist + 1) * nb, :]

    for t in range(dec_seq):
        acc = jnp.zeros((nb, CONV_CH), F32) + cb_ref[...]
        for k in range(CONV_W):
            acc = acc + cw_ref[k:k + 1, :] * ext(t + k)
        conv_scr[t * nb:(t + 1) * nb, :] = acc
    conv_n = _ln_silu_rms(conv_scr[...], lng_ref, lnb_ref, gc_ref)
    attn_n = (_rms(oa_ref[...]) * ga_ref[...]).astype(BF16)
    x1 = x_ref[...] + _dot(attn_n, woa_ref[...]) + _dot(conv_n, woc_ref[...])
    o_ref[...] = x1
    qm_ref[...] = _dot((_rms(x1) * gm_ref[...]).astype(BF16), wmq_ref[...])


def _mix_out_sample(o_attn, u, conv_state, x, mw, g_mem, w_mq, nb, dec_seq):
    m = x.shape[0]
    args = (o_attn, u, conv_state, x) + tuple(mw) + (g_mem, w_mq)
    return pl.pallas_call(
        functools.partial(_mix_out_sample_body, nb=nb, dec_seq=dec_seq),
        grid=(1,),
        in_specs=[_const_spec(a.shape) for a in args],
        out_specs=[_const_spec((m, D_MODEL)), _const_spec((m, MEM_W))],
        out_shape=[jax.ShapeDtypeStruct((m, D_MODEL), F32), jax.ShapeDtypeStruct((m, MEM_W), F32)],
        scratch_shapes=[pltpu.VMEM((m, CONV_CH), F32)],
        compiler_params=_params("arbitrary"),
        name="mix_out_sample",
    )(*args)


def _mem_attn_sample_body(q_ref, mk_ref, mv_ref, o_ref):
    jobs = [(bi, h) for bi in range(q_ref.shape[0]) for h in range(MEM_HEADS)]
    scores = [_dot_nt(q_ref[bi, :, h * MEM_DH:(h + 1) * MEM_DH], mk_ref[bi, :, h, :].astype(BF16)) * MEM_SCALE
              for bi, h in jobs]
    probs = [jnp.exp(s - jnp.max(s, axis=-1, keepdims=True)) for s in scores]
    outs = [_dot(p.astype(BF16), mv_ref[bi, :, h, :].astype(BF16)) / jnp.sum(p, axis=-1, keepdims=True)
            for p, (bi, h) in zip(probs, jobs)]
    for bi in range(q_ref.shape[0]):
        o_ref[bi] = jnp.concatenate(outs[bi * MEM_HEADS:(bi + 1) * MEM_HEADS], axis=1).astype(BF16)


def _mem_attn_sample(q_pad, mk, mv, bb=4):
    nb, rows, _ = q_pad.shape
    bb = bb if nb % bb == 0 else 1
    n_mem = mk.shape[1]
    q_spec = pl.BlockSpec((bb, rows, MEM_W), lambda b: (b, 0, 0))
    mem = pl.BlockSpec((bb, n_mem, MEM_HEADS, MEM_DH), lambda b: (b, 0, 0, 0))
    return pl.pallas_call(
        _mem_attn_sample_body,
        grid=(nb // bb,),
        in_specs=[q_spec, mem, mem],
        out_specs=q_spec,
        out_shape=jax.ShapeDtypeStruct((nb, rows, MEM_W), BF16),
        compiler_params=_params("arbitrary"),
        name="mem_attn_sample",
    )(q_pad, mk, mv)


def _ffn_sample_body(x_ref, a_ref, wo_ref, g_ref, wg_ref, wu_ref, cw_ref, cb_ref, wd_ref, gf_ref, st_ref,
                     o_ref, sto_ref, x2_scr, h_scr, acc_scr, gate_scr, conv_scr, *, nb, dec_seq):
    f = pl.program_id(0)
    hist = FFN_CONV_W - 1

    @pl.when(f == 0)
    def _():
        x2 = x_ref[...] + _dot(a_ref[...], wo_ref[...])
        x2_scr[...] = x2
        h_scr[...] = (_rms(x2) * g_ref[...]).astype(BF16)
        acc_scr[...] = jnp.zeros(acc_scr.shape, F32)

    hb = h_scr[...]
    gate_scr[...] = _dot(hb, wg_ref[...])

    def ext(j):
        if j < hist:
            return st_ref[j * nb:(j + 1) * nb, :]
        return gate_scr[(j - hist) * nb:(j - hist + 1) * nb, :]

    for t in range(dec_seq):
        acc = cb_ref[...] + cw_ref[0:1, :] * ext(t)
        for k in range(1, FFN_CONV_W):
            acc = acc + cw_ref[k:k + 1, :] * ext(t + k)
        conv_scr[t * nb:(t + 1) * nb, :] = acc
    for j in range(hist):
        sto_ref[j * nb:(j + 1) * nb, :] = ext(dec_seq + j)
    conv = conv_scr[...]
    a = conv * jax.nn.sigmoid(conv) * _dot(hb, wu_ref[...])
    acc_scr[...] += _dot(a.astype(BF16), wd_ref[...])
    _ffn_tail(f, pl.num_programs(0), x2_scr, acc_scr, gf_ref, o_ref)


def _ffn_sample(x1, a, w_mo, fw, ffn_state, nb, dec_seq, tf=512):
    g, wg, wu, cw, cb, wd, gf = fw
    m = x1.shape[0]
    nf = D_FF // tf
    hist = FFN_CONV_W - 1
    col = lambda r: pl.BlockSpec((r, tf), lambda f: (0, f))
    full = lambda shape: pl.BlockSpec(shape, lambda f: (0,) * len(shape))
    return pl.pallas_call(
        functools.partial(_ffn_sample_body, nb=nb, dec_seq=dec_seq),
        grid=(nf,),
        in_specs=[full(x1.shape), full(a.shape), full(w_mo.shape), full(g.shape), col(D_MODEL), col(D_MODEL),
                  col(cw.shape[0]), col(1), pl.BlockSpec((tf, D_MODEL), lambda f: (f, 0)), full(gf.shape),
                  col(nb * hist)],
        out_specs=[full((m, D_MODEL)), col(nb * hist)],
        out_shape=[jax.ShapeDtypeStruct((m, D_MODEL), F32), jax.ShapeDtypeStruct((nb * hist, D_FF), F32)],
        scratch_shapes=[pltpu.VMEM((m, D_MODEL), F32), pltpu.VMEM((m, D_MODEL), BF16), pltpu.VMEM((m, D_MODEL), F32),
                        pltpu.VMEM((m, tf), F32), pltpu.VMEM((m, tf), F32)],
        compiler_params=_params("arbitrary"),
        name="ffn_sample",
    )(x1, a, w_mo, g, wg, wu, cw, cb, wd, gf, ffn_state)


def _prepare_weights(norm_mix_g, w_in, b_gate, cmp_k, cmp_v, conv_w, conv_b, conv_ln_g, conv_ln_b,
                     grp_norm_attn_g, grp_norm_conv_g, w_out, norm_mem_g, mem_norm_g, w_mq, w_mk, w_mv, w_mo,
                     norm_ffn_g, w_ffn_gate, w_ffn_up, ffn_conv_w, ffn_conv_b, w_ffn_down, norm_final_g):
    vec = lambda v: v.reshape(1, -1)
    kv_end = ATTN_W + 6 * KV_W
    n_gate_cols = N_KV * GQA * N_GATE
    per_kv = GQA * N_GATE
    wg = w_in[:, kv_end:kv_end + n_gate_cols].reshape(D_MODEL, N_KV, per_kv)
    wg = jnp.pad(wg, ((0, 0), (0, 0), (0, LANES - per_kv))).reshape(D_MODEL, N_KV * LANES)
    bg = jnp.pad(b_gate.reshape(N_KV, per_kv), ((0, 0), (0, LANES - per_kv))).reshape(1, N_KV * LANES)
    return dict(
        in_proj=(vec(norm_mix_g), w_in[:, :ATTN_W].astype(BF16), w_in[:, ATTN_W:kv_end].astype(BF16),
                 wg.astype(BF16), bg, w_in[:, kv_end + n_gate_cols:].astype(BF16)),
        cmp_k=_compress_weights(*cmp_k),
        cmp_v=_compress_weights(*cmp_v),
        mix=(jnp.pad(conv_w, ((0, 32 - CONV_W), (0, 0))), vec(conv_b), vec(conv_ln_g), vec(conv_ln_b),
             vec(grp_norm_attn_g), vec(grp_norm_conv_g), w_out[:ATTN_W].astype(BF16), w_out[ATTN_W:].astype(BF16)),
        mem_kv=(vec(mem_norm_g), w_mk.astype(BF16), w_mv.astype(BF16)),
        mem=(vec(norm_mem_g), w_mq.astype(BF16), w_mo.astype(BF16)),
        ffn=(vec(norm_ffn_g), w_ffn_gate.astype(BF16), w_ffn_up.astype(BF16),
             jnp.pad(ffn_conv_w, ((0, 8 - FFN_CONV_W), (0, 0))), vec(ffn_conv_b), w_ffn_down.astype(BF16),
             vec(norm_final_g)),
    )


def _prompt_forward(x_prompt, mem_prompt, w):
    batch, t, _ = x_prompt.shape
    x = x_prompt.reshape(batch * t, D_MODEL)
    (q, kc, vc, ks, vs, kw, vw, ksb, vsb, kwb, vwb, gates, u) = _in_proj(x, *w["in_proj"], tm=512)
    kcc = _compress_prompt(kc, w["cmp_k"], batch)
    vcc = _compress_prompt(vc, w["cmp_v"], batch)
    gates_t = gates.reshape(batch * t, N_KV, LANES)[:, :, :GATE_ROWS].transpose(1, 2, 0)
    o_attn = _nsa_prompt(q.T, kcc, vcc.transpose(0, 1, 3, 2), ksb, _values_with_ones(vsb), kwb, _values_with_ones(vwb),
                         gates_t.reshape(N_KV * GATE_ROWS, batch * t), batch)
    x1 = _mix_out_prompt(o_attn, u, x, w["mix"], batch)
    n_mem = mem_prompt.shape[1]
    mk, mv = _mem_kv(mem_prompt.reshape(batch * n_mem, D_MODEL), *w["mem_kv"])
    g_mem, w_mq, w_mo = w["mem"]
    x2 = _mem_attn_prompt(x1, g_mem, w_mq, mk.reshape(batch, n_mem, MEM_W), mv.reshape(batch, n_mem, MEM_W),
                          w_mo, batch)
    y, ffn_tail = _ffn_prompt(x2, *w["ffn"], batch=batch)
    kv5 = lambda a: a.reshape(1, batch, t, N_KV, DH)
    win = lambda a: a.reshape(batch, t, N_KV, DH)[None, :, t - min(WINDOW, t):]
    tiles = ffn_tail.shape[0] // batch
    new_ffn = ffn_tail.reshape(batch, tiles, ffn_tail.shape[1], D_FF)[:, -1, -(FFN_CONV_W - 1):]
    new_conv = u.reshape(batch, t, CONV_CH)[:, t - (CONV_W - 1):]
    mem5 = lambda a: a.reshape(1, batch, n_mem, MEM_HEADS, MEM_DH)
    return (y.reshape(batch, t, D_MODEL), kv5(kc), kv5(vc), kv5(ks), kv5(vs), win(kw), win(vw),
            new_conv[None], new_ffn[None], mem5(mk), mem5(mv))


def _pad_axis(a, axis, size):
    pads = [(0, 0)] * a.ndim
    pads[axis] = (0, size - a.shape[axis])
    return jnp.pad(a, pads)


def _sample_forward(x_sample, pools, k_win, v_win, conv_state, ffn_state, mem_k, mem_v, page_table, w):
    nb, dec_seq, _ = x_sample.shape
    m = nb * dec_seq
    past_len = page_table.shape[1] * PAGE
    assert dec_seq <= HEAD_ROWS and k_win.shape[1] == WINDOW
    x = x_sample.reshape(m, D_MODEL)
    (q, kc, vc, ks, vs, kw, vw, ksb, vsb, kwb, vwb, gates, u) = _in_proj(x, *w["in_proj"], tm=m)
    pool_kc, pool_vc, pool_ks, pool_vs = pools
    kcc = _compress_sample(pool_kc, page_table, w["cmp_k"])
    vcc = _compress_sample(pool_vc, page_table, w["cmp_v"])

    q5 = q.reshape(nb, dec_seq, N_KV, GQA, DH)
    q_hm = _pad_axis(q5.transpose(0, 2, 3, 1, 4), 3, HEAD_ROWS).reshape(nb, N_KV, GQA * HEAD_ROWS, DH)
    q_tm = _pad_axis(q5.transpose(0, 2, 1, 3, 4), 3, TOKEN_ROWS).reshape(nb, N_KV, dec_seq * TOKEN_ROWS, DH)
    oc_hm, idx = _cmp_select_sample(q_hm, kcc, vcc, past_len)
    oc_tm = oc_hm.reshape(nb, N_KV, GQA, HEAD_ROWS, DH)[:, :, :, :dec_seq].transpose(0, 1, 3, 2, 4)
    oc_tm = _pad_axis(oc_tm, 3, TOKEN_ROWS).reshape(nb, N_KV, dec_seq * TOKEN_ROWS, DH)
    idx_flat = idx[:, :, :dec_seq, :TOP_N].reshape(nb, N_KV * dec_seq * TOP_N)
    idx_pad = _pad_axis(idx, 2, TOKEN_ROWS)
    new_rows = lambda a: _pad_axis(a.reshape(nb, dec_seq, N_KV, DH).transpose(0, 2, 1, 3), 2, TOKEN_ROWS)
    gates_tm = gates.reshape(nb, dec_seq, N_KV, LANES)[..., :GQA * N_GATE].reshape(nb, dec_seq, N_KV, GQA, N_GATE)
    gates_tm = _pad_axis(_pad_axis(gates_tm.transpose(0, 2, 1, 3, 4), 3, TOKEN_ROWS), 4, LANES)
    gates_tm = gates_tm.reshape(nb, N_KV, dec_seq * TOKEN_ROWS, LANES)
    o_tm = _attn_sample(page_table, idx_flat, q_tm, idx_pad, new_rows(ksb), new_rows(vsb), new_rows(kwb),
                        new_rows(vwb), k_win, v_win,
                        oc_tm, gates_tm, pool_ks, pool_vs, dec_seq, past_len)
    o_attn = o_tm.reshape(nb, N_KV, dec_seq, TOKEN_ROWS, DH)[:, :, :, :GQA].transpose(2, 0, 1, 3, 4).reshape(m, ATTN_W)
    step_major = lambda a: a.reshape(nb, -1, a.shape[-1]).transpose(1, 0, 2).reshape(-1, a.shape[-1])
    batch_major = lambda a: a.reshape(-1, nb, a.shape[-1]).transpose(1, 0, 2)

    g_mem, w_mq, w_mo = w["mem"]
    x1, qm = _mix_out_sample(o_attn, step_major(u), step_major(conv_state), step_major(x), w["mix"], g_mem, w_mq,
                             nb, dec_seq)
    n_mem = mem_k.shape[1]
    q_pad = _pad_axis(batch_major(qm), 1, TOKEN_ROWS).astype(BF16)
    a = _mem_attn_sample(q_pad, mem_k, mem_v)
    y, new_ffn = _ffn_sample(x1, step_major(a[:, :dec_seq]), w_mo, w["ffn"], step_major(ffn_state), nb, dec_seq)

    kv5 = lambda a: a.reshape(1, nb, dec_seq, N_KV, DH)
    shift = lambda buf, new: jnp.concatenate([buf[:, dec_seq:], new.reshape((nb, dec_seq) + buf.shape[2:])], axis=1)[None]
    return (batch_major(y), kv5(kc), kv5(vc), kv5(ks), kv5(vs), shift(k_win, kw), shift(v_win, vw),
            shift(conv_state, u), batch_major(new_ffn)[None])


def kernel(x_prompt, x_sample, cache_k_cmp, cache_v_cmp, cache_k_sel, cache_v_sel, cache_k_win, cache_v_win,
           state_conv, state_ffn_conv, cache_mem_k, cache_mem_v, page_table, mem_prompt,
           norm_mix_g, w_in, b_gate, cmp_k_pe, cmp_k_w1, cmp_k_b1, cmp_k_w2, cmp_v_pe, cmp_v_w1, cmp_v_b1, cmp_v_w2,
           conv_w, conv_b, conv_ln_g, conv_ln_b, grp_norm_attn_g, grp_norm_conv_g, w_out,
           norm_mem_g, mem_norm_g, w_mq, w_mk, w_mv, w_mo,
           norm_ffn_g, w_ffn_gate, w_ffn_up, ffn_conv_w, ffn_conv_b, w_ffn_down, norm_final_g):
    assert w_in.shape[0] == 1, "single-layer step"
    w = _prepare_weights(norm_mix_g[0], w_in[0], b_gate[0],
                         (cmp_k_pe[0], cmp_k_w1[0], cmp_k_b1[0], cmp_k_w2[0]),
                         (cmp_v_pe[0], cmp_v_w1[0], cmp_v_b1[0], cmp_v_w2[0]),
                         conv_w[0], conv_b[0], conv_ln_g[0], conv_ln_b[0], grp_norm_attn_g[0], grp_norm_conv_g[0],
                         w_out[0], norm_mem_g[0], mem_norm_g[0], w_mq[0], w_mk[0], w_mv[0], w_mo[0],
                         norm_ffn_g[0], w_ffn_gate[0], w_ffn_up[0], ffn_conv_w[0], ffn_conv_b[0], w_ffn_down[0],
                         norm_final_g)
    p = _prompt_forward(x_prompt, mem_prompt, w)
    s = _sample_forward(x_sample, (cache_k_cmp[0], cache_v_cmp[0], cache_k_sel[0], cache_v_sel[0]),
                        cache_k_win[0], cache_v_win[0], state_conv[0], state_ffn_conv[0],
                        cache_mem_k[0], cache_mem_v[0], page_table, w)
    return (p[0], s[0]) + p[1:] + s[1:]
```

```python
import functools

import jax
import jax.numpy as jnp
from jax import lax
from jax.experimental import pallas as pl
from jax.experimental.pallas import tpu as pltpu

F32 = jnp.float32
BF16 = jnp.bfloat16
I32 = jnp.int32

D_MODEL = 2048
N_KV = 2
GQA = 4
DH = 128
ATTN_W = N_KV * GQA * DH
KV_W = N_KV * DH
N_GATE = 3
BLOCK_CMP = 32
STRIDE_CMP = 16
CMP_HID = 256
SEL_BLOCK = 64
TOP_N = 16
WINDOW = 512
CONV_CH = D_MODEL - ATTN_W
CONV_W = 31
D_FF = 5632
FFN_CONV_W = 3
MEM_HEADS = 4
MEM_DH = 128
MEM_W = MEM_HEADS * MEM_DH
ATTN_SCALE = DH ** -0.5
Q_PRESCALE = ATTN_SCALE * 1.4426950408889634
MEM_SCALE = MEM_DH ** -0.5
EPS = 1e-6
NEG = -1e30
BIG = 1e30
LANES = 128
VMEM_LIMIT = 56 * 1024 * 1024


def _dot(a, b):
    return jnp.dot(a, b, preferred_element_type=F32)


def _dot_nt(a, b):
    return lax.dot_general(a, b, (((1,), (1,)), ((), ())), preferred_element_type=F32)


def _rms(x):
    return x * lax.rsqrt(jnp.mean(x * x, axis=-1, keepdims=True) + EPS)


def _const_spec(shape):
    return pl.BlockSpec(shape, lambda *_: (0,) * len(shape), pipeline_mode=pl.Buffered(1))


def _params(*sem):
    return pltpu.CompilerParams(dimension_semantics=sem, vmem_limit_bytes=VMEM_LIMIT)


def _in_proj_body(x_ref, g_ref, wq_ref, wkv_ref, wg_ref, bg_ref, wglu_ref,
                  q_ref, kc_ref, vc_ref, ks_ref, vs_ref, kw_ref, vw_ref,
                  ksb_ref, vsb_ref, kwb_ref, vwb_ref, gates_ref, u_ref):
    hb = (_rms(x_ref[...]) * g_ref[...]).astype(BF16)
    half = ATTN_W // 2
    for c in range(2):
        q_ref[:, c * half:(c + 1) * half] = (_dot(hb, wq_ref[:, c * half:(c + 1) * half]) * Q_PRESCALE).astype(BF16)
    f32_outs = (kc_ref, vc_ref, ks_ref, vs_ref, kw_ref, vw_ref)
    bf_outs = (None, None, ksb_ref, vsb_ref, kwb_ref, vwb_ref)
    for c in range(6):
        r = _dot(hb, wkv_ref[:, c * KV_W:(c + 1) * KV_W])
        for k in range(N_KV):
            f32_outs[c][:, k, :] = r[:, k * DH:(k + 1) * DH]
        if bf_outs[c] is not None:
            bf_outs[c][...] = r.astype(BF16)
    gates_ref[...] = jax.nn.sigmoid(_dot(hb, wg_ref[...]) + bg_ref[...])
    cw = 256
    for c in range(CONV_CH // cw):
        a = _dot(hb, wglu_ref[:, c * cw:(c + 1) * cw])
        gt = _dot(hb, wglu_ref[:, CONV_CH + c * cw:CONV_CH + (c + 1) * cw])
        u_ref[:, c * cw:(c + 1) * cw] = a * jax.nn.sigmoid(gt)


def _in_proj(x, g, wq, wkv, wg, bg, wglu, tm):
    m = x.shape[0]
    row = lambda w: pl.BlockSpec((tm, w), lambda i: (i, 0))
    out_shape = ([jax.ShapeDtypeStruct((m, ATTN_W), BF16)]
                 + [jax.ShapeDtypeStruct((m, N_KV, DH), F32)] * 6
                 + [jax.ShapeDtypeStruct((m, KV_W), BF16)] * 4
                 + [jax.ShapeDtypeStruct((m, N_KV * LANES), F32),
                    jax.ShapeDtypeStruct((m, CONV_CH), F32)])
    state = pl.BlockSpec((tm, N_KV, DH), lambda i: (i, 0, 0))
    out_specs = ([row(ATTN_W)] + [state] * 6 + [row(KV_W)] * 4 + [row(N_KV * LANES), row(CONV_CH)])
    return pl.pallas_call(
        _in_proj_body,
        grid=(m // tm,),
        in_specs=[row(D_MODEL), _const_spec(g.shape), _const_spec(wq.shape), _const_spec(wkv.shape),
                  _const_spec(wg.shape), _const_spec(bg.shape), _const_spec(wglu.shape)],
        out_specs=out_specs,
        out_shape=out_shape,
        compiler_params=_params("arbitrary"),
        name="in_proj",
    )(x, g, wq, wkv, wg, bg, wglu)


def _compress_rows(get_lanes, n, pe_ref, w1l_ref, w1t_ref, b1_ref, w2_ref):
    xk = jnp.concatenate([get_lanes(l) for l in range(STRIDE_CMP)], axis=1)
    lead = _dot((xk + pe_ref[0:1, :]).astype(BF16), w1l_ref[...])
    trail = _dot((xk + pe_ref[1:2, :]).astype(BF16), w1t_ref[...])
    trail_next = pltpu.roll(trail, n - 1, axis=0)
    hid = jax.nn.gelu(lead + trail_next + b1_ref[...])
    out = _dot(hid.astype(BF16), w2_ref[...])
    rows = lax.broadcasted_iota(I32, (n, 1), 0)
    return jnp.where(rows < n - 1, out, 0.0)


def _compress_prompt_body(x_ref, pe_ref, w1l_ref, w1t_ref, b1_ref, w2_ref, o_ref):
    n = x_ref.shape[0] // STRIDE_CMP
    for k in range(N_KV):
        get = lambda l, k=k: x_ref[pl.ds(l, n, stride=STRIDE_CMP), k, :]
        o_ref[0, k] = _compress_rows(get, n, pe_ref, w1l_ref, w1t_ref, b1_ref, w2_ref)


def _compress_weights(pe, w1, b1, w2):
    half = STRIDE_CMP * DH
    pe2 = pe.reshape(2, half)
    return pe2, w1[:half].astype(BF16), w1[half:].astype(BF16), b1.reshape(1, CMP_HID), w2.astype(BF16)


def _compress_prompt(rows, cw, batch):
    t = rows.shape[0] // batch
    n = t // STRIDE_CMP
    return pl.pallas_call(
        _compress_prompt_body,
        grid=(batch,),
        in_specs=[pl.BlockSpec((t, N_KV, DH), lambda b: (b, 0, 0))] + [_const_spec(w.shape) for w in cw],
        out_specs=pl.BlockSpec((1, N_KV, n, DH), lambda b: (b, 0, 0, 0)),
        out_shape=jax.ShapeDtypeStruct((batch, N_KV, n, DH), F32),
        compiler_params=_params("arbitrary"),
        name="compress_prompt",
    )(rows, *cw)


def _overlap_matrix(n_cmp, n_blk):
    i = lax.broadcasted_iota(I32, (n_cmp, n_blk), 0) * STRIDE_CMP
    j = lax.broadcasted_iota(I32, (n_cmp, n_blk), 1) * SEL_BLOCK
    ov = jnp.maximum(jnp.minimum(i + BLOCK_CMP, j + SEL_BLOCK) - jnp.maximum(i, j), 0)
    return (ov.astype(F32) * (1.0 / BLOCK_CMP)).astype(BF16)


def _importance(p_sum, ov):
    hi = p_sum.astype(BF16)
    r1 = p_sum - hi.astype(F32)
    mid = r1.astype(BF16)
    lo = (r1 - mid.astype(F32)).astype(BF16)
    return _dot(hi, ov) + _dot(mid, ov) + _dot(lo, ov)


def _softmax_av(s, v):
    p = jnp.exp(s - jnp.max(s, axis=-1, keepdims=True))
    return _dot(p.astype(BF16), v) / jnp.sum(p, axis=-1, keepdims=True)


GATE_ROWS = 16
VROWS = DH + 16


def _nsa_prompt_body(qt_ref, kcc_ref, vcct_ref, ks_ref, vst_ref, kw_ref, vwt_ref, gt_ref, o_ref,
                     acc_scr, bias_scr, *, tq, tk, wq, n_sel):
    i = pl.program_id(2)
    t0 = i * tq
    nq = GQA * tq
    heads = lambda a: jnp.concatenate([a] * GQA, axis=1)
    q_pos = t0 + lax.broadcasted_iota(I32, (1, tq), 1)
    q_pos4 = heads(q_pos)
    qt = jnp.concatenate([qt_ref[g * DH:(g + 1) * DH, :] for g in range(GQA)], axis=1)

    n_cmp = kcc_ref.shape[2]
    cmp_end = lax.broadcasted_iota(I32, (n_cmp, 1), 0) * STRIDE_CMP + (BLOCK_CMP - 1)
    cvis = cmp_end <= q_pos4
    s = jnp.where(cvis, _dot(kcc_ref[0, 0].astype(BF16), qt), NEG)
    e = jnp.where(cvis, jnp.exp2(s - jnp.max(s, axis=0, keepdims=True)), 0.0)
    p = e / jnp.maximum(jnp.sum(e, axis=0, keepdims=True), 1e-30)
    o_cmp = _dot(vcct_ref[0, 0].astype(BF16), p.astype(BF16))
    p_sum = p[:, 0:tq]
    for g in range(1, GQA):
        p_sum = p_sum + p[:, g * tq:(g + 1) * tq]

    ov_i = lax.broadcasted_iota(I32, (n_sel, n_cmp), 1) * STRIDE_CMP
    ov_j = lax.broadcasted_iota(I32, (n_sel, n_cmp), 0) * SEL_BLOCK
    ov = jnp.maximum(jnp.minimum(ov_i + BLOCK_CMP, ov_j + SEL_BLOCK) - jnp.maximum(ov_i, ov_j), 0)
    ov = (ov.astype(F32) * (1.0 / BLOCK_CMP)).astype(BF16)
    hi = p_sum.astype(BF16)
    r1 = p_sum - hi.astype(F32)
    mid = r1.astype(BF16)
    lo = (r1 - mid.astype(F32)).astype(BF16)
    imp = _dot(ov, hi) + _dot(ov, mid) + _dot(ov, lo)
    blk = lax.broadcasted_iota(I32, (n_sel, tq), 0)
    cur = q_pos // SEL_BLOCK
    forced = (blk == 0) | (blk == cur) | (blk == cur - 1)
    s_t = jnp.where(blk * SEL_BLOCK <= q_pos, jnp.where(forced, BIG, imp), -BIG)
    rank = jnp.zeros((n_sel, tq), F32)
    for ib in range(n_sel):
        row = s_t[ib:ib + 1, :]
        beats = (row > s_t) | ((row == s_t) & (blk > ib))
        rank = rank + jnp.where(beats, 1.0, 0.0)
    bias_scr[...] = jnp.where(rank < TOP_N, 0.0, NEG)

    acc_scr[...] = jnp.zeros(acc_scr.shape, F32)
    blocks_per_tile = tk // SEL_BLOCK
    last_tile = ks_ref.shape[0] // tk - 1

    def key_tile(j, state, m, causal):
        jd = jnp.minimum(j, last_tile)
        k0 = pl.multiple_of(jd * tk, tk)
        bias = jnp.concatenate(
            [jnp.broadcast_to(bias_scr[pl.ds(jd * blocks_per_tile + c, 1), :], (SEL_BLOCK, tq))
             for c in range(blocks_per_tile)], axis=0)
        if causal:
            bias = jnp.where(j * tk + lax.broadcasted_iota(I32, (tk, 1), 0) <= q_pos, bias, NEG)
        s = _dot(ks_ref[pl.ds(k0, tk), :], qt) + heads(bias)
        m_new = jnp.maximum(m, jnp.max(s, axis=0, keepdims=True))
        p = jnp.exp2(s - m_new).astype(BF16)
        acc_scr[state] = jnp.exp2(m - m_new) * acc_scr[state] + _dot(vst_ref[:, pl.ds(k0, tk)], p)
        return m_new

    def tile_pair(jp, c):
        return key_tile(2 * jp, 0, c[0], False), key_tile(2 * jp + 1, 1, c[1], False)

    n_pairs = t0 // (2 * tk)
    empty = jnp.full((1, nq), NEG, F32)
    c = lax.fori_loop(0, n_pairs, tile_pair, (empty, empty))
    m0 = key_tile(2 * n_pairs, 0, c[0], True)
    second_is_live = (t0 // tk) % 2 == 1
    m1 = lax.cond(second_is_live, lambda: key_tile(2 * n_pairs + 1, 1, c[1], True), lambda: c[1])
    m = jnp.maximum(m0, m1)
    merged = jnp.exp2(m0 - m) * acc_scr[0] + jnp.exp2(m1 - m) * acc_scr[1]
    o_sel = merged[0:DH] / merged[DH:DH + 1]

    span = WINDOW + wq
    n_sub = tq // wq
    o_sub = []
    for j in range(n_sub):
        sub = lambda a, j=j: jnp.concatenate([a[:, g * tq + j * wq:g * tq + (j + 1) * wq] for g in range(GQA)], axis=1)
        q_sub = sub(q_pos4)
        win0 = pl.multiple_of(jnp.maximum(t0 + j * wq - WINDOW, 0), wq)
        kw_pos = win0 + lax.broadcasted_iota(I32, (span, 1), 0)
        wvis = (kw_pos <= q_sub) & (q_sub - kw_pos < WINDOW)
        s = jnp.where(wvis, _dot(kw_ref[pl.ds(win0, span), :], sub(qt)), NEG)
        p = jnp.exp2(s - jnp.max(s, axis=0, keepdims=True)).astype(BF16)
        r = _dot(vwt_ref[:, pl.ds(win0, span)], p)
        o_sub.append(r[0:DH] / r[DH:DH + 1])
    o_win = jnp.concatenate([o_sub[j][:, g * wq:(g + 1) * wq] for g in range(GQA) for j in range(n_sub)], axis=1)

    gt = gt_ref[...]
    gate = lambda c: jnp.concatenate([gt[g * N_GATE + c:g * N_GATE + c + 1, :] for g in range(GQA)], axis=1)
    o = gate(0) * o_cmp + gate(1) * o_sel + gate(2) * o_win
    for g in range(GQA):
        o_ref[:, g * DH:(g + 1) * DH] = o[:, g * tq:(g + 1) * tq].T


def _values_with_ones(v):
    m = v.shape[0]
    vt = v.T.reshape(N_KV, DH, m)
    return jnp.concatenate([vt, jnp.ones((N_KV, VROWS - DH, m), v.dtype)], axis=1).reshape(N_KV * VROWS, m)


def _nsa_prompt(qt, kcc, vcct, ksb, vst, kwb, vwt, gates_t, batch, tq=512, tk=512, wq=128):
    m = qt.shape[1]
    t = m // batch
    tk = min(tk, t)
    nt = t // tq
    n_cmp = kcc.shape[2]
    n_sel = max(t // SEL_BLOCK, 8)
    assert t % tk == 0 and tk % tq == 0 and tq % wq == 0 and wq % LANES == 0 and t >= WINDOW + wq
    rows = lambda: pl.BlockSpec((t, DH), lambda b, k, i: (b, k))
    cols = lambda: pl.BlockSpec((VROWS, t), lambda b, k, i: (k, b))
    return pl.pallas_call(
        functools.partial(_nsa_prompt_body, tq=tq, tk=tk, wq=wq, n_sel=n_sel),
        grid=(batch, N_KV, nt),
        in_specs=[pl.BlockSpec((GQA * DH, tq), lambda b, k, i: (k, b * nt + i)),
                  pl.BlockSpec((1, 1, n_cmp, DH), lambda b, k, i: (b, k, 0, 0)),
                  pl.BlockSpec((1, 1, DH, n_cmp), lambda b, k, i: (b, k, 0, 0)),
                  rows(), cols(), rows(), cols(),
                  pl.BlockSpec((GATE_ROWS, tq), lambda b, k, i: (k, b * nt + i))],
        out_specs=pl.BlockSpec((tq, GQA * DH), lambda b, k, i: (b * nt + i, k)),
        out_shape=jax.ShapeDtypeStruct((m, ATTN_W), F32),
        scratch_shapes=[pltpu.VMEM((2, VROWS, GQA * tq), F32), pltpu.VMEM((n_sel, tq), F32)],
        compiler_params=_params("arbitrary", "arbitrary", "arbitrary"),
        name="nsa_prompt",
    )(qt, kcc, vcct, ksb, vst, kwb, vwt, gates_t)


def _ln_silu_rms(y, lng_ref, lnb_ref, gn_ref):
    mu = jnp.mean(y, axis=-1, keepdims=True)
    var = jnp.mean(jnp.square(y - mu), axis=-1, keepdims=True)
    y = (y - mu) * lax.rsqrt(var + EPS) * lng_ref[...] + lnb_ref[...]
    y = y * jax.nn.sigmoid(y)
    return (_rms(y) * gn_ref[...]).astype(BF16)


def _mix_out_prompt_body(oa_ref, u_ref, halo_ref, x_ref, cw_ref, cb_ref, lng_ref, lnb_ref, ga_ref, gc_ref,
                         woa_ref, woc_ref, o_ref, ext_scr, conv_scr, *, tm, tiles_per_seq, rc, cc):
    i = pl.program_id(0)
    pad = halo_ref.shape[0]
    first = (i % tiles_per_seq) == 0
    ext_scr[0:pad, :] = jnp.where(first, 0.0, halo_ref[...])
    ext_scr[pad:, :] = u_ref[...]
    off = pad - (CONV_W - 1)
    span = rc + pad

    def chunk(r, carry):
        base = pl.multiple_of(r * rc, rc)
        for c0 in range(0, CONV_CH, cc):
            lanes = slice(c0, c0 + cc)
            window = ext_scr[pl.ds(base, span), lanes]
            acc = jnp.zeros((rc, cc), F32) + cb_ref[:, lanes]
            for res in range(8):
                shifted = window if res == 0 else pltpu.roll(window, span - res, axis=0)
                for k in range(CONV_W):
                    if (k + off) % 8 == res:
                        a8 = k + off - res
                        acc = acc + cw_ref[k:k + 1, lanes] * shifted[a8:a8 + rc]
            conv_scr[pl.ds(base, rc), lanes] = acc
        return carry

    lax.fori_loop(0, tm // rc, chunk, 0)
    conv_n = _ln_silu_rms(conv_scr[...], lng_ref, lnb_ref, gc_ref)
    attn_n = (_rms(oa_ref[...]) * ga_ref[...]).astype(BF16)
    half = D_MODEL // 2
    for c in range(2):
        cs = slice(c * half, (c + 1) * half)
        o_ref[:, cs] = x_ref[:, cs] + _dot(attn_n, woa_ref[:, cs]) + _dot(conv_n, woc_ref[:, cs])


def _mix_out_prompt(o_attn, u, x, mw, batch, tm=512, rc=128, cc=128):
    m = x.shape[0]
    t = m // batch
    tm = min(tm, t)
    pad = 32
    row = lambda w: pl.BlockSpec((tm, w), lambda i: (i, 0))
    halo = pl.BlockSpec((pad, CONV_CH), lambda i: (jnp.maximum(i * (tm // pad) - 1, 0), 0))
    return pl.pallas_call(
        functools.partial(_mix_out_prompt_body, tm=tm, tiles_per_seq=t // tm, rc=rc, cc=cc),
        grid=(m // tm,),
        in_specs=[row(ATTN_W), row(CONV_CH), halo, row(D_MODEL)] + [_const_spec(w.shape) for w in mw],
        out_specs=row(D_MODEL),
        out_shape=jax.ShapeDtypeStruct((m, D_MODEL), F32),
        scratch_shapes=[pltpu.VMEM((tm + pad, CONV_CH), F32), pltpu.VMEM((tm, CONV_CH), F32)],
        compiler_params=_params("arbitrary"),
        name="mix_out_prompt",
    )(o_attn, u, u, x, *mw)


def _mem_kv_body(mem_ref, g_ref, wk_ref, wv_ref, k_ref, v_ref):
    mb = (_rms(mem_ref[...]) * g_ref[...]).astype(BF16)
    k_ref[...] = _dot(mb, wk_ref[...])
    v_ref[...] = _dot(mb, wv_ref[...])


def _mem_kv(mem, g, wk, wv, tm=256):
    m = mem.shape[0]
    row = lambda w: pl.BlockSpec((tm, w), lambda i: (i, 0))
    return pl.pallas_call(
        _mem_kv_body,
        grid=(m // tm,),
        in_specs=[row(D_MODEL), _const_spec(g.shape), _const_spec(wk.shape), _const_spec(wv.shape)],
        out_specs=[row(MEM_W), row(MEM_W)],
        out_shape=[jax.ShapeDtypeStruct((m, MEM_W), F32)] * 2,
        compiler_params=_params("arbitrary"),
        name="mem_kv",
    )(mem, g, wk, wv)


def _mem_attn_core(q, mem_k, mem_v):
    outs = []
    for h in range(MEM_HEADS):
        s = _dot_nt(q[:, h * MEM_DH:(h + 1) * MEM_DH].astype(BF16), mem_k(h)) * MEM_SCALE
        outs.append(_softmax_av(s, mem_v(h)))
    return jnp.concatenate(outs, axis=1).astype(BF16)


def _mem_attn_prompt_body(x_ref, g_ref, wq_ref, mk_ref, mv_ref, wo_ref, o_ref):
    x = x_ref[...]
    hb = (_rms(x) * g_ref[...]).astype(BF16)
    head = lambda ref: lambda h: ref[0, :, h * MEM_DH:(h + 1) * MEM_DH].astype(BF16)
    a = _mem_attn_core(_dot(hb, wq_ref[...]), head(mk_ref), head(mv_ref))
    o_ref[...] = x + _dot(a, wo_ref[...])


def _mem_attn_prompt(x, g, wq, mk, mv, wo, batch, tm=512):
    m = x.shape[0]
    tiles_per_seq = m // batch // tm
    n_mem = mk.shape[1]
    row = pl.BlockSpec((tm, D_MODEL), lambda i: (i, 0))
    mem = pl.BlockSpec((1, n_mem, MEM_W), lambda i: (i // tiles_per_seq, 0, 0))
    return pl.pallas_call(
        _mem_attn_prompt_body,
        grid=(m // tm,),
        in_specs=[row, _const_spec(g.shape), _const_spec(wq.shape), mem, mem, _const_spec(wo.shape)],
        out_specs=row,
        out_shape=jax.ShapeDtypeStruct((m, D_MODEL), F32),
        compiler_params=_params("arbitrary"),
        name="mem_attn_prompt",
    )(x, g, wq, mk, mv, wo)


def _ffn_tail(f, nf, x_ref, acc_scr, gf_ref, o_ref):
    @pl.when(f == nf - 1)
    def _():
        o_ref[...] = _rms(x_ref[...] + acc_scr[...]) * gf_ref[...]


def _ffn_prompt_body(x_ref, g_ref, wg0_ref, wu0_ref, cw0_ref, cb0_ref, wd0_ref, wg1_ref, wu1_ref, cw1_ref, cb1_ref,
                     wd1_ref, gf_ref, o_ref, st0_ref, st1_ref, h_scr, acc_scr, carry_scr, gext_scr,
                     *, tm, tiles_per_seq, n_tiles):
    i = pl.program_id(0)
    f = pl.program_id(1)
    hist = carry_scr.shape[1]

    @pl.when(f == 0)
    def _():
        h_scr[...] = (_rms(x_ref[...]) * g_ref[...]).astype(BF16)
        acc_scr[...] = jnp.zeros(acc_scr.shape, F32)

    hb = h_scr[...]
    first = (i % tiles_per_seq) == 0

    def hidden(slot, tile, wg_ref, wu_ref, cw_ref, cb_ref, st_ref):
        gate = _dot(hb, wg_ref[...])
        gext_scr[slot, 0:hist, :] = jnp.where(first, 0.0, carry_scr[tile])
        gext_scr[slot, hist:, :] = gate
        tail = gate[tm - hist:, :]
        carry_scr[tile] = tail
        st_ref[0, 0] = tail
        conv = (cw_ref[0:1, :] * gext_scr[slot, hist - 2:hist - 2 + tm, :]
                + cw_ref[1:2, :] * gext_scr[slot, hist - 1:hist - 1 + tm, :] + cw_ref[2:3, :] * gate + cb_ref[...])
        return conv * jax.nn.sigmoid(conv) * _dot(hb, wu_ref[...])

    def down(slot, tile, refs):
        wg_ref, wu_ref, cw_ref, cb_ref, wd_ref, st_ref = refs
        return _dot(hidden(slot, tile, wg_ref, wu_ref, cw_ref, cb_ref, st_ref).astype(BF16), wd_ref[...])

    tile0 = (wg0_ref, wu0_ref, cw0_ref, cb0_ref, wd0_ref, st0_ref)
    tile1 = (wg1_ref, wu1_ref, cw1_ref, cb1_ref, wd1_ref, st1_ref)
    pair = 2 * f + 1 < n_tiles

    @pl.when(pair)
    def _():
        acc_scr[...] += down(0, 2 * f, tile0) + down(1, 2 * f + 1, tile1)

    @pl.when(jnp.logical_not(pair))
    def _():
        acc_scr[...] += down(0, 2 * f, tile0)
        st1_ref[...] = jnp.zeros(st1_ref.shape, F32)

    _ffn_tail(f, pl.num_programs(1), x_ref, acc_scr, gf_ref, o_ref)


def _ffn_prompt(x, g, wg, wu, cw, cb, wd, gf, batch, tm=512, tf=512):
    m = x.shape[0]
    t = m // batch
    tm = min(tm, t)
    n_tiles = D_FF // tf
    n_steps = -(-n_tiles // 2)
    hist = 8
    row = pl.BlockSpec((tm, D_MODEL), lambda i, f: (i, 0))
    tile_of = (lambda f: 2 * f, lambda f: jnp.minimum(2 * f + 1, n_tiles - 1))
    col = lambda r, s: pl.BlockSpec((r, tf), lambda i, f: (0, tile_of[s](f)))
    weights = lambda s: [col(D_MODEL, s), col(D_MODEL, s), col(cw.shape[0], s), col(1, s),
                         pl.BlockSpec((tf, D_MODEL), lambda i, f: (tile_of[s](f), 0))]
    st_spec = pl.BlockSpec((1, 1, hist, tf), lambda i, f: (i, f, 0, 0))
    st_shape = jax.ShapeDtypeStruct((m // tm, n_steps, hist, tf), F32)
    y, st0, st1 = pl.pallas_call(
        functools.partial(_ffn_prompt_body, tm=tm, tiles_per_seq=t // tm, n_tiles=n_tiles),
        grid=(m // tm, n_steps),
        in_specs=[row, _const_spec(g.shape)] + weights(0) + weights(1) + [_const_spec(gf.shape)],
        out_specs=[row, st_spec, st_spec],
        out_shape=[jax.ShapeDtypeStruct((m, D_MODEL), F32), st_shape, st_shape],
        scratch_shapes=[pltpu.VMEM((tm, D_MODEL), BF16), pltpu.VMEM((tm, D_MODEL), F32),
                        pltpu.VMEM((n_tiles, hist, tf), F32), pltpu.VMEM((2, tm + hist, tf), F32)],
        compiler_params=_params("arbitrary", "arbitrary"),
        name="ffn_prompt",
    )(x, g, wg, wu, cw, cb, wd, wg, wu, cw, cb, wd, gf)
    st = jnp.stack([st0, st1], axis=2).transpose(0, 3, 1, 2, 4).reshape(m // tm, hist, 2 * n_steps * tf)
    return y, st[:, :, :D_FF]


PAGE = 128
CHUNKS_PER_PAGE = PAGE // STRIDE_CMP


def _compress_sample_body(pt_ref, pool_ref, pe_ref, w1l_ref, w1t_ref, b1_ref, w2_ref, o_ref, buf, sem, *, n_pages):
    b = pl.program_id(0)
    slot = b % 2

    def fetch(bb, sl):
        def one(j, carry):
            page = pt_ref[bb, j]
            for k in range(N_KV):
                pltpu.make_async_copy(pool_ref.at[page, :, k, :],
                                      buf.at[sl, k, pl.ds(pl.multiple_of(j * PAGE, PAGE), PAGE), :], sem.at[sl]).start()
            return carry
        lax.fori_loop(0, n_pages, one, 0, unroll=8)

    @pl.when(b == 0)
    def _():
        fetch(0, 0)

    @pl.when(b + 1 < pl.num_programs(0))
    def _():
        fetch(b + 1, 1 - slot)

    pltpu.make_async_copy(buf.at[slot], buf.at[slot], sem.at[slot]).wait()
    n = n_pages * CHUNKS_PER_PAGE
    for k in range(N_KV):
        get = lambda l, k=k: buf[slot, k, pl.ds(l, n, stride=STRIDE_CMP), :]
        o_ref[0, k] = _compress_rows(get, n, pe_ref, w1l_ref, w1t_ref, b1_ref, w2_ref)


def _compress_sample(pool, page_table, cw):
    nb, n_pages = page_table.shape
    n = n_pages * CHUNKS_PER_PAGE
    grid_spec = pltpu.PrefetchScalarGridSpec(
        num_scalar_prefetch=1,
        grid=(nb,),
        in_specs=[pl.BlockSpec(memory_space=pl.ANY)] + [_const_spec(w.shape) for w in cw],
        out_specs=pl.BlockSpec((1, N_KV, n, DH), lambda b, pt: (b, 0, 0, 0)),
        scratch_shapes=[pltpu.VMEM((2, N_KV, n_pages * PAGE, DH), F32), pltpu.SemaphoreType.DMA((2,))],
    )
    return pl.pallas_call(
        functools.partial(_compress_sample_body, n_pages=n_pages),
        grid_spec=grid_spec,
        out_shape=jax.ShapeDtypeStruct((nb, N_KV, n, DH), F32),
        compiler_params=_params("arbitrary"),
        name="compress_sample",
    )(page_table, pool, *cw)


HEAD_ROWS = 8
TOKEN_ROWS = 16


def _cmp_select_sample_body(q_ref, kcc_ref, vcc_ref, oc_ref, idx_ref, *, past_len, n_blk):
    bb = q_ref.shape[0]
    n_cmp = kcc_ref.shape[2]
    rows = GQA * HEAD_ROWS
    q_pos = past_len + lax.broadcasted_iota(I32, (rows, 1), 0) % HEAD_ROWS
    cmp_end = lax.broadcasted_iota(I32, (1, n_cmp), 1) * STRIDE_CMP + (BLOCK_CMP - 1)
    cmask = cmp_end <= q_pos
    jobs = [(bi, k) for bi in range(bb) for k in range(N_KV)]
    scores = [jnp.where(cmask, _dot_nt(q_ref[bi, k], kcc_ref[bi, k].astype(BF16)), NEG) for bi, k in jobs]
    p_sums = []
    for s, (bi, k) in zip(scores, jobs):
        e = jnp.where(cmask, jnp.exp2(s - jnp.max(s, axis=-1, keepdims=True)), 0.0)
        p = e / jnp.maximum(jnp.sum(e, axis=-1, keepdims=True), 1e-30)
        oc_ref[bi, k] = _dot(p.astype(BF16), vcc_ref[bi, k].astype(BF16))
        p_sum = p[0:HEAD_ROWS]
        for g in range(1, GQA):
            p_sum = p_sum + p[g * HEAD_ROWS:(g + 1) * HEAD_ROWS]
        p_sums.append(p_sum)
    n_rows = bb * N_KV * HEAD_ROWS
    imp = _importance(jnp.concatenate(p_sums, axis=0), _overlap_matrix(n_cmp, n_blk))
    q_pos_r = past_len + lax.broadcasted_iota(I32, (n_rows, 1), 0) % HEAD_ROWS
    blk = lax.broadcasted_iota(I32, (1, n_blk), 1)
    cur = q_pos_r // SEL_BLOCK
    forced = (blk == 0) | (blk == cur) | (blk == cur - 1)
    score = jnp.where(blk * SEL_BLOCK <= q_pos_r, jnp.where(forced, BIG, imp), -BIG)
    lane = lax.broadcasted_iota(I32, (n_rows, n_blk), 1).astype(F32)
    out_lane = lax.broadcasted_iota(I32, (n_rows, LANES), 1)
    picks = jnp.zeros((n_rows, LANES), F32)
    for n in range(TOP_N):
        best = jnp.max(score, axis=-1, keepdims=True)
        pick = jnp.min(jnp.where(score == best, lane, float(n_blk)), axis=-1, keepdims=True)
        picks = jnp.where(out_lane == n, pick, picks)
        score = jnp.where(lane == pick, -3e38, score)
    picks = picks.astype(I32)
    for r in range(bb * N_KV):
        idx_ref[r // N_KV, r % N_KV] = picks[r * HEAD_ROWS:(r + 1) * HEAD_ROWS]


def _cmp_select_sample(q_hm, kcc, vcc, past_len, bb=8):
    nb = q_hm.shape[0]
    bb = min(bb, nb)
    assert nb % bb == 0
    n_cmp = kcc.shape[2]
    n_blk = -(-(past_len // SEL_BLOCK + 1) // LANES) * LANES
    rows = GQA * HEAD_ROWS
    spec = lambda r, w: pl.BlockSpec((bb, N_KV, r, w), lambda b: (b, 0, 0, 0))
    return pl.pallas_call(
        functools.partial(_cmp_select_sample_body, past_len=past_len, n_blk=n_blk),
        grid=(nb // bb,),
        in_specs=[spec(rows, DH), spec(n_cmp, DH), spec(n_cmp, DH)],
        out_specs=[spec(rows, DH), spec(HEAD_ROWS, LANES)],
        out_shape=[jax.ShapeDtypeStruct((nb, N_KV, rows, DH), F32),
                   jax.ShapeDtypeStruct((nb, N_KV, HEAD_ROWS, LANES), I32)],
        compiler_params=_params("arbitrary"),
        name="cmp_select_sample",
    )(q_hm, kcc, vcc)


def _attn_sample_body(pt_ref, idx_s_ref, q_ref, idx_v_ref, ksn_ref, vsn_ref, kwn_ref, vwn_ref, kwin_ref, vwin_ref,
                      oc_ref, gates_ref, kpool_ref, vpool_ref, o_ref, kbuf, vbuf, sem, *, dec_seq, n_pool_blk):
    step = pl.program_id(0)
    slot = step % 2
    bb = q_ref.shape[0]
    n_kt = N_KV * dec_seq
    assert PAGE == 2 * SEL_BLOCK

    def gather(st, sl):
        for row in range(bb * n_kt):
            bg = st * bb + row // n_kt
            kt = row % n_kt
            k = kt // dec_seq
            for n in range(TOP_N):
                blk = idx_s_ref[bg, kt * TOP_N + n]
                src = jnp.where(blk < n_pool_blk, blk, 0)
                page = pt_ref[bg, lax.shift_right_logical(src, 1)]
                rows = pl.ds(pl.multiple_of((src & 1) * SEL_BLOCK, SEL_BLOCK), SEL_BLOCK)
                dst = pl.ds(n * SEL_BLOCK, SEL_BLOCK)
                pltpu.make_async_copy(kpool_ref.at[page, rows, k], kbuf.at[sl, row, dst], sem.at[sl, 0]).start()
                pltpu.make_async_copy(vpool_ref.at[page, rows, k], vbuf.at[sl, row, dst], sem.at[sl, 1]).start()

    @pl.when(step == 0)
    def _():
        gather(0, 0)

    @pl.when(step + 1 < pl.num_programs(0))
    def _():
        gather(step + 1, 1 - slot)

    pltpu.make_async_copy(kbuf.at[slot], kbuf.at[slot], sem.at[slot, 0]).wait()
    pltpu.make_async_copy(vbuf.at[slot], vbuf.at[slot], sem.at[slot, 1]).wait()

    n_keys = TOP_N * SEL_BLOCK
    rows = dec_seq * TOKEN_ROWS
    key_slot = lax.broadcasted_iota(I32, (LANES, n_keys), 1) // SEL_BLOCK
    expand = (lax.broadcasted_iota(I32, (LANES, n_keys), 0) == key_slot).astype(BF16)
    new_col = lax.broadcasted_iota(I32, (1, TOKEN_ROWS), 1)
    t_row = lax.broadcasted_iota(I32, (rows, 1), 0) // TOKEN_ROWS
    n_win = kwin_ref.shape[1]
    win_old_vis = lax.broadcasted_iota(I32, (1, n_win), 1) > t_row
    win_new_vis = new_col <= t_row
    heads = [(bi, k) for bi in range(bb) for k in range(N_KV)]
    jobs = []
    for bi, k in heads:
        qk = q_ref[bi, k]
        pool_ok = _dot((idx_v_ref[bi, k] < n_pool_blk).astype(BF16), expand)
        jobs.append((jnp.where(win_old_vis, _dot_nt(qk, kwin_ref[bi, :, k, :].astype(BF16)), NEG),
                     jnp.where(win_new_vis, _dot_nt(qk, kwn_ref[bi, k]), NEG),
                     lambda bi=bi, k=k: vwin_ref[bi, :, k, :].astype(BF16), vwn_ref[bi, k]))
        for t in range(dec_seq):
            qt = qk[t * TOKEN_ROWS:(t + 1) * TOKEN_ROWS]
            kt = bi * n_kt + k * dec_seq + t
            jobs.append((jnp.where(pool_ok[t:t + 1, :] > 0.5, _dot_nt(qt, kbuf[slot, kt].astype(BF16)), NEG),
                         jnp.where(new_col <= t, _dot_nt(qt, ksn_ref[bi, k]), NEG),
                         lambda kt=kt: vbuf[slot, kt].astype(BF16), vsn_ref[bi, k]))
    probs = []
    for s_a, s_b, _, _ in jobs:
        m = jnp.maximum(jnp.max(s_a, axis=-1, keepdims=True), jnp.max(s_b, axis=-1, keepdims=True))
        p_a, p_b = jnp.exp2(s_a - m), jnp.exp2(s_b - m)
        probs.append((p_a.astype(BF16), p_b.astype(BF16),
                      jnp.sum(p_a, axis=-1, keepdims=True) + jnp.sum(p_b, axis=-1, keepdims=True)))
    outs = [(_dot(p_a, v_a()) + _dot(p_b, v_b)) / l for (p_a, p_b, l), (_, _, v_a, v_b) in zip(probs, jobs)]
    for h, (bi, k) in enumerate(heads):
        o_win = outs[h * (dec_seq + 1)]
        gates = gates_ref[bi, k]
        for t in range(dec_seq):
            ts = slice(t * TOKEN_ROWS, (t + 1) * TOKEN_ROWS)
            gt = gates[ts]
            o_ref[bi, k, ts, :] = (gt[:, 0:1] * oc_ref[bi, k, ts, :] + gt[:, 1:2] * outs[h * (dec_seq + 1) + 1 + t]
                                   + gt[:, 2:3] * o_win[ts])


def _attn_sample(page_table, idx_flat, q_tm, idx_pad, ksn, vsn, kwn, vwn, k_win, v_win, oc_tm, gates_tm,
                 k_pool, v_pool, dec_seq, past_len):
    nb = q_tm.shape[0]
    bb = 1
    rows = dec_seq * TOKEN_ROWS
    n_win = k_win.shape[1]
    spec = lambda r, w: pl.BlockSpec((bb, N_KV, r, w), lambda b, *_: (b, 0, 0, 0))
    win = pl.BlockSpec((bb, n_win, N_KV, DH), lambda b, *_: (b, 0, 0, 0))
    any_spec = pl.BlockSpec(memory_space=pl.ANY)
    grid_spec = pltpu.PrefetchScalarGridSpec(
        num_scalar_prefetch=2,
        grid=(nb // bb,),
        in_specs=[spec(rows, DH), spec(TOKEN_ROWS, LANES), spec(TOKEN_ROWS, DH), spec(TOKEN_ROWS, DH),
                  spec(TOKEN_ROWS, DH), spec(TOKEN_ROWS, DH), win, win, spec(rows, DH), spec(rows, LANES),
                  any_spec, any_spec],
        out_specs=spec(rows, DH),
        scratch_shapes=[pltpu.VMEM((2, bb * N_KV * dec_seq, TOP_N * SEL_BLOCK, DH), F32),
                        pltpu.VMEM((2, bb * N_KV * dec_seq, TOP_N * SEL_BLOCK, DH), F32),
                        pltpu.SemaphoreType.DMA((2, 2))],
    )
    return pl.pallas_call(
        functools.partial(_attn_sample_body, dec_seq=dec_seq, n_pool_blk=past_len // SEL_BLOCK),
        grid_spec=grid_spec,
        out_shape=jax.ShapeDtypeStruct((nb, N_KV, rows, DH), F32),
        compiler_params=_params("arbitrary"),
        name="attn_sample",
    )(page_table, idx_flat, q_tm, idx_pad, ksn, vsn, kwn, vwn, k_win, v_win, oc_tm, gates_tm, k_pool, v_pool)


def _mix_out_sample_body(oa_ref, u_ref, st_ref, x_ref, cw_ref, cb_ref, lng_ref, lnb_ref, ga_ref, gc_ref,
                         woa_ref, woc_ref, gm_ref, wmq_ref, o_ref, qm_ref, conv_scr, *, nb, dec_seq):
    hist = CONV_W - 1

    def ext(j):
        if j < hist:
            return st_ref[j * nb:(j + 1) * nb, :]
        return u_ref[(j - hist) * nb:(j - hist + 1) * nb, :]

    for t in range(dec_seq):
        acc = jnp.zeros((nb, CONV_CH), F32) + cb_ref[...]
        for k in range(CONV_W):
            acc = acc + cw_ref[k:k + 1, :] * ext(t + k)
        conv_scr[t * nb:(t + 1) * nb, :] = acc
    conv_n = _ln_silu_rms(conv_scr[...], lng_ref, lnb_ref, gc_ref)
    attn_n = (_rms(oa_ref[...]) * ga_ref[...]).astype(BF16)
    x1 = x_ref[...] + _dot(attn_n, woa_ref[...]) + _dot(conv_n, woc_ref[...])
    o_ref[...] = x1
    qm_ref[...] = _dot((_rms(x1) * gm_ref[...]).astype(BF16), wmq_ref[...])


def _mix_out_sample(o_attn, u, conv_state, x, mw, g_mem, w_mq, nb, dec_seq):
    m = x.shape[0]
    args = (o_attn, u, conv_state, x) + tuple(mw) + (g_mem, w_mq)
    return pl.pallas_call(
        functools.partial(_mix_out_sample_body, nb=nb, dec_seq=dec_seq),
        grid=(1,),
        in_specs=[_const_spec(a.shape) for a in args],
        out_specs=[_const_spec((m, D_MODEL)), _const_spec((m, MEM_W))],
        out_shape=[jax.ShapeDtypeStruct((m, D_MODEL), F32), jax.ShapeDtypeStruct((m, MEM_W), F32)],
        scratch_shapes=[pltpu.VMEM((m, CONV_CH), F32)],
        compiler_params=_params("arbitrary"),
        name="mix_out_sample",
    )(*args)


def _mem_attn_sample_body(q_ref, mk_ref, mv_ref, o_ref):
    jobs = [(bi, h) for bi in range(q_ref.shape[0]) for h in range(MEM_HEADS)]
    scores = [_dot_nt(q_ref[bi, :, h * MEM_DH:(h + 1) * MEM_DH], mk_ref[bi, :, h, :].astype(BF16)) * MEM_SCALE
              for bi, h in jobs]
    probs = [jnp.exp(s - jnp.max(s, axis=-1, keepdims=True)) for s in scores]
    outs = [_dot(p.astype(BF16), mv_ref[bi, :, h, :].astype(BF16)) / jnp.sum(p, axis=-1, keepdims=True)
            for p, (bi, h) in zip(probs, jobs)]
    for bi in range(q_ref.shape[0]):
        o_ref[bi] = jnp.concatenate(outs[bi * MEM_HEADS:(bi + 1) * MEM_HEADS], axis=1).astype(BF16)


def _mem_attn_sample(q_pad, mk, mv, bb=4):
    nb, rows, _ = q_pad.shape
    bb = bb if nb % bb == 0 else 1
    n_mem = mk.shape[1]
    q_spec = pl.BlockSpec((bb, rows, MEM_W), lambda b: (b, 0, 0))
    mem = pl.BlockSpec((bb, n_mem, MEM_HEADS, MEM_DH), lambda b: (b, 0, 0, 0))
    return pl.pallas_call(
        _mem_attn_sample_body,
        grid=(nb // bb,),
        in_specs=[q_spec, mem, mem],
        out_specs=q_spec,
        out_shape=jax.ShapeDtypeStruct((nb, rows, MEM_W), BF16),
        compiler_params=_params("arbitrary"),
        name="mem_attn_sample",
    )(q_pad, mk, mv)


def _ffn_sample_body(x_ref, a_ref, wo_ref, g_ref, wg_ref, wu_ref, cw_ref, cb_ref, wd_ref, gf_ref, st_ref,
                     o_ref, sto_ref, x2_scr, h_scr, acc_scr, gate_scr, conv_scr, *, nb, dec_seq):
    f = pl.program_id(0)
    hist = FFN_CONV_W - 1

    @pl.when(f == 0)
    def _():
        x2 = x_ref[...] + _dot(a_ref[...], wo_ref[...])
        x2_scr[...] = x2
        h_scr[...] = (_rms(x2) * g_ref[...]).astype(BF16)
        acc_scr[...] = jnp.zeros(acc_scr.shape, F32)

    hb = h_scr[...]
    gate_scr[...] = _dot(hb, wg_ref[...])

    def ext(j):
        if j < hist:
            return st_ref[j * nb:(j + 1) * nb, :]
        return gate_scr[(j - hist) * nb:(j - hist + 1) * nb, :]

    for t in range(dec_seq):
        acc = cb_ref[...] + cw_ref[0:1, :] * ext(t)
        for k in range(1, FFN_CONV_W):
            acc = acc + cw_ref[k:k + 1, :] * ext(t + k)
        conv_scr[t * nb:(t + 1) * nb, :] = acc
    for j in range(hist):
        sto_ref[j * nb:(j + 1) * nb, :] = ext(dec_seq + j)
    conv = conv_scr[...]
    a = conv * jax.nn.sigmoid(conv) * _dot(hb, wu_ref[...])
    acc_scr[...] += _dot(a.astype(BF16), wd_ref[...])
    _ffn_tail(f, pl.num_programs(0), x2_scr, acc_scr, gf_ref, o_ref)


def _ffn_sample(x1, a, w_mo, fw, ffn_state, nb, dec_seq, tf=512):
    g, wg, wu, cw, cb, wd, gf = fw
    m = x1.shape[0]
    nf = D_FF // tf
    hist = FFN_CONV_W - 1
    col = lambda r: pl.BlockSpec((r, tf), lambda f: (0, f))
    full = lambda shape: pl.BlockSpec(shape, lambda f: (0,) * len(shape))
    return pl.pallas_call(
        functools.partial(_ffn_sample_body, nb=nb, dec_seq=dec_seq),
        grid=(nf,),
        in_specs=[full(x1.shape), full(a.shape), full(w_mo.shape), full(g.shape), col(D_MODEL), col(D_MODEL),
                  col(cw.shape[0]), col(1), pl.BlockSpec((tf, D_MODEL), lambda f: (f, 0)), full(gf.shape),
                  col(nb * hist)],
        out_specs=[full((m, D_MODEL)), col(nb * hist)],
        out_shape=[jax.ShapeDtypeStruct((m, D_MODEL), F32), jax.ShapeDtypeStruct((nb * hist, D_FF), F32)],
        scratch_shapes=[pltpu.VMEM((m, D_MODEL), F32), pltpu.VMEM((m, D_MODEL), BF16), pltpu.VMEM((m, D_MODEL), F32),
                        pltpu.VMEM((m, tf), F32), pltpu.VMEM((m, tf), F32)],
        compiler_params=_params("arbitrary"),
        name="ffn_sample",
    )(x1, a, w_mo, g, wg, wu, cw, cb, wd, gf, ffn_state)


def _prepare_weights(norm_mix_g, w_in, b_gate, cmp_k, cmp_v, conv_w, conv_b, conv_ln_g, conv_ln_b,
                     grp_norm_attn_g, grp_norm_conv_g, w_out, norm_mem_g, mem_norm_g, w_mq, w_mk, w_mv, w_mo,
                     norm_ffn_g, w_ffn_gate, w_ffn_up, ffn_conv_w, ffn_conv_b, w_ffn_down, norm_final_g):
    vec = lambda v: v.reshape(1, -1)
    kv_end = ATTN_W + 6 * KV_W
    n_gate_cols = N_KV * GQA * N_GATE
    per_kv = GQA * N_GATE
    wg = w_in[:, kv_end:kv_end + n_gate_cols].reshape(D_MODEL, N_KV, per_kv)
    wg = jnp.pad(wg, ((0, 0), (0, 0), (0, LANES - per_kv))).reshape(D_MODEL, N_KV * LANES)
    bg = jnp.pad(b_gate.reshape(N_KV, per_kv), ((0, 0), (0, LANES - per_kv))).reshape(1, N_KV * LANES)
    return dict(
        in_proj=(vec(norm_mix_g), w_in[:, :ATTN_W].astype(BF16), w_in[:, ATTN_W:kv_end].astype(BF16),
                 wg.astype(BF16), bg, w_in[:, kv_end + n_gate_cols:].astype(BF16)),
        cmp_k=_compress_weights(*cmp_k),
        cmp_v=_compress_weights(*cmp_v),
        mix=(jnp.pad(conv_w, ((0, 32 - CONV_W), (0, 0))), vec(conv_b), vec(conv_ln_g), vec(conv_ln_b),
             vec(grp_norm_attn_g), vec(grp_norm_conv_g), w_out[:ATTN_W].astype(BF16), w_out[ATTN_W:].astype(BF16)),
        mem_kv=(vec(mem_norm_g), w_mk.astype(BF16), w_mv.astype(BF16)),
        mem=(vec(norm_mem_g), w_mq.astype(BF16), w_mo.astype(BF16)),
        ffn=(vec(norm_ffn_g), w_ffn_gate.astype(BF16), w_ffn_up.astype(BF16),
             jnp.pad(ffn_conv_w, ((0, 8 - FFN_CONV_W), (0, 0))), vec(ffn_conv_b), w_ffn_down.astype(BF16),
             vec(norm_final_g)),
    )


def _prompt_forward(x_prompt, mem_prompt, w):
    batch, t, _ = x_prompt.shape
    x = x_prompt.reshape(batch * t, D_MODEL)
    (q, kc, vc, ks, vs, kw, vw, ksb, vsb, kwb, vwb, gates, u) = _in_proj(x, *w["in_proj"], tm=512)
    kcc = _compress_prompt(kc, w["cmp_k"], batch)
    vcc = _compress_prompt(vc, w["cmp_v"], batch)
    gates_t = gates.reshape(batch * t, N_KV, LANES)[:, :, :GATE_ROWS].transpose(1, 2, 0)
    o_attn = _nsa_prompt(q.T, kcc, vcc.transpose(0, 1, 3, 2), ksb, _values_with_ones(vsb), kwb, _values_with_ones(vwb),
                         gates_t.reshape(N_KV * GATE_ROWS, batch * t), batch)
    x1 = _mix_out_prompt(o_attn, u, x, w["mix"], batch)
    n_mem = mem_prompt.shape[1]
    mk, mv = _mem_kv(mem_prompt.reshape(batch * n_mem, D_MODEL), *w["mem_kv"])
    g_mem, w_mq, w_mo = w["mem"]
    x2 = _mem_attn_prompt(x1, g_mem, w_mq, mk.reshape(batch, n_mem, MEM_W), mv.reshape(batch, n_mem, MEM_W),
                          w_mo, batch)
    y, ffn_tail = _ffn_prompt(x2, *w["ffn"], batch=batch)
    kv5 = lambda a: a.reshape(1, batch, t, N_KV, DH)
    win = lambda a: a.reshape(batch, t, N_KV, DH)[None, :, t - min(WINDOW, t):]
    tiles = ffn_tail.shape[0] // batch
    new_ffn = ffn_tail.reshape(batch, tiles, ffn_tail.shape[1], D_FF)[:, -1, -(FFN_CONV_W - 1):]
    new_conv = u.reshape(batch, t, CONV_CH)[:, t - (CONV_W - 1):]
    mem5 = lambda a: a.reshape(1, batch, n_mem, MEM_HEADS, MEM_DH)
    return (y.reshape(batch, t, D_MODEL), kv5(kc), kv5(vc), kv5(ks), kv5(vs), win(kw), win(vw),
            new_conv[None], new_ffn[None], mem5(mk), mem5(mv))


def _pad_axis(a, axis, size):
    pads = [(0, 0)] * a.ndim
    pads[axis] = (0, size - a.shape[axis])
    return jnp.pad(a, pads)


def _sample_forward(x_sample, pools, k_win, v_win, conv_state, ffn_state, mem_k, mem_v, page_table, w):
    nb, dec_seq, _ = x_sample.shape
    m = nb * dec_seq
    past_len = page_table.shape[1] * PAGE
    assert dec_seq <= HEAD_ROWS and k_win.shape[1] == WINDOW
    x = x_sample.reshape(m, D_MODEL)
    (q, kc, vc, ks, vs, kw, vw, ksb, vsb, kwb, vwb, gates, u) = _in_proj(x, *w["in_proj"], tm=m)
    pool_kc, pool_vc, pool_ks, pool_vs = pools
    kcc = _compress_sample(pool_kc, page_table, w["cmp_k"])
    vcc = _compress_sample(pool_vc, page_table, w["cmp_v"])

    q5 = q.reshape(nb, dec_seq, N_KV, GQA, DH)
    q_hm = _pad_axis(q5.transpose(0, 2, 3, 1, 4), 3, HEAD_ROWS).reshape(nb, N_KV, GQA * HEAD_ROWS, DH)
    q_tm = _pad_axis(q5.transpose(0, 2, 1, 3, 4), 3, TOKEN_ROWS).reshape(nb, N_KV, dec_seq * TOKEN_ROWS, DH)
    oc_hm, idx = _cmp_select_sample(q_hm, kcc, vcc, past_len)
    oc_tm = oc_hm.reshape(nb, N_KV, GQA, HEAD_ROWS, DH)[:, :, :, :dec_seq].transpose(0, 1, 3, 2, 4)
    oc_tm = _pad_axis(oc_tm, 3, TOKEN_ROWS).reshape(nb, N_KV, dec_seq * TOKEN_ROWS, DH)
    idx_flat = idx[:, :, :dec_seq, :TOP_N].reshape(nb, N_KV * dec_seq * TOP_N)
    idx_pad = _pad_axis(idx, 2, TOKEN_ROWS)
    new_rows = lambda a: _pad_axis(a.reshape(nb, dec_seq, N_KV, DH).transpose(0, 2, 1, 3), 2, TOKEN_ROWS)
    gates_tm = gates.reshape(nb, dec_seq, N_KV, LANES)[..., :GQA * N_GATE].reshape(nb, dec_seq, N_KV, GQA, N_GATE)
    gates_tm = _pad_axis(_pad_axis(gates_tm.transpose(0, 2, 1, 3, 4), 3, TOKEN_ROWS), 4, LANES)
    gates_tm = gates_tm.reshape(nb, N_KV, dec_seq * TOKEN_ROWS, LANES)
    o_tm = _attn_sample(page_table, idx_flat, q_tm, idx_pad, new_rows(ksb), new_rows(vsb), new_rows(kwb),
                        new_rows(vwb), k_win, v_win,
                        oc_tm, gates_tm, pool_ks, pool_vs, dec_seq, past_len)
    o_attn = o_tm.reshape(nb, N_KV, dec_seq, TOKEN_ROWS, DH)[:, :, :, :GQA].transpose(2, 0, 1, 3, 4).reshape(m, ATTN_W)
    step_major = lambda a: a.reshape(nb, -1, a.shape[-1]).transpose(1, 0, 2).reshape(-1, a.shape[-1])
    batch_major = lambda a: a.reshape(-1, nb, a.shape[-1]).transpose(1, 0, 2)

    g_mem, w_mq, w_mo = w["mem"]
    x1, qm = _mix_out_sample(o_attn, step_major(u), step_major(conv_state), step_major(x), w["mix"], g_mem, w_mq,
                             nb, dec_seq)
    n_mem = mem_k.shape[1]
    q_pad = _pad_axis(batch_major(qm), 1, TOKEN_ROWS).astype(BF16)
    a = _mem_attn_sample(q_pad, mem_k, mem_v)
    y, new_ffn = _ffn_sample(x1, step_major(a[:, :dec_seq]), w_mo, w["ffn"], step_major(ffn_state), nb, dec_seq)

    kv5 = lambda a: a.reshape(1, nb, dec_seq, N_KV, DH)
    shift = lambda buf, new: jnp.concatenate([buf[:, dec_seq:], new.reshape((nb, dec_seq) + buf.shape[2:])], axis=1)[None]
    return (batch_major(y), kv5(kc), kv5(vc), kv5(ks), kv5(vs), shift(k_win, kw), shift(v_win, vw),
            shift(conv_state, u), batch_major(new_ffn)[None])


def kernel(x_prompt, x_sample, cache_k_cmp, cache_v_cmp, cache_k_sel, cache_v_sel, cache_k_win, cache_v_win,
           state_conv, state_ffn_conv, cache_mem_k, cache_mem_v, page_table, mem_prompt,
           norm_mix_g, w_in, b_gate, cmp_k_pe, cmp_k_w1, cmp_k_b1, cmp_k_w2, cmp_v_pe, cmp_v_w1, cmp_v_b1, cmp_v_w2,
           conv_w, conv_b, conv_ln_g, conv_ln_b, grp_norm_attn_g, grp_norm_conv_g, w_out,
           norm_mem_g, mem_norm_g, w_mq, w_mk, w_mv, w_mo,
           norm_ffn_g, w_ffn_gate, w_ffn_up, ffn_conv_w, ffn_conv_b, w_ffn_down, norm_final_g):
    assert w_in.shape[0] == 1, "single-layer step"
    w = _prepare_weights(norm_mix_g[0], w_in[0], b_gate[0],
                         (cmp_k_pe[0], cmp_k_w1[0], cmp_k_b1[0], cmp_k_w2[0]),
                         (cmp_v_pe[0], cmp_v_w1[0], cmp_v_b1[0], cmp_v_w2[0]),
                         conv_w[0], conv_b[0], conv_ln_g[0], conv_ln_b[0], grp_norm_attn_g[0], grp_norm_conv_g[0],
                         w_out[0], norm_mem_g[0], mem_norm_g[0], w_mq[0], w_mk[0], w_mv[0], w_mo[0],
                         norm_ffn_g[0], w_ffn_gate[0], w_ffn_up[0], ffn_conv_w[0], ffn_conv_b[0], w_ffn_down[0],
                         norm_final_g)
    p = _prompt_forward(x_prompt, mem_prompt, w)
    s = _sample_forward(x_sample, (cache_k_cmp[0], cache_v_cmp[0], cache_k_sel[0], cache_v_sel[0]),
                        cache_k_win[0], cache_v_win[0], state_conv[0], state_ffn_conv[0],
                        cache_mem_k[0], cache_mem_v[0], page_table, w)
    return (p[0], s[0]) + p[1:] + s[1:]
```

```python
import functools

import jax
import jax.numpy as jnp
from jax import lax
from jax.experimental import pallas as pl
from jax.experimental.pallas import tpu as pltpu

F32 = jnp.float32
BF16 = jnp.bfloat16
I32 = jnp.int32

D_MODEL = 2048
N_KV = 2
GQA = 4
DH = 128
ATTN_W = N_KV * GQA * DH
KV_W = N_KV * DH
N_GATE = 3
BLOCK_CMP = 32
STRIDE_CMP = 16
CMP_HID = 256
SEL_BLOCK = 64
TOP_N = 16
WINDOW = 512
CONV_CH = D_MODEL - ATTN_W
CONV_W = 31
D_FF = 5632
FFN_CONV_W = 3
MEM_HEADS = 4
MEM_DH = 128
MEM_W = MEM_HEADS * MEM_DH
ATTN_SCALE = DH ** -0.5
Q_PRESCALE = ATTN_SCALE * 1.4426950408889634
MEM_SCALE = MEM_DH ** -0.5
EPS = 1e-6
NEG = -1e30
BIG = 1e30
LANES = 128
VMEM_LIMIT = 56 * 1024 * 1024


def _dot(a, b):
    return jnp.dot(a, b, preferred_element_type=F32)


def _dot_nt(a, b):
    return lax.dot_general(a, b, (((1,), (1,)), ((), ())), preferred_element_type=F32)


def _rms(x):
    return x * lax.rsqrt(jnp.mean(x * x, axis=-1, keepdims=True) + EPS)


def _const_spec(shape):
    return pl.BlockSpec(shape, lambda *_: (0,) * len(shape), pipeline_mode=pl.Buffered(1))


def _params(*sem):
    return pltpu.CompilerParams(dimension_semantics=sem, vmem_limit_bytes=VMEM_LIMIT)


def _in_proj_body(x_ref, g_ref, wq_ref, wkv_ref, wg_ref, bg_ref, wglu_ref,
                  q_ref, kc_ref, vc_ref, ks_ref, vs_ref, kw_ref, vw_ref,
                  ksb_ref, vsb_ref, kwb_ref, vwb_ref, gates_ref, u_ref):
    hb = (_rms(x_ref[...]) * g_ref[...]).astype(BF16)
    half = ATTN_W // 2
    for c in range(2):
        q_ref[:, c * half:(c + 1) * half] = (_dot(hb, wq_ref[:, c * half:(c + 1) * half]) * Q_PRESCALE).astype(BF16)
    f32_outs = (kc_ref, vc_ref, ks_ref, vs_ref, kw_ref, vw_ref)
    bf_outs = (None, None, ksb_ref, vsb_ref, kwb_ref, vwb_ref)
    for c in range(6):
        r = _dot(hb, wkv_ref[:, c * KV_W:(c + 1) * KV_W])
        for k in range(N_KV):
            f32_outs[c][:, k, :] = r[:, k * DH:(k + 1) * DH]
        if bf_outs[c] is not None:
            bf_outs[c][...] = r.astype(BF16)
    gates_ref[...] = jax.nn.sigmoid(_dot(hb, wg_ref[...]) + bg_ref[...])
    cw = 256
    for c in range(CONV_CH // cw):
        a = _dot(hb, wglu_ref[:, c * cw:(c + 1) * cw])
        gt = _dot(hb, wglu_ref[:, CONV_CH + c * cw:CONV_CH + (c + 1) * cw])
        u_ref[:, c * cw:(c + 1) * cw] = a * jax.nn.sigmoid(gt)


def _in_proj(x, g, wq, wkv, wg, bg, wglu, tm):
    m = x.shape[0]
    row = lambda w: pl.BlockSpec((tm, w), lambda i: (i, 0))
    out_shape = ([jax.ShapeDtypeStruct((m, ATTN_W), BF16)]
                 + [jax.ShapeDtypeStruct((m, N_KV, DH), F32)] * 6
                 + [jax.ShapeDtypeStruct((m, KV_W), BF16)] * 4
                 + [jax.ShapeDtypeStruct((m, N_KV * LANES), F32),
                    jax.ShapeDtypeStruct((m, CONV_CH), F32)])
    state = pl.BlockSpec((tm, N_KV, DH), lambda i: (i, 0, 0))
    out_specs = ([row(ATTN_W)] + [state] * 6 + [row(KV_W)] * 4 + [row(N_KV * LANES), row(CONV_CH)])
    return pl.pallas_call(
        _in_proj_body,
        grid=(m // tm,),
        in_specs=[row(D_MODEL), _const_spec(g.shape), _const_spec(wq.shape), _const_spec(wkv.shape),
                  _const_spec(wg.shape), _const_spec(bg.shape), _const_spec(wglu.shape)],
        out_specs=out_specs,
        out_shape=out_shape,
        compiler_params=_params("arbitrary"),
        name="in_proj",
    )(x, g, wq, wkv, wg, bg, wglu)


def _compress_rows(get_lanes, n, pe_ref, w1l_ref, w1t_ref, b1_ref, w2_ref):
    xk = jnp.concatenate([get_lanes(l) for l in range(STRIDE_CMP)], axis=1)
    lead = _dot((xk + pe_ref[0:1, :]).astype(BF16), w1l_ref[...])
    trail = _dot((xk + pe_ref[1:2, :]).astype(BF16), w1t_ref[...])
    trail_next = pltpu.roll(trail, n - 1, axis=0)
    hid = jax.nn.gelu(lead + trail_next + b1_ref[...])
    out = _dot(hid.astype(BF16), w2_ref[...])
    rows = lax.broadcasted_iota(I32, (n, 1), 0)
    return jnp.where(rows < n - 1, out, 0.0)


def _compress_prompt_body(x_ref, pe_ref, w1l_ref, w1t_ref, b1_ref, w2_ref, o_ref):
    n = x_ref.shape[0] // STRIDE_CMP
    for k in range(N_KV):
        get = lambda l, k=k: x_ref[pl.ds(l, n, stride=STRIDE_CMP), k, :]
        o_ref[0, k] = _compress_rows(get, n, pe_ref, w1l_ref, w1t_ref, b1_ref, w2_ref)


def _compress_weights(pe, w1, b1, w2):
    half = STRIDE_CMP * DH
    pe2 = pe.reshape(2, half)
    return pe2, w1[:half].astype(BF16), w1[half:].astype(BF16), b1.reshape(1, CMP_HID), w2.astype(BF16)


def _compress_prompt(rows, cw, batch):
    t = rows.shape[0] // batch
    n = t // STRIDE_CMP
    return pl.pallas_call(
        _compress_prompt_body,
        grid=(batch,),
        in_specs=[pl.BlockSpec((t, N_KV, DH), lambda b: (b, 0, 0))] + [_const_spec(w.shape) for w in cw],
        out_specs=pl.BlockSpec((1, N_KV, n, DH), lambda b: (b, 0, 0, 0)),
        out_shape=jax.ShapeDtypeStruct((batch, N_KV, n, DH), F32),
        compiler_params=_params("arbitrary"),
        name="compress_prompt",
    )(rows, *cw)


def _overlap_matrix(n_cmp, n_blk):
    i = lax.broadcasted_iota(I32, (n_cmp, n_blk), 0) * STRIDE_CMP
    j = lax.broadcasted_iota(I32, (n_cmp, n_blk), 1) * SEL_BLOCK
    ov = jnp.maximum(jnp.minimum(i + BLOCK_CMP, j + SEL_BLOCK) - jnp.maximum(i, j), 0)
    return (ov.astype(F32) * (1.0 / BLOCK_CMP)).astype(BF16)


def _importance(p_sum, ov):
    hi = p_sum.astype(BF16)
    r1 = p_sum - hi.astype(F32)
    mid = r1.astype(BF16)
    lo = (r1 - mid.astype(F32)).astype(BF16)
    return _dot(hi, ov) + _dot(mid, ov) + _dot(lo, ov)


def _softmax_av(s, v):
    p = jnp.exp(s - jnp.max(s, axis=-1, keepdims=True))
    return _dot(p.astype(BF16), v) / jnp.sum(p, axis=-1, keepdims=True)


GATE_ROWS = 16
VROWS = DH + 16


def _nsa_prompt_body(qt_ref, kcc_ref, vcct_ref, ks_ref, vst_ref, kw_ref, vwt_ref, gt_ref, o_ref,
                     acc_scr, bias_scr, *, tq, tk, wq, n_sel):
    i = pl.program_id(2)
    t0 = i * tq
    nq = GQA * tq
    heads = lambda a: jnp.concatenate([a] * GQA, axis=1)
    q_pos = t0 + lax.broadcasted_iota(I32, (1, tq), 1)
    q_pos4 = heads(q_pos)
    qt = jnp.concatenate([qt_ref[g * DH:(g + 1) * DH, :] for g in range(GQA)], axis=1)

    n_cmp = kcc_ref.shape[2]
    cmp_end = lax.broadcasted_iota(I32, (n_cmp, 1), 0) * STRIDE_CMP + (BLOCK_CMP - 1)
    cvis = cmp_end <= q_pos4
    s = jnp.where(cvis, _dot(kcc_ref[0, 0].astype(BF16), qt), NEG)
    e = jnp.where(cvis, jnp.exp2(s - jnp.max(s, axis=0, keepdims=True)), 0.0)
    p = e / jnp.maximum(jnp.sum(e, axis=0, keepdims=True), 1e-30)
    o_cmp = _dot(vcct_ref[0, 0].astype(BF16), p.astype(BF16))
    p_sum = p[:, 0:tq]
    for g in range(1, GQA):
        p_sum = p_sum + p[:, g * tq:(g + 1) * tq]

    ov_i = lax.broadcasted_iota(I32, (n_sel, n_cmp), 1) * STRIDE_CMP
    ov_j = lax.broadcasted_iota(I32, (n_sel, n_cmp), 0) * SEL_BLOCK
    ov = jnp.maximum(jnp.minimum(ov_i + BLOCK_CMP, ov_j + SEL_BLOCK) - jnp.maximum(ov_i, ov_j), 0)
    ov = (ov.astype(F32) * (1.0 / BLOCK_CMP)).astype(BF16)
    hi = p_sum.astype(BF16)
    r1 = p_sum - hi.astype(F32)
    mid = r1.astype(BF16)
    lo = (r1 - mid.astype(F32)).astype(BF16)
    imp = _dot(ov, hi) + _dot(ov, mid) + _dot(ov, lo)
    blk = lax.broadcasted_iota(I32, (n_sel, tq), 0)
    cur = q_pos // SEL_BLOCK
    forced = (blk == 0) | (blk == cur) | (blk == cur - 1)
    s_t = jnp.where(blk * SEL_BLOCK <= q_pos, jnp.where(forced, BIG, imp), -BIG)
    rank = jnp.zeros((n_sel, tq), F32)
    for ib in range(n_sel):
        row = s_t[ib:ib + 1, :]
        beats = (row > s_t) | ((row == s_t) & (blk > ib))
        rank = rank + jnp.where(beats, 1.0, 0.0)
    bias_scr[...] = jnp.where(rank < TOP_N, 0.0, NEG)

    acc_scr[...] = jnp.zeros(acc_scr.shape, F32)
    blocks_per_tile = tk // SEL_BLOCK
    last_tile = ks_ref.shape[0] // tk - 1

    def key_tile(j, state, m, causal):
        jd = jnp.minimum(j, last_tile)
        k0 = pl.multiple_of(jd * tk, tk)
        bias = jnp.concatenate(
            [jnp.broadcast_to(bias_scr[pl.ds(jd * blocks_per_tile + c, 1), :], (SEL_BLOCK, tq))
             for c in range(blocks_per_tile)], axis=0)
        if causal:
            bias = jnp.where(j * tk + lax.broadcasted_iota(I32, (tk, 1), 0) <= q_pos, bias, NEG)
        s = _dot(ks_ref[pl.ds(k0, tk), :], qt) + heads(bias)
        m_new = jnp.maximum(m, jnp.max(s, axis=0, keepdims=True))
        p = jnp.exp2(s - m_new).astype(BF16)
        acc_scr[state] = jnp.exp2(m - m_new) * acc_scr[state] + _dot(vst_ref[:, pl.ds(k0, tk)], p)
        return m_new

    def tile_pair(jp, c):
        return key_tile(2 * jp, 0, c[0], False), key_tile(2 * jp + 1, 1, c[1], False)

    n_pairs = t0 // (2 * tk)
    empty = jnp.full((1, nq), NEG, F32)
    c = lax.fori_loop(0, n_pairs, tile_pair, (empty, empty))
    m0 = key_tile(2 * n_pairs, 0, c[0], True)
    second_is_live = (t0 // tk) % 2 == 1
    m1 = lax.cond(second_is_live, lambda: key_tile(2 * n_pairs + 1, 1, c[1], True), lambda: c[1])
    m = jnp.maximum(m0, m1)
    merged = jnp.exp2(m0 - m) * acc_scr[0] + jnp.exp2(m1 - m) * acc_scr[1]
    o_sel = merged[0:DH] / merged[DH:DH + 1]

    span = WINDOW + wq
    n_sub = tq // wq
    o_sub = []
    for j in range(n_sub):
        sub = lambda a, j=j: jnp.concatenate([a[:, g * tq + j * wq:g * tq + (j + 1) * wq] for g in range(GQA)], axis=1)
        q_sub = sub(q_pos4)
        win0 = pl.multiple_of(jnp.maximum(t0 + j * wq - WINDOW, 0), wq)
        kw_pos = win0 + lax.broadcasted_iota(I32, (span, 1), 0)
        wvis = (kw_pos <= q_sub) & (q_sub - kw_pos < WINDOW)
        s = jnp.where(wvis, _dot(kw_ref[pl.ds(win0, span), :], sub(qt)), NEG)
        p = jnp.exp2(s - jnp.max(s, axis=0, keepdims=True)).astype(BF16)
        r = _dot(vwt_ref[:, pl.ds(win0, span)], p)
        o_sub.append(r[0:DH] / r[DH:DH + 1])
    o_win = jnp.concatenate([o_sub[j][:, g * wq:(g + 1) * wq] for g in range(GQA) for j in range(n_sub)], axis=1)

    gt = gt_ref[...]
    gate = lambda c: jnp.concatenate([gt[g * N_GATE + c:g * N_GATE + c + 1, :] for g in range(GQA)], axis=1)
    o = gate(0) * o_cmp + gate(1) * o_sel + gate(2) * o_win
    for g in range(GQA):
        o_ref[:, g * DH:(g + 1) * DH] = o[:, g * tq:(g + 1) * tq].T


def _values_with_ones(v):
    m = v.shape[0]
    vt = v.T.reshape(N_KV, DH, m)
    return jnp.concatenate([vt, jnp.ones((N_KV, VROWS - DH, m), v.dtype)], axis=1).reshape(N_KV * VROWS, m)


def _nsa_prompt(qt, kcc, vcct, ksb, vst, kwb, vwt, gates_t, batch, tq=512, tk=512, wq=128):
    m = qt.shape[1]
    t = m // batch
    tk = min(tk, t)
    nt = t // tq
    n_cmp = kcc.shape[2]
    n_sel = max(t // SEL_BLOCK, 8)
    assert t % tk == 0 and tk % tq == 0 and tq % wq == 0 and wq % LANES == 0 and t >= WINDOW + wq
    rows = lambda: pl.BlockSpec((t, DH), lambda b, k, i: (b, k))
    cols = lambda: pl.BlockSpec((VROWS, t), lambda b, k, i: (k, b))
    return pl.pallas_call(
        functools.partial(_nsa_prompt_body, tq=tq, tk=tk, wq=wq, n_sel=n_sel),
        grid=(batch, N_KV, nt),
        in_specs=[pl.BlockSpec((GQA * DH, tq), lambda b, k, i: (k, b * nt + i)),
                  pl.BlockSpec((1, 1, n_cmp, DH), lambda b, k, i: (b, k, 0, 0)),
                  pl.BlockSpec((1, 1, DH, n_cmp), lambda b, k, i: (b, k, 0, 0)),
                  rows(), cols(), rows(), cols(),
                  pl.BlockSpec((GATE_ROWS, tq), lambda b, k, i: (k, b * nt + i))],
        out_specs=pl.BlockSpec((tq, GQA * DH), lambda b, k, i: (b * nt + i, k)),
        out_shape=jax.ShapeDtypeStruct((m, ATTN_W), F32),
        scratch_shapes=[pltpu.VMEM((2, VROWS, GQA * tq), F32), pltpu.VMEM((n_sel, tq), F32)],
        compiler_params=_params("arbitrary", "arbitrary", "arbitrary"),
        name="nsa_prompt",
    )(qt, kcc, vcct, ksb, vst, kwb, vwt, gates_t)


def _ln_silu_rms(y, lng_ref, lnb_ref, gn_ref):
    mu = jnp.mean(y, axis=-1, keepdims=True)
    var = jnp.mean(jnp.square(y - mu), axis=-1, keepdims=True)
    y = (y - mu) * lax.rsqrt(var + EPS) * lng_ref[...] + lnb_ref[...]
    y = y * jax.nn.sigmoid(y)
    return (_rms(y) * gn_ref[...]).astype(BF16)


def _mix_out_prompt_body(oa_ref, u_ref, halo_ref, x_ref, cw_ref, cb_ref, lng_ref, lnb_ref, ga_ref, gc_ref,
                         woa_ref, woc_ref, o_ref, ext_scr, conv_scr, *, tm, tiles_per_seq, rc, cc):
    i = pl.program_id(0)
    pad = halo_ref.shape[0]
    first = (i % tiles_per_seq) == 0
    ext_scr[0:pad, :] = jnp.where(first, 0.0, halo_ref[...])
    ext_scr[pad:, :] = u_ref[...]
    off = pad - (CONV_W - 1)
    span = rc + pad

    def chunk(r, carry):
        base = pl.multiple_of(r * rc, rc)
        for c0 in range(0, CONV_CH, cc):
            lanes = slice(c0, c0 + cc)
            window = ext_scr[pl.ds(base, span), lanes]
            acc = jnp.zeros((rc, cc), F32) + cb_ref[:, lanes]
            for res in range(8):
                shifted = window if res == 0 else pltpu.roll(window, span - res, axis=0)
                for k in range(CONV_W):
                    if (k + off) % 8 == res:
                        a8 = k + off - res
                        acc = acc + cw_ref[k:k + 1, lanes] * shifted[a8:a8 + rc]
            conv_scr[pl.ds(base, rc), lanes] = acc
        return carry

    lax.fori_loop(0, tm // rc, chunk, 0)
    conv_n = _ln_silu_rms(conv_scr[...], lng_ref, lnb_ref, gc_ref)
    attn_n = (_rms(oa_ref[...]) * ga_ref[...]).astype(BF16)
    half = D_MODEL // 2
    for c in range(2):
        cs = slice(c * half, (c + 1) * half)
        o_ref[:, cs] = x_ref[:, cs] + _dot(attn_n, woa_ref[:, cs]) + _dot(conv_n, woc_ref[:, cs])


def _mix_out_prompt(o_attn, u, x, mw, batch, tm=512, rc=128, cc=128):
    m = x.shape[0]
    t = m // batch
    tm = min(tm, t)
    pad = 32
    row = lambda w: pl.BlockSpec((tm, w), lambda i: (i, 0))
    halo = pl.BlockSpec((pad, CONV_CH), lambda i: (jnp.maximum(i * (tm // pad) - 1, 0), 0))
    return pl.pallas_call(
        functools.partial(_mix_out_prompt_body, tm=tm, tiles_per_seq=t // tm, rc=rc, cc=cc),
        grid=(m // tm,),
        in_specs=[row(ATTN_W), row(CONV_CH), halo, row(D_MODEL)] + [_const_spec(w.shape) for w in mw],
        out_specs=row(D_MODEL),
        out_shape=jax.ShapeDtypeStruct((m, D_MODEL), F32),
        scratch_shapes=[pltpu.VMEM((tm + pad, CONV_CH), F32), pltpu.VMEM((tm, CONV_CH), F32)],
        compiler_params=_params("arbitrary"),
        name="mix_out_prompt",
    )(o_attn, u, u, x, *mw)


def _mem_kv_body(mem_ref, g_ref, wk_ref, wv_ref, k_ref, v_ref):
    mb = (_rms(mem_ref[...]) * g_ref[...]).astype(BF16)
    k_ref[...] = _dot(mb, wk_ref[...])
    v_ref[...] = _dot(mb, wv_ref[...])


def _mem_kv(mem, g, wk, wv, tm=256):
    m = mem.shape[0]
    row = lambda w: pl.BlockSpec((tm, w), lambda i: (i, 0))
    return pl.pallas_call(
        _mem_kv_body,
        grid=(m // tm,),
        in_specs=[row(D_MODEL), _const_spec(g.shape), _const_spec(wk.shape), _const_spec(wv.shape)],
        out_specs=[row(MEM_W), row(MEM_W)],
        out_shape=[jax.ShapeDtypeStruct((m, MEM_W), F32)] * 2,
        compiler_params=_params("arbitrary"),
        name="mem_kv",
    )(mem, g, wk, wv)


def _mem_attn_core(q, mem_k, mem_v):
    outs = []
    for h in range(MEM_HEADS):
        s = _dot_nt(q[:, h * MEM_DH:(h + 1) * MEM_DH].astype(BF16), mem_k(h)) * MEM_SCALE
        outs.append(_softmax_av(s, mem_v(h)))
    return jnp.concatenate(outs, axis=1).astype(BF16)


def _mem_attn_prompt_body(x_ref, g_ref, wq_ref, mk_ref, mv_ref, wo_ref, o_ref):
    x = x_ref[...]
    hb = (_rms(x) * g_ref[...]).astype(BF16)
    head = lambda ref: lambda h: ref[0, :, h * MEM_DH:(h + 1) * MEM_DH].astype(BF16)
    a = _mem_attn_core(_dot(hb, wq_ref[...]), head(mk_ref), head(mv_ref))
    o_ref[...] = x + _dot(a, wo_ref[...])


def _mem_attn_prompt(x, g, wq, mk, mv, wo, batch, tm=512):
    m = x.shape[0]
    tiles_per_seq = m // batch // tm
    n_mem = mk.shape[1]
    row = pl.BlockSpec((tm, D_MODEL), lambda i: (i, 0))
    mem = pl.BlockSpec((1, n_mem, MEM_W), lambda i: (i // tiles_per_seq, 0, 0))
    return pl.pallas_call(
        _mem_attn_prompt_body,
        grid=(m // tm,),
        in_specs=[row, _const_spec(g.shape), _const_spec(wq.shape), mem, mem, _const_spec(wo.shape)],
        out_specs=row,
        out_shape=jax.ShapeDtypeStruct((m, D_MODEL), F32),
        compiler_params=_params("arbitrary"),
        name="mem_attn_prompt",
    )(x, g, wq, mk, mv, wo)


def _ffn_tail(f, nf, x_ref, acc_scr, gf_ref, o_ref):
    @pl.when(f == nf - 1)
    def _():
        o_ref[...] = _rms(x_ref[...] + acc_scr[...]) * gf_ref[...]


def _ffn_prompt_body(x_ref, g_ref, wg0_ref, wu0_ref, cw0_ref, cb0_ref, wd0_ref, wg1_ref, wu1_ref, cw1_ref, cb1_ref,
                     wd1_ref, gf_ref, o_ref, st0_ref, st1_ref, h_scr, acc_scr, carry_scr, gext_scr,
                     *, tm, tiles_per_seq, n_tiles):
    i = pl.program_id(0)
    f = pl.program_id(1)
    hist = carry_scr.shape[1]

    @pl.when(f == 0)
    def _():
        h_scr[...] = (_rms(x_ref[...]) * g_ref[...]).astype(BF16)
        acc_scr[...] = jnp.zeros(acc_scr.shape, F32)

    hb = h_scr[...]
    first = (i % tiles_per_seq) == 0

    def hidden(slot, tile, wg_ref, wu_ref, cw_ref, cb_ref, st_ref):
        gate = _dot(hb, wg_ref[...])
        gext_scr[slot, 0:hist, :] = jnp.where(first, 0.0, carry_scr[tile])
        gext_scr[slot, hist:, :] = gate
        tail = gate[tm - hist:, :]
        carry_scr[tile] = tail
        st_ref[0, 0] = tail
        conv = (cw_ref[0:1, :] * gext_scr[slot, hist - 2:hist - 2 + tm, :]
                + cw_ref[1:2, :] * gext_scr[slot, hist - 1:hist - 1 + tm, :] + cw_ref[2:3, :] * gate + cb_ref[...])
        return conv * jax.nn.sigmoid(conv) * _dot(hb, wu_ref[...])

    def down(slot, tile, refs):
        wg_ref, wu_ref, cw_ref, cb_ref, wd_ref, st_ref = refs
        return _dot(hidden(slot, tile, wg_ref, wu_ref, cw_ref, cb_ref, st_ref).astype(BF16), wd_ref[...])

    tile0 = (wg0_ref, wu0_ref, cw0_ref, cb0_ref, wd0_ref, st0_ref)
    tile1 = (wg1_ref, wu1_ref, cw1_ref, cb1_ref, wd1_ref, st1_ref)
    pair = 2 * f + 1 < n_tiles

    @pl.when(pair)
    def _():
        acc_scr[...] += down(0, 2 * f, tile0) + down(1, 2 * f + 1, tile1)

    @pl.when(jnp.logical_not(pair))
    def _():
        acc_scr[...] += down(0, 2 * f, tile0)
        st1_ref[...] = jnp.zeros(st1_ref.shape, F32)

    _ffn_tail(f, pl.num_programs(1), x_ref, acc_scr, gf_ref, o_ref)


def _ffn_prompt(x, g, wg, wu, cw, cb, wd, gf, batch, tm=512, tf=512):
    m = x.shape[0]
    t = m // batch
    tm = min(tm, t)
    n_tiles = D_FF // tf
    n_steps = -(-n_tiles // 2)
    hist = 8
    row = pl.BlockSpec((tm, D_MODEL), lambda i, f: (i, 0))
    tile_of = (lambda f: 2 * f, lambda f: jnp.minimum(2 * f + 1, n_tiles - 1))
    col = lambda r, s: pl.BlockSpec((r, tf), lambda i, f: (0, tile_of[s](f)))
    weights = lambda s: [col(D_MODEL, s), col(D_MODEL, s), col(cw.shape[0], s), col(1, s),
                         pl.BlockSpec((tf, D_MODEL), lambda i, f: (tile_of[s](f), 0))]
    st_spec = pl.BlockSpec((1, 1, hist, tf), lambda i, f: (i, f, 0, 0))
    st_shape = jax.ShapeDtypeStruct((m // tm, n_steps, hist, tf), F32)
    y, st0, st1 = pl.pallas_call(
        functools.partial(_ffn_prompt_body, tm=tm, tiles_per_seq=t // tm, n_tiles=n_tiles),
        grid=(m // tm, n_steps),
        in_specs=[row, _const_spec(g.shape)] + weights(0) + weights(1) + [_const_spec(gf.shape)],
        out_specs=[row, st_spec, st_spec],
        out_shape=[jax.ShapeDtypeStruct((m, D_MODEL), F32), st_shape, st_shape],
        scratch_shapes=[pltpu.VMEM((tm, D_MODEL), BF16), pltpu.VMEM((tm, D_MODEL), F32),
                        pltpu.VMEM((n_tiles, hist, tf), F32), pltpu.VMEM((2, tm + hist, tf), F32)],
        compiler_params=_params("arbitrary", "arbitrary"),
        name="ffn_prompt",
    )(x, g, wg, wu, cw, cb, wd, wg, wu, cw, cb, wd, gf)
    st = jnp.stack([st0, st1], axis=2).transpose(0, 3, 1, 2, 4).reshape(m // tm, hist, 2 * n_steps * tf)
    return y, st[:, :, :D_FF]


PAGE = 128
CHUNKS_PER_PAGE = PAGE // STRIDE_CMP


def _compress_sample_body(pt_ref, pool_ref, pe_ref, w1l_ref, w1t_ref, b1_ref, w2_ref, o_ref, buf, sem, *, n_pages):
    b = pl.program_id(0)
    slot = b % 2

    def fetch(bb, sl):
        def one(j, carry):
            page = pt_ref[bb, j]
            for k in range(N_KV):
                pltpu.make_async_copy(pool_ref.at[page, :, k, :],
                                      buf.at[sl, k, pl.ds(pl.multiple_of(j * PAGE, PAGE), PAGE), :],
                                      sem.at[sl]).start(priority=k % 2)
            return carry
        lax.fori_loop(0, n_pages, one, 0, unroll=8)

    @pl.when(b == 0)
    def _():
        fetch(0, 0)

    @pl.when(b + 1 < pl.num_programs(0))
    def _():
        fetch(b + 1, 1 - slot)

    pltpu.make_async_copy(buf.at[slot], buf.at[slot], sem.at[slot]).wait()
    n = n_pages * CHUNKS_PER_PAGE
    for k in range(N_KV):
        get = lambda l, k=k: buf[slot, k, pl.ds(l, n, stride=STRIDE_CMP), :]
        o_ref[0, k] = _compress_rows(get, n, pe_ref, w1l_ref, w1t_ref, b1_ref, w2_ref)


def _compress_sample(pool, page_table, cw):
    nb, n_pages = page_table.shape
    n = n_pages * CHUNKS_PER_PAGE
    grid_spec = pltpu.PrefetchScalarGridSpec(
        num_scalar_prefetch=1,
        grid=(nb,),
        in_specs=[pl.BlockSpec(memory_space=pl.ANY)] + [_const_spec(w.shape) for w in cw],
        out_specs=pl.BlockSpec((1, N_KV, n, DH), lambda b, pt: (b, 0, 0, 0)),
        scratch_shapes=[pltpu.VMEM((2, N_KV, n_pages * PAGE, DH), F32), pltpu.SemaphoreType.DMA((2,))],
    )
    return pl.pallas_call(
        functools.partial(_compress_sample_body, n_pages=n_pages),
        grid_spec=grid_spec,
        out_shape=jax.ShapeDtypeStruct((nb, N_KV, n, DH), F32),
        compiler_params=_params("arbitrary"),
        name="compress_sample",
    )(page_table, pool, *cw)


HEAD_ROWS = 8
TOKEN_ROWS = 16


def _cmp_select_sample_body(q_ref, kcc_ref, vcc_ref, oc_ref, idx_ref, *, past_len, n_blk):
    bb = q_ref.shape[0]
    n_cmp = kcc_ref.shape[2]
    rows = GQA * HEAD_ROWS
    q_pos = past_len + lax.broadcasted_iota(I32, (rows, 1), 0) % HEAD_ROWS
    cmp_end = lax.broadcasted_iota(I32, (1, n_cmp), 1) * STRIDE_CMP + (BLOCK_CMP - 1)
    cmask = cmp_end <= q_pos
    jobs = [(bi, k) for bi in range(bb) for k in range(N_KV)]
    scores = [jnp.where(cmask, _dot_nt(q_ref[bi, k], kcc_ref[bi, k].astype(BF16)), NEG) for bi, k in jobs]
    p_sums = []
    for s, (bi, k) in zip(scores, jobs):
        e = jnp.where(cmask, jnp.exp2(s - jnp.max(s, axis=-1, keepdims=True)), 0.0)
        p = e / jnp.maximum(jnp.sum(e, axis=-1, keepdims=True), 1e-30)
        oc_ref[bi, k] = _dot(p.astype(BF16), vcc_ref[bi, k].astype(BF16))
        p_sum = p[0:HEAD_ROWS]
        for g in range(1, GQA):
            p_sum = p_sum + p[g * HEAD_ROWS:(g + 1) * HEAD_ROWS]
        p_sums.append(p_sum)
    n_rows = bb * N_KV * HEAD_ROWS
    imp = _importance(jnp.concatenate(p_sums, axis=0), _overlap_matrix(n_cmp, n_blk))
    q_pos_r = past_len + lax.broadcasted_iota(I32, (n_rows, 1), 0) % HEAD_ROWS
    blk = lax.broadcasted_iota(I32, (1, n_blk), 1)
    cur = q_pos_r // SEL_BLOCK
    forced = (blk == 0) | (blk == cur) | (blk == cur - 1)
    score = jnp.where(blk * SEL_BLOCK <= q_pos_r, jnp.where(forced, BIG, imp), -BIG)
    lane = lax.broadcasted_iota(I32, (n_rows, n_blk), 1).astype(F32)
    out_lane = lax.broadcasted_iota(I32, (n_rows, LANES), 1)
    picks = jnp.zeros((n_rows, LANES), F32)
    for n in range(TOP_N):
        best = jnp.max(score, axis=-1, keepdims=True)
        pick = jnp.min(jnp.where(score == best, lane, float(n_blk)), axis=-1, keepdims=True)
        picks = jnp.where(out_lane == n, pick, picks)
        score = jnp.where(lane == pick, -3e38, score)
    picks = picks.astype(I32)
    for r in range(bb * N_KV):
        idx_ref[r // N_KV, r % N_KV] = picks[r * HEAD_ROWS:(r + 1) * HEAD_ROWS]


def _cmp_select_sample(q_hm, kcc, vcc, past_len, bb=8):
    nb = q_hm.shape[0]
    bb = min(bb, nb)
    assert nb % bb == 0
    n_cmp = kcc.shape[2]
    n_blk = -(-(past_len // SEL_BLOCK + 1) // LANES) * LANES
    rows = GQA * HEAD_ROWS
    spec = lambda r, w: pl.BlockSpec((bb, N_KV, r, w), lambda b: (b, 0, 0, 0))
    return pl.pallas_call(
        functools.partial(_cmp_select_sample_body, past_len=past_len, n_blk=n_blk),
        grid=(nb // bb,),
        in_specs=[spec(rows, DH), spec(n_cmp, DH), spec(n_cmp, DH)],
        out_specs=[spec(rows, DH), spec(HEAD_ROWS, LANES)],
        out_shape=[jax.ShapeDtypeStruct((nb, N_KV, rows, DH), F32),
                   jax.ShapeDtypeStruct((nb, N_KV, HEAD_ROWS, LANES), I32)],
        compiler_params=_params("arbitrary"),
        name="cmp_select_sample",
    )(q_hm, kcc, vcc)


def _attn_sample_body(pt_ref, idx_s_ref, q_ref, idx_v_ref, ksn_ref, vsn_ref, kwn_ref, vwn_ref, kwin_ref, vwin_ref,
                      oc_ref, gates_ref, kpool_ref, vpool_ref, o_ref, kbuf, vbuf, sem, *, dec_seq, n_pool_blk):
    step = pl.program_id(0)
    slot = step % 2
    bb = q_ref.shape[0]
    n_kt = N_KV * dec_seq
    assert PAGE == 2 * SEL_BLOCK

    def gather(st, sl):
        for row in range(bb * n_kt):
            bg = st * bb + row // n_kt
            kt = row % n_kt
            k = kt // dec_seq
            for n in range(TOP_N):
                blk = idx_s_ref[bg, kt * TOP_N + n]
                src = jnp.where(blk < n_pool_blk, blk, 0)
                page = pt_ref[bg, lax.shift_right_logical(src, 1)]
                rows = pl.ds(pl.multiple_of((src & 1) * SEL_BLOCK, SEL_BLOCK), SEL_BLOCK)
                dst = pl.ds(n * SEL_BLOCK, SEL_BLOCK)
                pltpu.make_async_copy(kpool_ref.at[page, rows, k], kbuf.at[sl, row, dst],
                                      sem.at[sl, 0]).start(priority=n % 2)
                pltpu.make_async_copy(vpool_ref.at[page, rows, k], vbuf.at[sl, row, dst],
                                      sem.at[sl, 1]).start(priority=(n + 1) % 2)

    @pl.when(step == 0)
    def _():
        gather(0, 0)

    @pl.when(step + 1 < pl.num_programs(0))
    def _():
        gather(step + 1, 1 - slot)

    pltpu.make_async_copy(kbuf.at[slot], kbuf.at[slot], sem.at[slot, 0]).wait()
    pltpu.make_async_copy(vbuf.at[slot], vbuf.at[slot], sem.at[slot, 1]).wait()

    n_keys = TOP_N * SEL_BLOCK
    rows = dec_seq * TOKEN_ROWS
    key_slot = lax.broadcasted_iota(I32, (LANES, n_keys), 1) // SEL_BLOCK
    expand = (lax.broadcasted_iota(I32, (LANES, n_keys), 0) == key_slot).astype(BF16)
    new_col = lax.broadcasted_iota(I32, (1, TOKEN_ROWS), 1)
    t_row = lax.broadcasted_iota(I32, (rows, 1), 0) // TOKEN_ROWS
    n_win = kwin_ref.shape[1]
    win_old_vis = lax.broadcasted_iota(I32, (1, n_win), 1) > t_row
    win_new_vis = new_col <= t_row
    heads = [(bi, k) for bi in range(bb) for k in range(N_KV)]
    jobs = []
    for bi, k in heads:
        qk = q_ref[bi, k]
        pool_ok = _dot((idx_v_ref[bi, k] < n_pool_blk).astype(BF16), expand)
        jobs.append((jnp.where(win_old_vis, _dot_nt(qk, kwin_ref[bi, :, k, :].astype(BF16)), NEG),
                     jnp.where(win_new_vis, _dot_nt(qk, kwn_ref[bi, k]), NEG),
                     lambda bi=bi, k=k: vwin_ref[bi, :, k, :].astype(BF16), vwn_ref[bi, k]))
        for t in range(dec_seq):
            qt = qk[t * TOKEN_ROWS:(t + 1) * TOKEN_ROWS]
            kt = bi * n_kt + k * dec_seq + t
            jobs.append((jnp.where(pool_ok[t:t + 1, :] > 0.5, _dot_nt(qt, kbuf[slot, kt].astype(BF16)), NEG),
                         jnp.where(new_col <= t, _dot_nt(qt, ksn_ref[bi, k]), NEG),
                         lambda kt=kt: vbuf[slot, kt].astype(BF16), vsn_ref[bi, k]))
    probs = []
    for s_a, s_b, _, _ in jobs:
        m = jnp.maximum(jnp.max(s_a, axis=-1, keepdims=True), jnp.max(s_b, axis=-1, keepdims=True))
        p_a, p_b = jnp.exp2(s_a - m), jnp.exp2(s_b - m)
        probs.append((p_a.astype(BF16), p_b.astype(BF16),
                      jnp.sum(p_a, axis=-1, keepdims=True) + jnp.sum(p_b, axis=-1, keepdims=True)))
    outs = [(_dot(p_a, v_a()) + _dot(p_b, v_b)) / l for (p_a, p_b, l), (_, _, v_a, v_b) in zip(probs, jobs)]
    for h, (bi, k) in enumerate(heads):
        o_win = outs[h * (dec_seq + 1)]
        gates = gates_ref[bi, k]
        for t in range(dec_seq):
            ts = slice(t * TOKEN_ROWS, (t + 1) * TOKEN_ROWS)
            gt = gates[ts]
            o_ref[bi, k, ts, :] = (gt[:, 0:1] * oc_ref[bi, k, ts, :] + gt[:, 1:2] * outs[h * (dec_seq + 1) + 1 + t]
                                   + gt[:, 2:3] * o_win[ts])


def _attn_sample(page_table, idx_flat, q_tm, idx_pad, ksn, vsn, kwn, vwn, k_win, v_win, oc_tm, gates_tm,
                 k_pool, v_pool, dec_seq, past_len):
    nb = q_tm.shape[0]
    bb = 1
    rows = dec_seq * TOKEN_ROWS
    n_win = k_win.shape[1]
    spec = lambda r, w: pl.BlockSpec((bb, N_KV, r, w), lambda b, *_: (b, 0, 0, 0))
    win = pl.BlockSpec((bb, n_win, N_KV, DH), lambda b, *_: (b, 0, 0, 0))
    any_spec = pl.BlockSpec(memory_space=pl.ANY)
    grid_spec = pltpu.PrefetchScalarGridSpec(
        num_scalar_prefetch=2,
        grid=(nb // bb,),
        in_specs=[spec(rows, DH), spec(TOKEN_ROWS, LANES), spec(TOKEN_ROWS, DH), spec(TOKEN_ROWS, DH),
                  spec(TOKEN_ROWS, DH), spec(TOKEN_ROWS, DH), win, win, spec(rows, DH), spec(rows, LANES),
                  any_spec, any_spec],
        out_specs=spec(rows, DH),
        scratch_shapes=[pltpu.VMEM((2, bb * N_KV * dec_seq, TOP_N * SEL_BLOCK, DH), F32),
                        pltpu.VMEM((2, bb * N_KV * dec_seq, TOP_N * SEL_BLOCK, DH), F32),
                        pltpu.SemaphoreType.DMA((2, 2))],
    )
    return pl.pallas_call(
        functools.partial(_attn_sample_body, dec_seq=dec_seq, n_pool_blk=past_len // SEL_BLOCK),
        grid_spec=grid_spec,
        out_shape=jax.ShapeDtypeStruct((nb, N_KV, rows, DH), F32),
        compiler_params=_params("arbitrary"),
        name="attn_sample",
    )(page_table, idx_flat, q_tm, idx_pad, ksn, vsn, kwn, vwn, k_win, v_win, oc_tm, gates_tm, k_pool, v_pool)


def _mix_out_sample_body(oa_ref, u_ref, st_ref, x_ref, cw_ref, cb_ref, lng_ref, lnb_ref, ga_ref, gc_ref,
                         woa_ref, woc_ref, gm_ref, wmq_ref, o_ref, qm_ref, conv_scr, *, nb, dec_seq):
    hist = CONV_W - 1

    def ext(j):
        if j < hist:
            return st_ref[j * nb:(j + 1) * nb, :]
        return u_ref[(j - hist) * nb:(j - hist + 1) * nb, :]

    for t in range(dec_seq):
        acc = jnp.zeros((nb, CONV_CH), F32) + cb_ref[...]
        for k in range(CONV_W):
            acc = acc + cw_ref[k:k + 1, :] * ext(t + k)
        conv_scr[t * nb:(t + 1) * nb, :] = acc
    conv_n = _ln_silu_rms(conv_scr[...], lng_ref, lnb_ref, gc_ref)
    attn_n = (_rms(oa_ref[...]) * ga_ref[...]).astype(BF16)
    x1 = x_ref[...] + _dot(attn_n, woa_ref[...]) + _dot(conv_n, woc_ref[...])
    o_ref[...] = x1
    qm_ref[...] = _dot((_rms(x1) * gm_ref[...]).astype(BF16), wmq_ref[...])


def _mix_out_sample(o_attn, u, conv_state, x, mw, g_mem, w_mq, nb, dec_seq):
    m = x.shape[0]
    args = (o_attn, u, conv_state, x) + tuple(mw) + (g_mem, w_mq)
    return pl.pallas_call(
        functools.partial(_mix_out_sample_body, nb=nb, dec_seq=dec_seq),
        grid=(1,),
        in_specs=[_const_spec(a.shape) for a in args],
        out_specs=[_const_spec((m, D_MODEL)), _const_spec((m, MEM_W))],
        out_shape=[jax.ShapeDtypeStruct((m, D_MODEL), F32), jax.ShapeDtypeStruct((m, MEM_W), F32)],
        scratch_shapes=[pltpu.VMEM((m, CONV_CH), F32)],
        compiler_params=_params("arbitrary"),
        name="mix_out_sample",
    )(*args)


def _mem_attn_sample_body(q_ref, mk_ref, mv_ref, o_ref):
    jobs = [(bi, h) for bi in range(q_ref.shape[0]) for h in range(MEM_HEADS)]
    scores = [_dot_nt(q_ref[bi, :, h * MEM_DH:(h + 1) * MEM_DH], mk_ref[bi, :, h, :].astype(BF16)) * MEM_SCALE
              for bi, h in jobs]
    probs = [jnp.exp(s - jnp.max(s, axis=-1, keepdims=True)) for s in scores]
    outs = [_dot(p.astype(BF16), mv_ref[bi, :, h, :].astype(BF16)) / jnp.sum(p, axis=-1, keepdims=True)
            for p, (bi, h) in zip(probs, jobs)]
    for bi in range(q_ref.shape[0]):
        o_ref[bi] = jnp.concatenate(outs[bi * MEM_HEADS:(bi + 1) * MEM_HEADS], axis=1).astype(BF16)


def _mem_attn_sample(q_pad, mk, mv, bb=4):
    nb, rows, _ = q_pad.shape
    bb = bb if nb % bb == 0 else 1
    n_mem = mk.shape[1]
    q_spec = pl.BlockSpec((bb, rows, MEM_W), lambda b: (b, 0, 0))
    mem = pl.BlockSpec((bb, n_mem, MEM_HEADS, MEM_DH), lambda b: (b, 0, 0, 0))
    return pl.pallas_call(
        _mem_attn_sample_body,
        grid=(nb // bb,),
        in_specs=[q_spec, mem, mem],
        out_specs=q_spec,
        out_shape=jax.ShapeDtypeStruct((nb, rows, MEM_W), BF16),
        compiler_params=_params("arbitrary"),
        name="mem_attn_sample",
    )(q_pad, mk, mv)


def _ffn_sample_body(x_ref, a_ref, wo_ref, g_ref, wg_ref, wu_ref, cw_ref, cb_ref, wd_ref, gf_ref, st_ref,
                     o_ref, sto_ref, x2_scr, h_scr, acc_scr, gate_scr, conv_scr, *, nb, dec_seq):
    f = pl.program_id(0)
    hist = FFN_CONV_W - 1

    @pl.when(f == 0)
    def _():
        x2 = x_ref[...] + _dot(a_ref[...], wo_ref[...])
        x2_scr[...] = x2
        h_scr[...] = (_rms(x2) * g_ref[...]).astype(BF16)
        acc_scr[...] = jnp.zeros(acc_scr.shape, F32)

    hb = h_scr[...]
    gate_scr[...] = _dot(hb, wg_ref[...])

    def ext(j):
        if j < hist:
            return st_ref[j * nb:(j + 1) * nb, :]
        return gate_scr[(j - hist) * nb:(j - hist + 1) * nb, :]

    for t in range(dec_seq):
        acc = cb_ref[...] + cw_ref[0:1, :] * ext(t)
        for k in range(1, FFN_CONV_W):
            acc = acc + cw_ref[k:k + 1, :] * ext(t + k)
        conv_scr[t * nb:(t + 1) * nb, :] = acc
    for j in range(hist):
        sto_ref[j * nb:(j + 1) * nb, :] = ext(dec_seq + j)
    conv = conv_scr[...]
    a = conv * jax.nn.sigmoid(conv) * _dot(hb, wu_ref[...])
    acc_scr[...] += _dot(a.astype(BF16), wd_ref[...])
    _ffn_tail(f, pl.num_programs(0), x2_scr, acc_scr, gf_ref, o_ref)


def _ffn_sample(x1, a, w_mo, fw, ffn_state, nb, dec_seq, tf=512):
    g, wg, wu, cw, cb, wd, gf = fw
    m = x1.shape[0]
    nf = D_FF // tf
    hist = FFN_CONV_W - 1
    col = lambda r: pl.BlockSpec((r, tf), lambda f: (0, f))
    full = lambda shape: pl.BlockSpec(shape, lambda f: (0,) * len(shape))
    return pl.pallas_call(
        functools.partial(_ffn_sample_body, nb=nb, dec_seq=dec_seq),
        grid=(nf,),
        in_specs=[full(x1.shape), full(a.shape), full(w_mo.shape), full(g.shape), col(D_MODEL), col(D_MODEL),
                  col(cw.shape[0]), col(1), pl.BlockSpec((tf, D_MODEL), lambda f: (f, 0)), full(gf.shape),
                  col(nb * hist)],
        out_specs=[full((m, D_MODEL)), col(nb * hist)],
        out_shape=[jax.ShapeDtypeStruct((m, D_MODEL), F32), jax.ShapeDtypeStruct((nb * hist, D_FF), F32)],
        scratch_shapes=[pltpu.VMEM((m, D_MODEL), F32), pltpu.VMEM((m, D_MODEL), BF16), pltpu.VMEM((m, D_MODEL), F32),
                        pltpu.VMEM((m, tf), F32), pltpu.VMEM((m, tf), F32)],
        compiler_params=_params("arbitrary"),
        name="ffn_sample",
    )(x1, a, w_mo, g, wg, wu, cw, cb, wd, gf, ffn_state)


def _prepare_weights(norm_mix_g, w_in, b_gate, cmp_k, cmp_v, conv_w, conv_b, conv_ln_g, conv_ln_b,
                     grp_norm_attn_g, grp_norm_conv_g, w_out, norm_mem_g, mem_norm_g, w_mq, w_mk, w_mv, w_mo,
                     norm_ffn_g, w_ffn_gate, w_ffn_up, ffn_conv_w, ffn_conv_b, w_ffn_down, norm_final_g):
    vec = lambda v: v.reshape(1, -1)
    kv_end = ATTN_W + 6 * KV_W
    n_gate_cols = N_KV * GQA * N_GATE
    per_kv = GQA * N_GATE
    wg = w_in[:, kv_end:kv_end + n_gate_cols].reshape(D_MODEL, N_KV, per_kv)
    wg = jnp.pad(wg, ((0, 0), (0, 0), (0, LANES - per_kv))).reshape(D_MODEL, N_KV * LANES)
    bg = jnp.pad(b_gate.reshape(N_KV, per_kv), ((0, 0), (0, LANES - per_kv))).reshape(1, N_KV * LANES)
    return dict(
        in_proj=(vec(norm_mix_g), w_in[:, :ATTN_W].astype(BF16), w_in[:, ATTN_W:kv_end].astype(BF16),
                 wg.astype(BF16), bg, w_in[:, kv_end + n_gate_cols:].astype(BF16)),
        cmp_k=_compress_weights(*cmp_k),
        cmp_v=_compress_weights(*cmp_v),
        mix=(jnp.pad(conv_w, ((0, 32 - CONV_W), (0, 0))), vec(conv_b), vec(conv_ln_g), vec(conv_ln_b),
             vec(grp_norm_attn_g), vec(grp_norm_conv_g), w_out[:ATTN_W].astype(BF16), w_out[ATTN_W:].astype(BF16)),
        mem_kv=(vec(mem_norm_g), w_mk.astype(BF16), w_mv.astype(BF16)),
        mem=(vec(norm_mem_g), w_mq.astype(BF16), w_mo.astype(BF16)),
        ffn=(vec(norm_ffn_g), w_ffn_gate.astype(BF16), w_ffn_up.astype(BF16),
             jnp.pad(ffn_conv_w, ((0, 8 - FFN_CONV_W), (0, 0))), vec(ffn_conv_b), w_ffn_down.astype(BF16),
             vec(norm_final_g)),
    )


def _prompt_forward(x_prompt, mem_prompt, w):
    batch, t, _ = x_prompt.shape
    x = x_prompt.reshape(batch * t, D_MODEL)
    (q, kc, vc, ks, vs, kw, vw, ksb, vsb, kwb, vwb, gates, u) = _in_proj(x, *w["in_proj"], tm=512)
    kcc = _compress_prompt(kc, w["cmp_k"], batch)
    vcc = _compress_prompt(vc, w["cmp_v"], batch)
    gates_t = gates.reshape(batch * t, N_KV, LANES)[:, :, :GATE_ROWS].transpose(1, 2, 0)
    o_attn = _nsa_prompt(q.T, kcc, vcc.transpose(0, 1, 3, 2), ksb, _values_with_ones(vsb), kwb, _values_with_ones(vwb),
                         gates_t.reshape(N_KV * GATE_ROWS, batch * t), batch)
    x1 = _mix_out_prompt(o_attn, u, x, w["mix"], batch)
    n_mem = mem_prompt.shape[1]
    mk, mv = _mem_kv(mem_prompt.reshape(batch * n_mem, D_MODEL), *w["mem_kv"])
    g_mem, w_mq, w_mo = w["mem"]
    x2 = _mem_attn_prompt(x1, g_mem, w_mq, mk.reshape(batch, n_mem, MEM_W), mv.reshape(batch, n_mem, MEM_W),
                          w_mo, batch)
    y, ffn_tail = _ffn_prompt(x2, *w["ffn"], batch=batch)
    kv5 = lambda a: a.reshape(1, batch, t, N_KV, DH)
    win = lambda a: a.reshape(batch, t, N_KV, DH)[None, :, t - min(WINDOW, t):]
    tiles = ffn_tail.shape[0] // batch
    new_ffn = ffn_tail.reshape(batch, tiles, ffn_tail.shape[1], D_FF)[:, -1, -(FFN_CONV_W - 1):]
    new_conv = u.reshape(batch, t, CONV_CH)[:, t - (CONV_W - 1):]
    mem5 = lambda a: a.reshape(1, batch, n_mem, MEM_HEADS, MEM_DH)
    return (y.reshape(batch, t, D_MODEL), kv5(kc), kv5(vc), kv5(ks), kv5(vs), win(kw), win(vw),
            new_conv[None], new_ffn[None], mem5(mk), mem5(mv))


def _pad_axis(a, axis, size):
    pads = [(0, 0)] * a.ndim
    pads[axis] = (0, size - a.shape[axis])
    return jnp.pad(a, pads)


def _sample_forward(x_sample, pools, k_win, v_win, conv_state, ffn_state, mem_k, mem_v, page_table, w):
    nb, dec_seq, _ = x_sample.shape
    m = nb * dec_seq
    past_len = page_table.shape[1] * PAGE
    assert dec_seq <= HEAD_ROWS and k_win.shape[1] == WINDOW
    x = x_sample.reshape(m, D_MODEL)
    (q, kc, vc, ks, vs, kw, vw, ksb, vsb, kwb, vwb, gates, u) = _in_proj(x, *w["in_proj"], tm=m)
    pool_kc, pool_vc, pool_ks, pool_vs = pools
    kcc = _compress_sample(pool_kc, page_table, w["cmp_k"])
    vcc = _compress_sample(pool_vc, page_table, w["cmp_v"])

    q5 = q.reshape(nb, dec_seq, N_KV, GQA, DH)
    q_hm = _pad_axis(q5.transpose(0, 2, 3, 1, 4), 3, HEAD_ROWS).reshape(nb, N_KV, GQA * HEAD_ROWS, DH)
    q_tm = _pad_axis(q5.transpose(0, 2, 1, 3, 4), 3, TOKEN_ROWS).reshape(nb, N_KV, dec_seq * TOKEN_ROWS, DH)
    oc_hm, idx = _cmp_select_sample(q_hm, kcc, vcc, past_len)
    oc_tm = oc_hm.reshape(nb, N_KV, GQA, HEAD_ROWS, DH)[:, :, :, :dec_seq].transpose(0, 1, 3, 2, 4)
    oc_tm = _pad_axis(oc_tm, 3, TOKEN_ROWS).reshape(nb, N_KV, dec_seq * TOKEN_ROWS, DH)
    idx_flat = idx[:, :, :dec_seq, :TOP_N].reshape(nb, N_KV * dec_seq * TOP_N)
    idx_pad = _pad_axis(idx, 2, TOKEN_ROWS)
    new_rows = lambda a: _pad_axis(a.reshape(nb, dec_seq, N_KV, DH).transpose(0, 2, 1, 3), 2, TOKEN_ROWS)
    gates_tm = gates.reshape(nb, dec_seq, N_KV, LANES)[..., :GQA * N_GATE].reshape(nb, dec_seq, N_KV, GQA, N_GATE)
    gates_tm = _pad_axis(_pad_axis(gates_tm.transpose(0, 2, 1, 3, 4), 3, TOKEN_ROWS), 4, LANES)
    gates_tm = gates_tm.reshape(nb, N_KV, dec_seq * TOKEN_ROWS, LANES)
    o_tm = _attn_sample(page_table, idx_flat, q_tm, idx_pad, new_rows(ksb), new_rows(vsb), new_rows(kwb),
                        new_rows(vwb), k_win, v_win,
                        oc_tm, gates_tm, pool_ks, pool_vs, dec_seq, past_len)
    o_attn = o_tm.reshape(nb, N_KV, dec_seq, TOKEN_ROWS, DH)[:, :, :, :GQA].transpose(2, 0, 1, 3, 4).reshape(m, ATTN_W)
    step_major = lambda a: a.reshape(nb, -1, a.shape[-1]).transpose(1, 0, 2).reshape(-1, a.shape[-1])
    batch_major = lambda a: a.reshape(-1, nb, a.shape[-1]).transpose(1, 0, 2)

    g_mem, w_mq, w_mo = w["mem"]
    x1, qm = _mix_out_sample(o_attn, step_major(u), step_major(conv_state), step_major(x), w["mix"], g_mem, w_mq,
                             nb, dec_seq)
    n_mem = mem_k.shape[1]
    q_pad = _pad_axis(batch_major(qm), 1, TOKEN_ROWS).astype(BF16)
    a = _mem_attn_sample(q_pad, mem_k, mem_v)
    y, new_ffn = _ffn_sample(x1, step_major(a[:, :dec_seq]), w_mo, w["ffn"], step_major(ffn_state), nb, dec_seq)

    kv5 = lambda a: a.reshape(1, nb, dec_seq, N_KV, DH)
    shift = lambda buf, new: jnp.concatenate([buf[:, dec_seq:], new.reshape((nb, dec_seq) + buf.shape[2:])], axis=1)[None]
    return (batch_major(y), kv5(kc), kv5(vc), kv5(ks), kv5(vs), shift(k_win, kw), shift(v_win, vw),
            shift(conv_state, u), batch_major(new_ffn)[None])


def kernel(x_prompt, x_sample, cache_k_cmp, cache_v_cmp, cache_k_sel, cache_v_sel, cache_k_win, cache_v_win,
           state_conv, state_ffn_conv, cache_mem_k, cache_mem_v, page_table, mem_prompt,
           norm_mix_g, w_in, b_gate, cmp_k_pe, cmp_k_w1, cmp_k_b1, cmp_k_w2, cmp_v_pe, cmp_v_w1, cmp_v_b1, cmp_v_w2,
           conv_w, conv_b, conv_ln_g, conv_ln_b, grp_norm_attn_g, grp_norm_conv_g, w_out,
           norm_mem_g, mem_norm_g, w_mq, w_mk, w_mv, w_mo,
           norm_ffn_g, w_ffn_gate, w_ffn_up, ffn_conv_w, ffn_conv_b, w_ffn_down, norm_final_g):
    assert w_in.shape[0] == 1, "single-layer step"
    w = _prepare_weights(norm_mix_g[0], w_in[0], b_gate[0],
                         (cmp_k_pe[0], cmp_k_w1[0], cmp_k_b1[0], cmp_k_w2[0]),
                         (cmp_v_pe[0], cmp_v_w1[0], cmp_v_b1[0], cmp_v_w2[0]),
                         conv_w[0], conv_b[0], conv_ln_g[0], conv_ln_b[0], grp_norm_attn_g[0], grp_norm_conv_g[0],
                         w_out[0], norm_mem_g[0], mem_norm_g[0], w_mq[0], w_mk[0], w_mv[0], w_mo[0],
                         norm_ffn_g[0], w_ffn_gate[0], w_ffn_up[0], ffn_conv_w[0], ffn_conv_b[0], w_ffn_down[0],
                         norm_final_g)
    p = _prompt_forward(x_prompt, mem_prompt, w)
    s = _sample_forward(x_sample, (cache_k_cmp[0], cache_v_cmp[0], cache_k_sel[0], cache_v_sel[0]),
                        cache_k_win[0], cache_v_win[0], state_conv[0], state_ffn_conv[0],
                        cache_mem_k[0], cache_mem_v[0], page_table, w)
    return (p[0], s[0]) + p[1:] + s[1:]
```
